```python
import jax, jax.numpy as jnp
from jax import lax
import numpy as np

D_MODEL = 2048
BATCH = 16
SEQ = 2048
DEPTH = 2

CHUNK = 64
D_CONV = D_MODEL // 2
D_RET = D_MODEL - D_CONV
RET_HEAD_DIM = 128
RET_HEADS = D_RET // RET_HEAD_DIM
CONV_WIDTH = 3
D_IN_PROJ = 3 * D_CONV + 4 * D_RET
ROPE_BASE = 10000.0
N_GROUPS = 4
EXPERTS_PER_GROUP = 8
N_EXPERTS = N_GROUPS * EXPERTS_PER_GROUP
TOP_K = 2
D_EXPERT = D_MODEL // 2
MOE_BLOCK = 128
EPS = 1e-6

kernel_name = "hybrid_conv_retention_hmoe_trunk"


def rmsnorm(x, g):
    x32 = x.astype(jnp.float32)
    y = x32 * lax.rsqrt(jnp.mean(x32 * x32, axis=-1, keepdims=True) + EPS)
    return (y * g.astype(jnp.float32)).astype(x.dtype)


def head_rmsnorm(x):
    x32 = x.astype(jnp.float32)
    return x32 * lax.rsqrt(jnp.mean(x32 * x32, axis=-1, keepdims=True) + EPS)


def rotary(x, pos):
    half = x.shape[-1] // 2
    inv = ROPE_BASE ** (-jnp.arange(half, dtype=jnp.float32) / half)
    ang = pos[:, None] * inv[None, :]
    cos = jnp.cos(ang)[None, :, None, :]
    sin = jnp.sin(ang)[None, :, None, :]
    x32 = x.astype(jnp.float32)
    x1, x2 = x32[..., :half], x32[..., half:]
    return jnp.concatenate([x1 * cos - x2 * sin, x2 * cos + x1 * sin], axis=-1)


def retention(q, k, v):
    bsz, seq, nh, dh = q.shape
    n_chunks = seq // CHUNK
    log_g = jnp.log1p(-jnp.exp2(-5.0 - jnp.arange(nh, dtype=jnp.float32)))
    idx = jnp.arange(CHUNK, dtype=jnp.float32)
    d_intra = jnp.exp(log_g[:, None, None] * jnp.abs(idx[:, None] - idx[None, :]))
    q_dec = jnp.exp(log_g[None, :] * (idx[:, None] + 1.0))
    k_dec = jnp.exp(log_g[None, :] * (CHUNK - 1.0 - idx[:, None]))
    chunk_dec = jnp.exp(log_g * CHUNK)

    qc = q.reshape(bsz, n_chunks, CHUNK, nh, dh)
    kc = k.reshape(bsz, n_chunks, CHUNK, nh, dh)
    vc = v.reshape(bsz, n_chunks, CHUNK, nh, dh)

    s = jnp.einsum('bnihd,bnjhd->bnhij', qc, kc) * d_intra[None, None]
    o_intra = jnp.einsum('bnhij,bnjhe->bnihe', s, vc)

    def step(state, inp):
        qn, kn, vn = inp
        cross = jnp.einsum('bihd,bhde->bihe', qn * q_dec[None, :, :, None], state)
        state = state * chunk_dec[None, :, None, None] + jnp.einsum(
            'bjhd,bjhe->bhde', kn * k_dec[None, :, :, None], vn)
        return state, cross

    r0 = jnp.zeros((bsz, nh, dh, dh), jnp.float32)
    xs = (qc.transpose(1, 0, 2, 3, 4), kc.transpose(1, 0, 2, 3, 4), vc.transpose(1, 0, 2, 3, 4))
    _, cross = lax.scan(step, r0, xs)
    out = o_intra + cross.transpose(1, 0, 2, 3, 4)
    return out.reshape(bsz, seq, nh, dh)


def token_mixer(h, w_in, conv_w, w_out):
    bsz, seq, _ = h.shape
    z = h @ w_in
    b_g, c_g, v_c, q, k, v, g = jnp.split(
        z, [D_CONV, 2 * D_CONV, 3 * D_CONV, 3 * D_CONV + D_RET,
            3 * D_CONV + 2 * D_RET, 3 * D_CONV + 3 * D_RET], axis=-1)

    u = c_g * v_c
    y_conv = lax.conv_general_dilated(
        u, conv_w[:, None, :], window_strides=(1,), padding=[(CONV_WIDTH - 1, 0)],
        dimension_numbers=('NWC', 'WIO', 'NWC'), feature_group_count=D_CONV)
    conv_out = b_g * y_conv

    pos = jnp.arange(seq, dtype=jnp.float32)
    qh = rotary(q.reshape(bsz, seq, RET_HEADS, RET_HEAD_DIM), pos)
    kh = rotary(k.reshape(bsz, seq, RET_HEADS, RET_HEAD_DIM), pos) * (RET_HEAD_DIM ** -0.5)
    vh = v.reshape(bsz, seq, RET_HEADS, RET_HEAD_DIM).astype(jnp.float32)
    o = head_rmsnorm(retention(qh, kh, vh)).reshape(bsz, seq, D_RET)
    ret_out = (o * jax.nn.silu(g.astype(jnp.float32))).astype(h.dtype)

    return jnp.concatenate([conv_out, ret_out], axis=-1) @ w_out


def hierarchical_moe(h, w_group, b_group, w_router, b_router, w1, w3, w2):
    bsz, seq, dm = h.shape
    n_tok = bsz * seq
    xt = h.reshape(n_tok, dm)

    gl = (xt @ w_group).astype(jnp.float32) + b_group.astype(jnp.float32)
    g_sel = jnp.argmax(gl, axis=-1)
    p_g = jnp.take_along_axis(jax.nn.softmax(gl, axis=-1), g_sel[:, None], axis=1)[:, 0]
    el = ((xt @ w_router).astype(jnp.float32) + b_router.astype(jnp.float32)).reshape(
        n_tok, N_GROUPS, EXPERTS_PER_GROUP)
    el_sel = jnp.take_along_axis(el, g_sel[:, None, None], axis=1)[:, 0]
    top_v, top_i = lax.top_k(el_sel, TOP_K)
    gate = p_g[:, None] * jax.nn.softmax(top_v, axis=-1)
    expert = g_sel[:, None] * EXPERTS_PER_GROUP + top_i

    n_assign = n_tok * TOP_K
    e_flat = expert.reshape(n_assign).astype(jnp.int32)
    w_flat = gate.reshape(n_assign)
    tok = jnp.repeat(jnp.arange(n_tok, dtype=jnp.int32), TOP_K)
    order = jnp.argsort(e_flat)
    e_s, tok_s, w_s = e_flat[order], tok[order], w_flat[order]
    counts = jnp.bincount(e_flat, length=N_EXPERTS)
    starts = jnp.cumsum(counts) - counts
    pcounts = (counts + MOE_BLOCK - 1) // MOE_BLOCK * MOE_BLOCK
    pends = jnp.cumsum(pcounts)
    pstarts = pends - pcounts
    dest = pstarts[e_s] + jnp.arange(n_assign, dtype=jnp.int32) - starts[e_s]
    n_blocks = -(-n_assign // MOE_BLOCK) + N_EXPERTS
    n_rows = n_blocks * MOE_BLOCK
    row_tok = jnp.full((n_rows,), n_tok, jnp.int32).at[dest].set(tok_s)
    row_w = jnp.zeros((n_rows,), jnp.float32).at[dest].set(w_s)
    blk_e = jnp.minimum(
        jnp.searchsorted(pends, jnp.arange(n_blocks, dtype=jnp.int32) * MOE_BLOCK, side='right'),
        N_EXPERTS - 1)
    xpad = jnp.concatenate([xt, jnp.zeros((1, dm), xt.dtype)], axis=0)

    def block_fn(args):
        idx, e = args
        xb = xpad[idx]
        hid = jax.nn.silu(xb @ w1[e]) * (xb @ w3[e])
        return hid @ w2[e]

    yb = lax.map(block_fn, (row_tok.reshape(n_blocks, MOE_BLOCK), blk_e))
    yb = yb.reshape(n_rows, dm) * row_w[:, None].astype(yb.dtype)
    y = jnp.zeros((n_tok + 1, dm), yb.dtype).at[row_tok].add(yb)
    return y[:n_tok].reshape(bsz, seq, dm)


def setup_inputs(seed: int = 0) -> dict:
    key = jax.random.key(seed)
    ks = jax.random.split(key, 14)
    f32 = jnp.float32
    nrm = lambda k, shape, scale: jax.random.normal(k, shape, f32) * scale
    return {
        "x": nrm(ks[0], (BATCH, SEQ, D_MODEL), 1.0),
        "norm_mix_g": 1.0 + nrm(ks[1], (DEPTH, D_MODEL), 0.05),
        "w_in": nrm(ks[2], (DEPTH, D_MODEL, D_IN_PROJ), D_MODEL ** -0.5),
        "conv_w": nrm(ks[3], (DEPTH, CONV_WIDTH, D_CONV), CONV_WIDTH ** -0.5),
        "w_out": nrm(ks[4], (DEPTH, D_CONV + D_RET, D_MODEL), (D_CONV + D_RET) ** -0.5),
        "norm_ffn_g": 1.0 + nrm(ks[5], (DEPTH, D_MODEL), 0.05),
        "w_group": nrm(ks[6], (DEPTH, D_MODEL, N_GROUPS), D_MODEL ** -0.5),
        "b_group": nrm(ks[7], (DEPTH, N_GROUPS), 0.01),
        "w_router": nrm(ks[8], (DEPTH, D_MODEL, N_EXPERTS), D_MODEL ** -0.5),
        "b_router": nrm(ks[9], (DEPTH, N_EXPERTS), 0.01),
        "w_expert_gate": nrm(ks[10], (DEPTH, N_EXPERTS, D_MODEL, D_EXPERT), D_MODEL ** -0.5),
        "w_expert_up": nrm(ks[11], (DEPTH, N_EXPERTS, D_MODEL, D_EXPERT), D_MODEL ** -0.5),
        "w_expert_down": nrm(ks[12], (DEPTH, N_EXPERTS, D_EXPERT, D_MODEL), D_EXPERT ** -0.5),
        "final_norm_g": 1.0 + nrm(ks[13], (D_MODEL,), 0.05),
    }


def reference(x, norm_mix_g, w_in, conv_w, w_out, norm_ffn_g, w_group, b_group,
              w_router, b_router, w_expert_gate, w_expert_up, w_expert_down, final_norm_g):
    for l in range(DEPTH):
        h = rmsnorm(x, norm_mix_g[l])
        x = x + token_mixer(h, w_in[l], conv_w[l], w_out[l])
        h = rmsnorm(x, norm_ffn_g[l])
        x = x + hierarchical_moe(h, w_group[l], b_group[l], w_router[l], b_router[l],
                                 w_expert_gate[l], w_expert_up[l], w_expert_down[l])
    return rmsnorm(x, final_norm_g)
```

```python
import functools

import jax
import jax.numpy as jnp
from jax import lax
from jax.experimental import pallas as pl
from jax.experimental.pallas import tpu as pltpu

CHUNK = 64
RET_HEAD_DIM = 128
CONV_WIDTH = 3
ROPE_BASE = 10000.0
N_GROUPS = 4
EXPERTS_PER_GROUP = 8
N_EXPERTS = N_GROUPS * EXPERTS_PER_GROUP
TOP_K = 2
EPS = 1e-6

LANES = 128
SUBLANES = 8
VMEM_LIMIT = 56 * 1024 * 1024

F32 = jnp.float32
BF16 = jnp.bfloat16


def _tile(n, want):
    t = min(n, want)
    while n % t:
        t //= 2
    return t


def _rms(x, g):
    return x * lax.rsqrt(jnp.mean(x * x, axis=-1, keepdims=True) + EPS) * g


def _in_proj_kernel(x_ref, g_ref, w_ref, o_ref, h_scr):
    @pl.when(pl.program_id(1) == 0)
    def _():
        h_scr[...] = _rms(x_ref[...], g_ref[...]).astype(BF16)

    o_ref[...] = jnp.dot(h_scr[...], w_ref[...], preferred_element_type=F32)


def _in_proj(x2, g, w_bf16):
    n_tok, dm = x2.shape
    n_out = w_bf16.shape[1]
    tm = _tile(n_tok, 512)
    tn = _tile(n_out, 1024)
    return pl.pallas_call(
        _in_proj_kernel,
        grid=(n_tok // tm, n_out // tn),
        in_specs=[
            pl.BlockSpec((tm, dm), lambda i, j: (i, 0)),
            pl.BlockSpec((1, dm), lambda i, j: (0, 0)),
            pl.BlockSpec((dm, tn), lambda i, j: (0, j)),
        ],
        out_specs=pl.BlockSpec((tm, tn), lambda i, j: (i, j)),
        out_shape=jax.ShapeDtypeStruct((n_tok, n_out), F32),
        scratch_shapes=[pltpu.VMEM((tm, dm), BF16)],
        compiler_params=pltpu.CompilerParams(
            dimension_semantics=("parallel", "arbitrary"), vmem_limit_bytes=VMEM_LIMIT),
        name="in_proj",
    )(x2, g.reshape(1, dm), w_bf16)


def _mixer_kernel(z_ref, x_ref, wout_ref, convw_ref, cos_ref, sin_ref, dmask_ref, qdec_ref,
                  kdec_ref, sdec_ref, o_ref, state_scr, u_scr, mixed_scr, *, ts, d_conv, n_heads):
    dh = RET_HEAD_DIM
    d_ret = n_heads * dh

    @pl.when(pl.program_id(1) == 0)
    def _():
        state_scr[...] = jnp.zeros_like(state_scr)
        u_scr[0:SUBLANES, :] = jnp.zeros((SUBLANES, d_conv), F32)

    u_scr[SUBLANES:SUBLANES + ts, :] = z_ref[:, d_conv:2 * d_conv] * z_ref[:, 2 * d_conv:3 * d_conv]
    y = (convw_ref[2:3, :] * u_scr[SUBLANES:SUBLANES + ts, :]
         + convw_ref[1:2, :] * u_scr[SUBLANES - 1:SUBLANES - 1 + ts, :]
         + convw_ref[0:1, :] * u_scr[SUBLANES - 2:SUBLANES - 2 + ts, :])
    mixed_scr[:, 0:d_conv] = (z_ref[:, 0:d_conv] * y).astype(BF16)
    u_scr[0:SUBLANES, :] = u_scr[ts:ts + SUBLANES, :]

    cos = cos_ref[...]
    sin = sin_ref[...]
    scale = RET_HEAD_DIM ** -0.5
    base = 3 * d_conv
    for h in range(n_heads):
        c0 = h * dh
        q = z_ref[:, base + c0:base + c0 + dh]
        k = z_ref[:, base + d_ret + c0:base + d_ret + c0 + dh]
        v = z_ref[:, base + 2 * d_ret + c0:base + 2 * d_ret + c0 + dh].astype(BF16)
        g = z_ref[:, base + 3 * d_ret + c0:base + 3 * d_ret + c0 + dh]
        qr = q * cos + pltpu.roll(q, dh // 2, 1) * sin
        kr = (k * cos + pltpu.roll(k, dh // 2, 1) * sin) * scale
        s = lax.dot_general(qr.astype(BF16), kr.astype(BF16), (((1,), (1,)), ((), ())),
                            preferred_element_type=F32) * dmask_ref[h]
        o = jnp.dot(s.astype(BF16), v, preferred_element_type=F32)
        st = state_scr[h]
        o = o + jnp.dot((qr * qdec_ref[:, c0:c0 + dh]).astype(BF16), st.astype(BF16),
                        preferred_element_type=F32)
        kv = lax.dot_general((kr * kdec_ref[:, c0:c0 + dh]).astype(BF16), v,
                             (((0,), (0,)), ((), ())), preferred_element_type=F32)
        state_scr[h] = st * sdec_ref[:, c0:c0 + dh] + kv
        on = o * lax.rsqrt(jnp.mean(o * o, axis=-1, keepdims=True) + EPS)
        mixed_scr[:, d_conv + c0:d_conv + c0 + dh] = (on * (g * jax.nn.sigmoid(g))).astype(BF16)

    o_ref[...] = x_ref[...] + jnp.dot(mixed_scr[...], wout_ref[...], preferred_element_type=F32)


def _retention_tables(seq, ts, n_heads):
    dh = RET_HEAD_DIM
    half = dh // 2
    pos = jnp.arange(seq, dtype=F32)
    inv = ROPE_BASE ** (-jnp.arange(half, dtype=F32) / half)
    ang = pos[:, None] * inv[None, :]
    cos = jnp.cos(ang)
    sin = jnp.sin(ang)
    cos_full = jnp.concatenate([cos, cos], axis=-1)
    sin_signed = jnp.concatenate([-sin, sin], axis=-1)
    log_g = jnp.log1p(-jnp.exp2(-5.0 - jnp.arange(n_heads, dtype=F32)))
    idx = jnp.arange(ts, dtype=F32)
    dist = jnp.abs(idx[:, None] - idx[None, :])
    chunk_id = jnp.arange(ts) // CHUNK
    visible = chunk_id[None, :] <= chunk_id[:, None]
    dmask = jnp.where(visible[None], jnp.exp(log_g[:, None, None] * dist[None]), 0.0)
    rep = lambda a: jnp.repeat(a, dh, axis=-1)
    qdec = rep(jnp.exp(log_g[None, :] * (idx[:, None] + 1.0)))
    kdec = rep(jnp.exp(log_g[None, :] * (ts - 1.0 - idx[:, None])))
    sdec = rep(jnp.exp(log_g * ts)[None, :])
    return cos_full, sin_signed, dmask.astype(F32), qdec, kdec, sdec


def _mixer(z, x2, wout_bf16, conv_w, bsz, seq):
    n_tok, dm = x2.shape
    d_conv = conv_w.shape[1]
    d_ret = dm - d_conv
    n_heads = d_ret // RET_HEAD_DIM
    d_in = z.shape[1]
    ts = _tile(seq, 256)
    ns = seq // ts
    cos_full, sin_signed, dmask, qdec, kdec, sdec = _retention_tables(seq, ts, n_heads)
    kern = functools.partial(_mixer_kernel, ts=ts, d_conv=d_conv, n_heads=n_heads)
    const2 = lambda b, s: (0, 0)
    return pl.pallas_call(
        kern,
        grid=(bsz, ns),
        in_specs=[
            pl.BlockSpec((ts, d_in), lambda b, s: (b * ns + s, 0)),
            pl.BlockSpec((ts, dm), lambda b, s: (b * ns + s, 0)),
            pl.BlockSpec((dm, dm), const2),
            pl.BlockSpec((CONV_WIDTH, d_conv), const2),
            pl.BlockSpec((ts, RET_HEAD_DIM), lambda b, s: (s, 0)),
            pl.BlockSpec((ts, RET_HEAD_DIM), lambda b, s: (s, 0)),
            pl.BlockSpec((n_heads, ts, ts), lambda b, s: (0, 0, 0)),
            pl.BlockSpec((ts, d_ret), const2),
            pl.BlockSpec((ts, d_ret), const2),
            pl.BlockSpec((1, d_ret), const2),
        ],
        out_specs=pl.BlockSpec((ts, dm), lambda b, s: (b * ns + s, 0)),
        out_shape=jax.ShapeDtypeStruct((n_tok, dm), F32),
        scratch_shapes=[
            pltpu.VMEM((n_heads, RET_HEAD_DIM, RET_HEAD_DIM), F32),
            pltpu.VMEM((ts + SUBLANES, d_conv), F32),
            pltpu.VMEM((ts, dm), BF16),
        ],
        compiler_params=pltpu.CompilerParams(
            dimension_semantics=("parallel", "arbitrary"), vmem_limit_bytes=VMEM_LIMIT),
        name="mixer",
    )(z, x2, wout_bf16, conv_w, cos_full, sin_signed, dmask, qdec, kdec, sdec)


def _router_kernel(x_ref, g_ref, wr_ref, br_ref, route_ref):
    h = _rms(x_ref[...], g_ref[...]).astype(BF16)
    logits = jnp.dot(h, wr_ref[...], preferred_element_type=F32) + br_ref[...]
    lane = lax.broadcasted_iota(jnp.int32, logits.shape, 1)
    neg = jnp.float32(-jnp.inf)
    big = jnp.int32(LANES)

    gl = jnp.where(lane < N_GROUPS, logits, neg)
    gmax = jnp.max(gl, axis=-1, keepdims=True)
    g_sel = jnp.min(jnp.where(gl == gmax, lane, big), axis=-1, keepdims=True)
    p_g = 1.0 / jnp.sum(jnp.exp(gl - gmax), axis=-1, keepdims=True)

    lo = N_GROUPS + g_sel * EXPERTS_PER_GROUP
    el = jnp.where(lane >= lo, jnp.where(lane < lo + EXPERTS_PER_GROUP, logits, neg), neg)
    v1 = jnp.max(el, axis=-1, keepdims=True)
    i1 = jnp.min(jnp.where(el == v1, lane, big), axis=-1, keepdims=True)
    el2 = jnp.where(lane == i1, neg, el)
    v2 = jnp.max(el2, axis=-1, keepdims=True)
    i2 = jnp.min(jnp.where(el2 == v2, lane, big), axis=-1, keepdims=True)
    t = jnp.exp(v2 - v1)
    gate1 = p_g / (1.0 + t)
    gate2 = p_g * t / (1.0 + t)

    e1 = (i1 - N_GROUPS).astype(F32)
    e2 = (i2 - N_GROUPS).astype(F32)
    route_ref[...] = jnp.where(lane == 0, e1, jnp.where(lane == 1, e2, jnp.where(
        lane == 2, gate1, jnp.where(lane == 3, gate2, 0.0))))


def _router(x2, g, w_group, b_group, w_router, b_router):
    n_tok, dm = x2.shape
    n_logit = N_GROUPS + N_EXPERTS
    wr = jnp.zeros((dm, LANES), F32).at[:, :N_GROUPS].set(w_group).at[:, N_GROUPS:n_logit].set(w_router)
    br = jnp.zeros((1, LANES), F32).at[0, :N_GROUPS].set(b_group).at[0, N_GROUPS:n_logit].set(b_router)
    tm = _tile(n_tok, 512)
    return pl.pallas_call(
        _router_kernel,
        grid=(n_tok // tm,),
        in_specs=[
            pl.BlockSpec((tm, dm), lambda i: (i, 0)),
            pl.BlockSpec((1, dm), lambda i: (0, 0)),
            pl.BlockSpec((dm, LANES), lambda i: (0, 0)),
            pl.BlockSpec((1, LANES), lambda i: (0, 0)),
        ],
        out_specs=pl.BlockSpec((tm, LANES), lambda i: (i, 0)),
        out_shape=jax.ShapeDtypeStruct((n_tok, LANES), F32),
        compiler_params=pltpu.CompilerParams(
            dimension_semantics=("parallel",), vmem_limit_bytes=VMEM_LIMIT),
        name="router",
    )(x2, g.reshape(1, dm), wr.astype(BF16), br)


def _dispatch_kernel(row_tok_ref, x_hbm, o_ref, sem, *, rb):
    base = pl.program_id(0) * rb

    def issue(r, carry):
        tok = row_tok_ref[base + r]
        pltpu.make_async_copy(x_hbm.at[pl.ds(tok, 1)], o_ref.at[pl.ds(r, 1)], sem).start()
        return carry

    lax.fori_loop(0, rb, issue, 0)
    pltpu.make_async_copy(x_hbm.at[pl.ds(0, rb)], o_ref, sem).wait()


def _dispatch(x2, row_tok, rb):
    n_rows = row_tok.shape[0]
    dm = x2.shape[1]
    return pl.pallas_call(
        functools.partial(_dispatch_kernel, rb=rb),
        grid_spec=pltpu.PrefetchScalarGridSpec(
            num_scalar_prefetch=1,
            grid=(n_rows // rb,),
            in_specs=[pl.BlockSpec(memory_space=pl.ANY)],
            out_specs=pl.BlockSpec((rb, dm), lambda i, rt: (i, 0)),
            scratch_shapes=[pltpu.SemaphoreType.DMA],
        ),
        out_shape=jax.ShapeDtypeStruct((n_rows, dm), F32),
        compiler_params=pltpu.CompilerParams(
            dimension_semantics=("arbitrary",), vmem_limit_bytes=VMEM_LIMIT),
        name="dispatch",
    )(row_tok, x2)


def _experts_kernel(blk_e_ref, n_used_ref, xs_ref, g_ref, w1_ref, w3_ref, w2_ref, o_ref):
    del blk_e_ref
    i = pl.program_id(0)

    @pl.when(i < n_used_ref[0])
    def _():
        h = _rms(xs_ref[...], g_ref[...]).astype(BF16)
        a = jnp.dot(h, w1_ref[...], preferred_element_type=F32)
        b = jnp.dot(h, w3_ref[...], preferred_element_type=F32)
        hid = (a * jax.nn.sigmoid(a) * b).astype(BF16)
        o_ref[...] = jnp.dot(hid, w2_ref[...], preferred_element_type=F32)

    @pl.when(i >= n_used_ref[0])
    def _():
        o_ref[...] = jnp.zeros_like(o_ref)


def _experts(xs, g, w1, w3, w2, blk_e, n_used, tmb):
    n_rows, dm = xs.shape
    de = w1.shape[2]
    nb = n_rows // tmb
    return pl.pallas_call(
        _experts_kernel,
        grid_spec=pltpu.PrefetchScalarGridSpec(
            num_scalar_prefetch=2,
            grid=(nb,),
            in_specs=[
                pl.BlockSpec((tmb, dm), lambda i, be, nu: (jnp.minimum(i, nu[0] - 1), 0)),
                pl.BlockSpec((1, dm), lambda i, be, nu: (0, 0)),
                pl.BlockSpec((None, dm, de), lambda i, be, nu: (be[i], 0, 0)),
                pl.BlockSpec((None, dm, de), lambda i, be, nu: (be[i], 0, 0)),
                pl.BlockSpec((None, de, dm), lambda i, be, nu: (be[i], 0, 0)),
            ],
            out_specs=pl.BlockSpec((tmb, dm), lambda i, be, nu: (i, 0)),
        ),
        out_shape=jax.ShapeDtypeStruct((n_rows, dm), F32),
        compiler_params=pltpu.CompilerParams(
            dimension_semantics=("arbitrary",), vmem_limit_bytes=VMEM_LIMIT),
        name="experts",
    )(blk_e, n_used, xs, g.reshape(1, dm), w1, w3, w2)


def _combine_kernel(dest_ref, x_ref, route_ref, gf_ref, yb_hbm, o_ref, buf, sem, *, tm, final):
    base = pl.program_id(0) * tm

    def issue(r, carry):
        a = (base + r) * TOP_K
        for k in range(TOP_K):
            pltpu.make_async_copy(yb_hbm.at[pl.ds(dest_ref[a + k], 1)],
                                  buf.at[k, pl.ds(r, 1)], sem.at[k]).start()
        return carry

    lax.fori_loop(0, tm, issue, 0)
    for k in range(TOP_K):
        pltpu.make_async_copy(yb_hbm.at[pl.ds(0, tm)], buf.at[k], sem.at[k]).wait()
    out = x_ref[...] + route_ref[:, 2:3] * buf[0] + route_ref[:, 3:4] * buf[1]
    if final:
        out = _rms(out, gf_ref[...])
    o_ref[...] = out


def _combine(x2, route, yb, dest, gf, final):
    n_tok, dm = x2.shape
    tm = _tile(n_tok, 256)
    return pl.pallas_call(
        functools.partial(_combine_kernel, tm=tm, final=final),
        grid_spec=pltpu.PrefetchScalarGridSpec(
            num_scalar_prefetch=1,
            grid=(n_tok // tm,),
            in_specs=[
                pl.BlockSpec((tm, dm), lambda i, d: (i, 0)),
                pl.BlockSpec((tm, LANES), lambda i, d: (i, 0)),
                pl.BlockSpec((1, dm), lambda i, d: (0, 0)),
                pl.BlockSpec(memory_space=pl.ANY),
            ],
            out_specs=pl.BlockSpec((tm, dm), lambda i, d: (i, 0)),
            scratch_shapes=[pltpu.VMEM((TOP_K, tm, dm), F32), pltpu.SemaphoreType.DMA((TOP_K,))],
        ),
        out_shape=jax.ShapeDtypeStruct((n_tok, dm), F32),
        compiler_params=pltpu.CompilerParams(
            dimension_semantics=("arbitrary",), vmem_limit_bytes=VMEM_LIMIT),
        name="combine",
    )(dest, x2, route, gf.reshape(1, dm), yb)


def _plan(route, tmb, n_blocks):
    n_tok = route.shape[0]
    n_assign = n_tok * TOP_K
    e_flat = route[:, :TOP_K].astype(jnp.int32).reshape(n_assign)
    onehot = (e_flat[:, None] == jnp.arange(N_EXPERTS, dtype=jnp.int32)[None, :]).astype(jnp.int32)
    csum = jnp.cumsum(onehot, axis=0)
    counts = csum[-1]
    rank = jnp.take_along_axis(csum, e_flat[:, None], axis=1)[:, 0] - 1
    pcounts = (counts + tmb - 1) // tmb * tmb
    pends = jnp.cumsum(pcounts)
    pstarts = pends - pcounts
    dest = (pstarts[e_flat] + rank).astype(jnp.int32)
    tok = jnp.arange(n_assign, dtype=jnp.int32) // TOP_K
    row_tok = jnp.zeros((n_blocks * tmb,), jnp.int32).at[dest].set(tok)
    blk_e = jnp.minimum(
        jnp.searchsorted(pends, jnp.arange(n_blocks, dtype=jnp.int32) * tmb, side='right'),
        N_EXPERTS - 1).astype(jnp.int32)
    n_used = (pends[-1:] // tmb).astype(jnp.int32)
    return dest, row_tok, blk_e, n_used


def kernel(x, norm_mix_g, w_in, conv_w, w_out, norm_ffn_g, w_group, b_group, w_router, b_router,
           w_expert_gate, w_expert_up, w_expert_down, final_norm_g):
    bsz, seq, dm = x.shape
    depth = w_in.shape[0]
    n_tok = bsz * seq
    tmb = _tile(n_tok * TOP_K, 256)
    n_blocks = n_tok * TOP_K // tmb + N_EXPERTS
    rb = _tile(n_blocks * tmb, 512)
    x2 = x.reshape(n_tok, dm)
    for l in range(depth):
        z = _in_proj(x2, norm_mix_g[l], w_in[l].astype(BF16))
        x2 = _mixer(z, x2, w_out[l].astype(BF16), conv_w[l], bsz, seq)
        route = _router(x2, norm_ffn_g[l], w_group[l], b_group[l], w_router[l], b_router[l])
        dest, row_tok, blk_e, n_used = _plan(route, tmb, n_blocks)
        xs = _dispatch(x2, row_tok, rb)
        yb = _experts(xs, norm_ffn_g[l], w_expert_gate[l].astype(BF16), w_expert_up[l].astype(BF16),
                      w_expert_down[l].astype(BF16), blk_e, n_used, tmb)
        x2 = _combine(x2, route, yb, dest, final_norm_g, final=(l == depth - 1))
    return x2.reshape(bsz, seq, dm)
```

```python
import functools

import jax
import jax.numpy as jnp
from jax import lax
from jax.experimental import pallas as pl
from jax.experimental.pallas import tpu as pltpu

CHUNK = 64
RET_HEAD_DIM = 128
CONV_WIDTH = 3
ROPE_BASE = 10000.0
N_GROUPS = 4
EXPERTS_PER_GROUP = 8
N_EXPERTS = N_GROUPS * EXPERTS_PER_GROUP
TOP_K = 2
EPS = 1e-6

LANES = 128
SUBLANES = 8
VMEM_LIMIT = 56 * 1024 * 1024

F32 = jnp.float32
BF16 = jnp.bfloat16


def _tile(n, want):
    t = min(n, want)
    while n % t:
        t //= 2
    return t


def _rms(x, g):
    return x * lax.rsqrt(jnp.mean(x * x, axis=-1, keepdims=True) + EPS) * g


def _in_proj_kernel(x_ref, g_ref, w_ref, o_ref, h_scr):
    @pl.when(pl.program_id(1) == 0)
    def _():
        h_scr[...] = _rms(x_ref[...], g_ref[...]).astype(BF16)

    o_ref[...] = jnp.dot(h_scr[...], w_ref[...], preferred_element_type=F32)


def _in_proj(x2, g, w_bf16, layer):
    n_tok, dm = x2.shape
    n_out = w_bf16.shape[2]
    tm = _tile(n_tok, 1024)
    tn = _tile(n_out, 1024)
    return pl.pallas_call(
        _in_proj_kernel,
        grid=(n_tok // tm, n_out // tn),
        in_specs=[
            pl.BlockSpec((tm, dm), lambda i, j: (i, 0)),
            pl.BlockSpec((1, dm), lambda i, j: (0, 0)),
            pl.BlockSpec((None, dm, tn), lambda i, j: (layer, 0, j)),
        ],
        out_specs=pl.BlockSpec((tm, tn), lambda i, j: (i, j)),
        out_shape=jax.ShapeDtypeStruct((n_tok, n_out), F32),
        scratch_shapes=[pltpu.VMEM((tm, dm), BF16)],
        compiler_params=pltpu.CompilerParams(
            dimension_semantics=("parallel", "arbitrary"), vmem_limit_bytes=VMEM_LIMIT),
        name="in_proj",
    )(x2, g.reshape(1, dm), w_bf16)


def _mixer_kernel(z_ref, x_ref, wout_ref, convw_ref, cos_ref, sin_ref, dmask_ref, qdec_ref,
                  kdec_ref, sdec_ref, o_ref, state_scr, u_scr, mixed_scr, *, ts, d_conv, n_heads):
    dh = RET_HEAD_DIM
    d_ret = n_heads * dh

    @pl.when(pl.program_id(1) == 0)
    def _():
        state_scr[...] = jnp.zeros_like(state_scr)
        u_scr[0:SUBLANES, :] = jnp.zeros((SUBLANES, d_conv), F32)

    u_scr[SUBLANES:SUBLANES + ts, :] = z_ref[:, d_conv:2 * d_conv] * z_ref[:, 2 * d_conv:3 * d_conv]
    y = (convw_ref[2:3, :] * u_scr[SUBLANES:SUBLANES + ts, :]
         + convw_ref[1:2, :] * u_scr[SUBLANES - 1:SUBLANES - 1 + ts, :]
         + convw_ref[0:1, :] * u_scr[SUBLANES - 2:SUBLANES - 2 + ts, :])
    mixed_scr[:, 0:d_conv] = (z_ref[:, 0:d_conv] * y).astype(BF16)
    u_scr[0:SUBLANES, :] = u_scr[ts:ts + SUBLANES, :]

    cos = cos_ref[...]
    sin = sin_ref[...]
    scale = RET_HEAD_DIM ** -0.5
    base = 3 * d_conv
    for h in range(n_heads):
        c0 = h * dh
        q = z_ref[:, base + c0:base + c0 + dh]
        k = z_ref[:, base + d_ret + c0:base + d_ret + c0 + dh]
        v = z_ref[:, base + 2 * d_ret + c0:base + 2 * d_ret + c0 + dh].astype(BF16)
        g = z_ref[:, base + 3 * d_ret + c0:base + 3 * d_ret + c0 + dh]
        qr = q * cos + pltpu.roll(q, dh // 2, 1) * sin
        kr = (k * cos + pltpu.roll(k, dh // 2, 1) * sin) * scale
        s = lax.dot_general(qr.astype(BF16), kr.astype(BF16), (((1,), (1,)), ((), ())),
                            preferred_element_type=F32) * dmask_ref[h]
        o = jnp.dot(s.astype(BF16), v, preferred_element_type=F32)
        st = state_scr[h]
        o = o + jnp.dot((qr * qdec_ref[:, c0:c0 + dh]).astype(BF16), st.astype(BF16),
                        preferred_element_type=F32)
        kv = lax.dot_general((kr * kdec_ref[:, c0:c0 + dh]).astype(BF16), v,
                             (((0,), (0,)), ((), ())), preferred_element_type=F32)
        state_scr[h] = st * sdec_ref[:, c0:c0 + dh] + kv
        on = o * lax.rsqrt(jnp.mean(o * o, axis=-1, keepdims=True) + EPS)
        mixed_scr[:, d_conv + c0:d_conv + c0 + dh] = (on * (g * jax.nn.sigmoid(g))).astype(BF16)

    o_ref[...] = x_ref[...] + jnp.dot(mixed_scr[...], wout_ref[...], preferred_element_type=F32)


def _retention_tables(seq, ts, n_heads):
    dh = RET_HEAD_DIM
    half = dh // 2
    pos = jnp.arange(seq, dtype=F32)
    inv = ROPE_BASE ** (-jnp.arange(half, dtype=F32) / half)
    ang = pos[:, None] * inv[None, :]
    cos = jnp.cos(ang)
    sin = jnp.sin(ang)
    cos_full = jnp.concatenate([cos, cos], axis=-1)
    sin_signed = jnp.concatenate([-sin, sin], axis=-1)
    log_g = jnp.log1p(-jnp.exp2(-5.0 - jnp.arange(n_heads, dtype=F32)))
    idx = jnp.arange(ts, dtype=F32)
    dist = jnp.abs(idx[:, None] - idx[None, :])
    chunk_id = jnp.arange(ts) // CHUNK
    visible = chunk_id[None, :] <= chunk_id[:, None]
    dmask = jnp.where(visible[None], jnp.exp(log_g[:, None, None] * dist[None]), 0.0)
    rep = lambda a: jnp.repeat(a, dh, axis=-1)
    qdec = rep(jnp.exp(log_g[None, :] * (idx[:, None] + 1.0)))
    kdec = rep(jnp.exp(log_g[None, :] * (ts - 1.0 - idx[:, None])))
    sdec = rep(jnp.exp(log_g * ts)[None, :])
    return cos_full, sin_signed, dmask.astype(F32), qdec, kdec, sdec


def _mixer(z, x2, wout_bf16, conv_w, bsz, seq, layer):
    n_tok, dm = x2.shape
    d_conv = conv_w.shape[1]
    d_ret = dm - d_conv
    n_heads = d_ret // RET_HEAD_DIM
    d_in = z.shape[1]
    ts = _tile(seq, 256)
    ns = seq // ts
    cos_full, sin_signed, dmask, qdec, kdec, sdec = _retention_tables(seq, ts, n_heads)
    kern = functools.partial(_mixer_kernel, ts=ts, d_conv=d_conv, n_heads=n_heads)
    const2 = lambda b, s: (0, 0)
    return pl.pallas_call(
        kern,
        grid=(bsz, ns),
        in_specs=[
            pl.BlockSpec((ts, d_in), lambda b, s: (b * ns + s, 0)),
            pl.BlockSpec((ts, dm), lambda b, s: (b * ns + s, 0)),
            pl.BlockSpec((None, dm, dm), lambda b, s: (layer, 0, 0)),
            pl.BlockSpec((CONV_WIDTH, d_conv), const2),
            pl.BlockSpec((ts, RET_HEAD_DIM), lambda b, s: (s, 0)),
            pl.BlockSpec((ts, RET_HEAD_DIM), lambda b, s: (s, 0)),
            pl.BlockSpec((n_heads, ts, ts), lambda b, s: (0, 0, 0)),
            pl.BlockSpec((ts, d_ret), const2),
            pl.BlockSpec((ts, d_ret), const2),
            pl.BlockSpec((1, d_ret), const2),
        ],
        out_specs=pl.BlockSpec((ts, dm), lambda b, s: (b * ns + s, 0)),
        out_shape=jax.ShapeDtypeStruct((n_tok, dm), F32),
        scratch_shapes=[
            pltpu.VMEM((n_heads, RET_HEAD_DIM, RET_HEAD_DIM), F32),
            pltpu.VMEM((ts + SUBLANES, d_conv), F32),
            pltpu.VMEM((ts, dm), BF16),
        ],
        compiler_params=pltpu.CompilerParams(
            dimension_semantics=("parallel", "arbitrary"), vmem_limit_bytes=VMEM_LIMIT),
        name="mixer",
    )(z, x2, wout_bf16, conv_w, cos_full, sin_signed, dmask, qdec, kdec, sdec)


def _router_kernel(x_ref, g_ref, wr_ref, br_ref, route_ref):
    h = _rms(x_ref[...], g_ref[...]).astype(BF16)
    logits = jnp.dot(h, wr_ref[...], preferred_element_type=F32) + br_ref[...]
    lane = lax.broadcasted_iota(jnp.int32, logits.shape, 1)
    neg = jnp.float32(-jnp.inf)
    big = jnp.int32(LANES)

    gl = jnp.where(lane < N_GROUPS, logits, neg)
    gmax = jnp.max(gl, axis=-1, keepdims=True)
    g_sel = jnp.min(jnp.where(gl == gmax, lane, big), axis=-1, keepdims=True)
    p_g = 1.0 / jnp.sum(jnp.exp(gl - gmax), axis=-1, keepdims=True)

    lo = N_GROUPS + g_sel * EXPERTS_PER_GROUP
    el = jnp.where(lane >= lo, jnp.where(lane < lo + EXPERTS_PER_GROUP, logits, neg), neg)
    v1 = jnp.max(el, axis=-1, keepdims=True)
    i1 = jnp.min(jnp.where(el == v1, lane, big), axis=-1, keepdims=True)
    el2 = jnp.where(lane == i1, neg, el)
    v2 = jnp.max(el2, axis=-1, keepdims=True)
    i2 = jnp.min(jnp.where(el2 == v2, lane, big), axis=-1, keepdims=True)
    t = jnp.exp(v2 - v1)
    gate1 = p_g / (1.0 + t)
    gate2 = p_g * t / (1.0 + t)

    e1 = (i1 - N_GROUPS).astype(F32)
    e2 = (i2 - N_GROUPS).astype(F32)
    route_ref[...] = jnp.where(lane == 0, e1, jnp.where(lane == 1, e2, jnp.where(
        lane == 2, gate1, jnp.where(lane == 3, gate2, 0.0))))


def _router(x2, g, w_group, b_group, w_router, b_router):
    n_tok, dm = x2.shape
    n_logit = N_GROUPS + N_EXPERTS
    wr = jnp.zeros((dm, LANES), F32).at[:, :N_GROUPS].set(w_group).at[:, N_GROUPS:n_logit].set(w_router)
    br = jnp.zeros((1, LANES), F32).at[0, :N_GROUPS].set(b_group).at[0, N_GROUPS:n_logit].set(b_router)
    tm = _tile(n_tok, 512)
    return pl.pallas_call(
        _router_kernel,
        grid=(n_tok // tm,),
        in_specs=[
            pl.BlockSpec((tm, dm), lambda i: (i, 0)),
            pl.BlockSpec((1, dm), lambda i: (0, 0)),
            pl.BlockSpec((dm, LANES), lambda i: (0, 0)),
            pl.BlockSpec((1, LANES), lambda i: (0, 0)),
        ],
        out_specs=pl.BlockSpec((tm, LANES), lambda i: (i, 0)),
        out_shape=jax.ShapeDtypeStruct((n_tok, LANES), F32),
        compiler_params=pltpu.CompilerParams(
            dimension_semantics=("parallel",), vmem_limit_bytes=VMEM_LIMIT),
        name="router",
    )(x2, g.reshape(1, dm), wr.astype(BF16), br)


ISSUE_UNROLL = 8


def _issue_rows(idx_ref, idx_base, src_hbm, dst, sem, n_rows):
    def body(j, carry):
        for u in range(ISSUE_UNROLL):
            r = j * ISSUE_UNROLL + u
            pltpu.make_async_copy(src_hbm.at[pl.ds(idx_ref[idx_base + r], 1)],
                                  dst.at[pl.ds(r, 1)], sem).start()
        return carry

    lax.fori_loop(0, n_rows // ISSUE_UNROLL, body, 0)


def _wait_rows(src_hbm, dst, sem):
    pltpu.make_async_copy(src_hbm.at[pl.ds(0, dst.shape[0])], dst, sem).wait()


def _experts_kernel(blk_e_ref, n_used_ref, row_tok_ref, x_hbm, g_ref, w1_ref, w3_ref, w2_ref, o_ref,
                    xbuf, sem, *, tmb):
    del blk_e_ref
    i = pl.program_id(0)
    n_used = n_used_ref[0]

    @pl.when(i == 0)
    def _():
        _issue_rows(row_tok_ref, 0, x_hbm, xbuf.at[0], sem.at[0], tmb)

    @pl.when(i + 1 < n_used)
    def _():
        nxt = (i + 1) % 2
        _issue_rows(row_tok_ref, (i + 1) * tmb, x_hbm, xbuf.at[nxt], sem.at[nxt], tmb)

    @pl.when(i < n_used)
    def _():
        slot = i % 2
        _wait_rows(x_hbm, xbuf.at[slot], sem.at[slot])
        h = _rms(xbuf[slot], g_ref[...]).astype(BF16)
        a = jnp.dot(h, w1_ref[...], preferred_element_type=F32)
        b = jnp.dot(h, w3_ref[...], preferred_element_type=F32)
        hid = (a * jax.nn.sigmoid(a) * b).astype(BF16)
        o_ref[...] = jnp.dot(hid, w2_ref[...], preferred_element_type=F32)

    @pl.when(i >= n_used)
    def _():
        o_ref[...] = jnp.zeros_like(o_ref)


def _experts(x2, g, w1, w3, w2, blk_e, n_used, row_tok, tmb, layer):
    n_tok, dm = x2.shape
    de = w1.shape[3]
    n_rows = row_tok.shape[0]
    nb = n_rows // tmb
    wmap = lambda i, be, nu, rt: (layer, be[i], 0, 0)
    return pl.pallas_call(
        functools.partial(_experts_kernel, tmb=tmb),
        grid_spec=pltpu.PrefetchScalarGridSpec(
            num_scalar_prefetch=3,
            grid=(nb,),
            in_specs=[
                pl.BlockSpec(memory_space=pl.ANY),
                pl.BlockSpec((1, dm), lambda i, be, nu, rt: (0, 0)),
                pl.BlockSpec((None, None, dm, de), wmap),
                pl.BlockSpec((None, None, dm, de), wmap),
                pl.BlockSpec((None, None, de, dm), wmap),
            ],
            out_specs=pl.BlockSpec((tmb, dm), lambda i, be, nu, rt: (i, 0)),
            scratch_shapes=[pltpu.VMEM((2, tmb, dm), F32), pltpu.SemaphoreType.DMA((2,))],
        ),
        out_shape=jax.ShapeDtypeStruct((n_rows, dm), F32),
        compiler_params=pltpu.CompilerParams(
            dimension_semantics=("arbitrary",), vmem_limit_bytes=VMEM_LIMIT),
        name="experts",
    )(blk_e, n_used, row_tok, x2, g.reshape(1, dm), w1, w3, w2)


def _combine_kernel(dest_ref, x_ref, route_ref, gf_ref, yb_hbm, o_ref, buf, sem, *, tm, n_tok, final):
    i = pl.program_id(0)

    def issue(step, slot):
        for k in range(TOP_K):
            _issue_rows(dest_ref, k * n_tok + step * tm, yb_hbm, buf.at[slot, k], sem.at[slot, k], tm)

    @pl.when(i == 0)
    def _():
        issue(0, 0)

    @pl.when(i + 1 < pl.num_programs(0))
    def _():
        issue(i + 1, (i + 1) % 2)

    slot = i % 2
    for k in range(TOP_K):
        _wait_rows(yb_hbm, buf.at[slot, k], sem.at[slot, k])
    out = x_ref[...] + route_ref[:, 2:3] * buf[slot, 0] + route_ref[:, 3:4] * buf[slot, 1]
    if final:
        out = _rms(out, gf_ref[...])
    o_ref[...] = out


def _combine(x2, route, yb, dest, gf, final):
    n_tok, dm = x2.shape
    tm = _tile(n_tok, 256)
    return pl.pallas_call(
        functools.partial(_combine_kernel, tm=tm, n_tok=n_tok, final=final),
        grid_spec=pltpu.PrefetchScalarGridSpec(
            num_scalar_prefetch=1,
            grid=(n_tok // tm,),
            in_specs=[
                pl.BlockSpec((tm, dm), lambda i, d: (i, 0)),
                pl.BlockSpec((tm, LANES), lambda i, d: (i, 0)),
                pl.BlockSpec((1, dm), lambda i, d: (0, 0)),
                pl.BlockSpec(memory_space=pl.ANY),
            ],
            out_specs=pl.BlockSpec((tm, dm), lambda i, d: (i, 0)),
            scratch_shapes=[pltpu.VMEM((2, TOP_K, tm, dm), F32), pltpu.SemaphoreType.DMA((2, TOP_K))],
        ),
        out_shape=jax.ShapeDtypeStruct((n_tok, dm), F32),
        compiler_params=pltpu.CompilerParams(
            dimension_semantics=("arbitrary",), vmem_limit_bytes=VMEM_LIMIT),
        name="combine",
    )(dest, x2, route, gf.reshape(1, dm), yb)


def _plan(route, tmb, n_blocks):
    n_tok = route.shape[0]
    n_assign = n_tok * TOP_K
    e_flat = route[:, :TOP_K].astype(jnp.int32).reshape(n_assign)
    onehot = (e_flat[:, None] == jnp.arange(N_EXPERTS, dtype=jnp.int32)[None, :]).astype(jnp.int32)
    csum = jnp.cumsum(onehot, axis=0)
    counts = csum[-1]
    rank = jnp.take_along_axis(csum, e_flat[:, None], axis=1)[:, 0] - 1
    pcounts = (counts + tmb - 1) // tmb * tmb
    pends = jnp.cumsum(pcounts)
    pstarts = pends - pcounts
    dest = (pstarts[e_flat] + rank).astype(jnp.int32)
    tok = jnp.arange(n_assign, dtype=jnp.int32) // TOP_K
    row_tok = jnp.zeros((n_blocks * tmb,), jnp.int32).at[dest].set(tok)
    blk_start = jnp.arange(n_blocks, dtype=jnp.int32) * tmb
    blk_e = jnp.minimum(jnp.sum(pends[None, :] <= blk_start[:, None], axis=1),
                        N_EXPERTS - 1).astype(jnp.int32)
    n_used = (pends[-1:] // tmb).astype(jnp.int32)
    dest_kmajor = dest.reshape(n_tok, TOP_K).T.reshape(n_assign)
    return dest_kmajor, row_tok, blk_e, n_used


def kernel(x, norm_mix_g, w_in, conv_w, w_out, norm_ffn_g, w_group, b_group, w_router, b_router,
           w_expert_gate, w_expert_up, w_expert_down, final_norm_g):
    bsz, seq, dm = x.shape
    depth = w_in.shape[0]
    n_tok = bsz * seq
    tmb = _tile(n_tok * TOP_K, 256)
    n_blocks = n_tok * TOP_K // tmb + N_EXPERTS
    x2 = x.reshape(n_tok, dm)
    w_in_b, w_out_b = w_in.astype(BF16), w_out.astype(BF16)
    w1_b, w3_b, w2_b = w_expert_gate.astype(BF16), w_expert_up.astype(BF16), w_expert_down.astype(BF16)
    for l in range(depth):
        z = _in_proj(x2, norm_mix_g[l], w_in_b, l)
        x2 = _mixer(z, x2, w_out_b, conv_w[l], bsz, seq, l)
        route = _router(x2, norm_ffn_g[l], w_group[l], b_group[l], w_router[l], b_router[l])
        dest, row_tok, blk_e, n_used = _plan(route, tmb, n_blocks)
        yb = _experts(x2, norm_ffn_g[l], w1_b, w3_b, w2_b, blk_e, n_used, row_tok, tmb, l)
        x2 = _combine(x2, route, yb, dest, final_norm_g, final=(l == depth - 1))
    return x2.reshape(bsz, seq, dm)
```

```python
import functools

import jax
import jax.numpy as jnp
from jax import lax
from jax.experimental import pallas as pl
from jax.experimental.pallas import tpu as pltpu

CHUNK = 64
RET_HEAD_DIM = 128
CONV_WIDTH = 3
ROPE_BASE = 10000.0
N_GROUPS = 4
EXPERTS_PER_GROUP = 8
N_EXPERTS = N_GROUPS * EXPERTS_PER_GROUP
TOP_K = 2
EPS = 1e-6

LANES = 128
SUBLANES = 8
VMEM_LIMIT = 56 * 1024 * 1024

F32 = jnp.float32
BF16 = jnp.bfloat16


def _tile(n, want):
    t = min(n, want)
    while n % t:
        t //= 2
    return t


def _rms(x, g):
    return x * lax.rsqrt(jnp.mean(x * x, axis=-1, keepdims=True) + EPS) * g


def _tm_write_copies(src_dense, dst_tm_hbm, row0, sem):
    rows = src_dense.shape[0]
    return [pltpu.make_async_copy(src_dense.at[:, pl.ds(c * LANES, LANES)],
                                  dst_tm_hbm.at[pl.ds(row0, rows), c, :], sem)
            for c in range(dst_tm_hbm.shape[1])]


def _dense_chunk(buf, lead, c):
    v = buf[(*lead, slice(None), c)]
    return v.reshape(v.shape[0] * SUBLANES, LANES)


def _in_proj_kernel(x_ref, g_ref, w_ref, o_ref, h_scr):
    @pl.when(pl.program_id(1) == 0)
    def _():
        h_scr[...] = _rms(x_ref[...], g_ref[...]).astype(BF16)

    o_ref[...] = jnp.dot(h_scr[...], w_ref[...], preferred_element_type=F32)


def _in_proj(x2, g, w_bf16, layer):
    n_tok, dm = x2.shape
    n_out = w_bf16.shape[2]
    tm = _tile(n_tok, 1024)
    tn = _tile(n_out, 1024)
    return pl.pallas_call(
        _in_proj_kernel,
        grid=(n_tok // tm, n_out // tn),
        in_specs=[
            pl.BlockSpec((tm, dm), lambda i, j: (i, 0)),
            pl.BlockSpec((1, dm), lambda i, j: (0, 0)),
            pl.BlockSpec((None, dm, tn), lambda i, j: (layer, 0, j)),
        ],
        out_specs=pl.BlockSpec((tm, tn), lambda i, j: (i, j)),
        out_shape=jax.ShapeDtypeStruct((n_tok, n_out), F32),
        scratch_shapes=[pltpu.VMEM((tm, dm), BF16)],
        compiler_params=pltpu.CompilerParams(
            dimension_semantics=("parallel", "arbitrary"), vmem_limit_bytes=VMEM_LIMIT),
        name="in_proj",
    )(x2, g.reshape(1, dm), w_bf16)


def _mixer_kernel(z_ref, x_ref, wout_ref, convw_ref, cos_ref, sin_ref, dmask_ref, qdec_ref,
                  kdec_ref, sdec_ref, o_ref, ot_hbm, state_scr, u_scr, mixed_scr, tbuf, tsem,
                  *, ts, d_conv, n_heads):
    dh = RET_HEAD_DIM
    d_ret = n_heads * dh
    step = pl.program_id(0) * pl.num_programs(1) + pl.program_id(1)
    last = pl.num_programs(0) * pl.num_programs(1) - 1
    slot = step % 2

    @pl.when(step >= 2)
    def _():
        for cp in _tm_write_copies(tbuf.at[slot], ot_hbm, 0, tsem.at[slot]):
            cp.wait()

    @pl.when(pl.program_id(1) == 0)
    def _():
        state_scr[...] = jnp.zeros_like(state_scr)
        u_scr[0:SUBLANES, :] = jnp.zeros((SUBLANES, d_conv), F32)

    u_scr[SUBLANES:SUBLANES + ts, :] = z_ref[:, d_conv:2 * d_conv] * z_ref[:, 2 * d_conv:3 * d_conv]
    y = (convw_ref[2:3, :] * u_scr[SUBLANES:SUBLANES + ts, :]
         + convw_ref[1:2, :] * u_scr[SUBLANES - 1:SUBLANES - 1 + ts, :]
         + convw_ref[0:1, :] * u_scr[SUBLANES - 2:SUBLANES - 2 + ts, :])
    mixed_scr[:, 0:d_conv] = (z_ref[:, 0:d_conv] * y).astype(BF16)
    u_scr[0:SUBLANES, :] = u_scr[ts:ts + SUBLANES, :]

    cos = cos_ref[...]
    sin = sin_ref[...]
    scale = RET_HEAD_DIM ** -0.5
    base = 3 * d_conv
    for h in range(n_heads):
        c0 = h * dh
        q = z_ref[:, base + c0:base + c0 + dh]
        k = z_ref[:, base + d_ret + c0:base + d_ret + c0 + dh]
        v = z_ref[:, base + 2 * d_ret + c0:base + 2 * d_ret + c0 + dh].astype(BF16)
        g = z_ref[:, base + 3 * d_ret + c0:base + 3 * d_ret + c0 + dh]
        qr = q * cos + pltpu.roll(q, dh // 2, 1) * sin
        kr = (k * cos + pltpu.roll(k, dh // 2, 1) * sin) * scale
        s = lax.dot_general(qr.astype(BF16), kr.astype(BF16), (((1,), (1,)), ((), ())),
                            preferred_element_type=F32) * dmask_ref[h]
        o = jnp.dot(s.astype(BF16), v, preferred_element_type=F32)
        st = state_scr[h]
        o = o + jnp.dot((qr * qdec_ref[:, c0:c0 + dh]).astype(BF16), st.astype(BF16),
                        preferred_element_type=F32)
        kv = lax.dot_general((kr * kdec_ref[:, c0:c0 + dh]).astype(BF16), v,
                             (((0,), (0,)), ((), ())), preferred_element_type=F32)
        state_scr[h] = st * sdec_ref[:, c0:c0 + dh] + kv
        on = o * lax.rsqrt(jnp.mean(o * o, axis=-1, keepdims=True) + EPS)
        mixed_scr[:, d_conv + c0:d_conv + c0 + dh] = (on * (g * jax.nn.sigmoid(g))).astype(BF16)

    o_ref[...] = x_ref[...] + jnp.dot(mixed_scr[...], wout_ref[...], preferred_element_type=F32)
    tbuf[slot] = o_ref[...]
    for cp in _tm_write_copies(tbuf.at[slot], ot_hbm, step * ts, tsem.at[slot]):
        cp.start()

    @pl.when(step == last)
    def _():
        for s in range(2):
            for cp in _tm_write_copies(tbuf.at[s], ot_hbm, 0, tsem.at[s]):
                cp.wait()


def _retention_tables(seq, ts, n_heads):
    dh = RET_HEAD_DIM
    half = dh // 2
    pos = jnp.arange(seq, dtype=F32)
    inv = ROPE_BASE ** (-jnp.arange(half, dtype=F32) / half)
    ang = pos[:, None] * inv[None, :]
    cos = jnp.cos(ang)
    sin = jnp.sin(ang)
    cos_full = jnp.concatenate([cos, cos], axis=-1)
    sin_signed = jnp.concatenate([-sin, sin], axis=-1)
    log_g = jnp.log1p(-jnp.exp2(-5.0 - jnp.arange(n_heads, dtype=F32)))
    idx = jnp.arange(ts, dtype=F32)
    dist = jnp.abs(idx[:, None] - idx[None, :])
    chunk_id = jnp.arange(ts) // CHUNK
    visible = chunk_id[None, :] <= chunk_id[:, None]
    dmask = jnp.where(visible[None], jnp.exp(log_g[:, None, None] * dist[None]), 0.0)
    rep = lambda a: jnp.repeat(a, dh, axis=-1)
    qdec = rep(jnp.exp(log_g[None, :] * (idx[:, None] + 1.0)))
    kdec = rep(jnp.exp(log_g[None, :] * (ts - 1.0 - idx[:, None])))
    sdec = rep(jnp.exp(log_g * ts)[None, :])
    return cos_full, sin_signed, dmask.astype(F32), qdec, kdec, sdec


def _mixer(z, x2, wout_bf16, conv_w, bsz, seq, layer):
    n_tok, dm = x2.shape
    d_conv = conv_w.shape[1]
    d_ret = dm - d_conv
    n_heads = d_ret // RET_HEAD_DIM
    d_in = z.shape[1]
    ts = _tile(seq, 256)
    ns = seq // ts
    assert bsz * ns >= 2, "the token-major write pipeline needs at least two grid steps"
    cos_full, sin_signed, dmask, qdec, kdec, sdec = _retention_tables(seq, ts, n_heads)
    kern = functools.partial(_mixer_kernel, ts=ts, d_conv=d_conv, n_heads=n_heads)
    const2 = lambda b, s: (0, 0)
    return pl.pallas_call(
        kern,
        grid=(bsz, ns),
        in_specs=[
            pl.BlockSpec((ts, d_in), lambda b, s: (b * ns + s, 0)),
            pl.BlockSpec((ts, dm), lambda b, s: (b * ns + s, 0)),
            pl.BlockSpec((None, dm, dm), lambda b, s: (layer, 0, 0), pipeline_mode=pl.Buffered(1)),
            pl.BlockSpec((CONV_WIDTH, d_conv), const2),
            pl.BlockSpec((ts, RET_HEAD_DIM), lambda b, s: (s, 0)),
            pl.BlockSpec((ts, RET_HEAD_DIM), lambda b, s: (s, 0)),
            pl.BlockSpec((n_heads, ts, ts), lambda b, s: (0, 0, 0)),
            pl.BlockSpec((ts, d_ret), const2),
            pl.BlockSpec((ts, d_ret), const2),
            pl.BlockSpec((1, d_ret), const2),
        ],
        out_specs=[pl.BlockSpec((ts, dm), lambda b, s: (b * ns + s, 0)),
                   pl.BlockSpec(memory_space=pl.ANY)],
        out_shape=[jax.ShapeDtypeStruct((n_tok, dm), F32),
                   jax.ShapeDtypeStruct((n_tok, dm // LANES, LANES), F32)],
        scratch_shapes=[
            pltpu.VMEM((n_heads, RET_HEAD_DIM, RET_HEAD_DIM), F32),
            pltpu.VMEM((ts + SUBLANES, d_conv), F32),
            pltpu.VMEM((ts, dm), BF16),
            pltpu.VMEM((2, ts, dm), F32),
            pltpu.SemaphoreType.DMA((2,)),
        ],
        compiler_params=pltpu.CompilerParams(
            dimension_semantics=("arbitrary", "arbitrary"), vmem_limit_bytes=VMEM_LIMIT),
        name="mixer",
    )(z, x2, wout_bf16, conv_w, cos_full, sin_signed, dmask, qdec, kdec, sdec)


def _router_kernel(x_ref, g_ref, wr_ref, br_ref, route_ref):
    h = _rms(x_ref[...], g_ref[...]).astype(BF16)
    logits = jnp.dot(h, wr_ref[...], preferred_element_type=F32) + br_ref[...]
    lane = lax.broadcasted_iota(jnp.int32, logits.shape, 1)
    neg = jnp.float32(-jnp.inf)
    big = jnp.int32(LANES)

    gl = jnp.where(lane < N_GROUPS, logits, neg)
    gmax = jnp.max(gl, axis=-1, keepdims=True)
    g_sel = jnp.min(jnp.where(gl == gmax, lane, big), axis=-1, keepdims=True)
    p_g = 1.0 / jnp.sum(jnp.exp(gl - gmax), axis=-1, keepdims=True)

    lo = N_GROUPS + g_sel * EXPERTS_PER_GROUP
    el = jnp.where(lane >= lo, jnp.where(lane < lo + EXPERTS_PER_GROUP, logits, neg), neg)
    v1 = jnp.max(el, axis=-1, keepdims=True)
    i1 = jnp.min(jnp.where(el == v1, lane, big), axis=-1, keepdims=True)
    el2 = jnp.where(lane == i1, neg, el)
    v2 = jnp.max(el2, axis=-1, keepdims=True)
    i2 = jnp.min(jnp.where(el2 == v2, lane, big), axis=-1, keepdims=True)
    t = jnp.exp(v2 - v1)
    gate1 = p_g / (1.0 + t)
    gate2 = p_g * t / (1.0 + t)

    e1 = (i1 - N_GROUPS).astype(F32)
    e2 = (i2 - N_GROUPS).astype(F32)
    route_ref[...] = jnp.where(lane == 0, e1, jnp.where(lane == 1, e2, jnp.where(
        lane == 2, gate1, jnp.where(lane == 3, gate2, 0.0))))


def _router(x2, g, w_group, b_group, w_router, b_router):
    n_tok, dm = x2.shape
    n_logit = N_GROUPS + N_EXPERTS
    wr = jnp.zeros((dm, LANES), F32).at[:, :N_GROUPS].set(w_group).at[:, N_GROUPS:n_logit].set(w_router)
    br = jnp.zeros((1, LANES), F32).at[0, :N_GROUPS].set(b_group).at[0, N_GROUPS:n_logit].set(b_router)
    tm = _tile(n_tok, 512)
    return pl.pallas_call(
        _router_kernel,
        grid=(n_tok // tm,),
        in_specs=[
            pl.BlockSpec((tm, dm), lambda i: (i, 0)),
            pl.BlockSpec((1, dm), lambda i: (0, 0)),
            pl.BlockSpec((dm, LANES), lambda i: (0, 0)),
            pl.BlockSpec((1, LANES), lambda i: (0, 0)),
        ],
        out_specs=pl.BlockSpec((tm, LANES), lambda i: (i, 0)),
        out_shape=jax.ShapeDtypeStruct((n_tok, LANES), F32),
        compiler_params=pltpu.CompilerParams(
            dimension_semantics=("parallel",), vmem_limit_bytes=VMEM_LIMIT),
        name="router",
    )(x2, g.reshape(1, dm), wr.astype(BF16), br)


ISSUE_UNROLL = 16


def _row_copy(src_tm, idx, dst, r8, s, sem):
    return pltpu.make_async_copy(src_tm.at[idx], dst.at[r8, :, s, :], sem)


def _issue_rows(idx_ref, idx_base, src_tm, dst, sem, n_rows):
    def body(j, carry):
        for u in range(ISSUE_UNROLL):
            r8 = j * (ISSUE_UNROLL // SUBLANES) + u // SUBLANES
            _row_copy(src_tm, idx_ref[idx_base + j * ISSUE_UNROLL + u], dst, r8, u % SUBLANES, sem).start()
        return carry

    lax.fori_loop(0, n_rows // ISSUE_UNROLL, body, 0)


def _wait_rows(dst, sem):
    pltpu.make_async_copy(dst, dst, sem).wait()


def _experts_kernel(blk_e_ref, n_used_ref, row_tok_ref, xt_hbm, g_ref, w1_ref, w3_ref, w2_ref, yb_hbm,
                    xbuf, h_scr, obuf, sem, osem, *, tmb):
    del blk_e_ref
    i = pl.program_id(0)
    last = pl.num_programs(0) - 1
    n_used = n_used_ref[0]
    slot = i % 2
    nxt = (i + 1) % 2
    n_chunks = xt_hbm.shape[1]
    dm = n_chunks * LANES

    def wait_out(s):
        for cp in _tm_write_copies(obuf.at[s], yb_hbm, 0, osem.at[s]):
            cp.wait()

    @pl.when(i == 0)
    def _():
        _issue_rows(row_tok_ref, 0, xt_hbm, xbuf.at[0], sem.at[0], tmb)

    @pl.when(i <= n_used)
    def _():
        _wait_rows(xbuf.at[slot], sem.at[slot])

    @pl.when(i >= 2)
    def _():
        wait_out(slot)

    @pl.when(i < n_used)
    def _():
        ss = jnp.zeros((tmb, 1), F32)
        for c in range(n_chunks):
            xc = _dense_chunk(xbuf, (slot,), c)
            ss = ss + jnp.sum(xc * xc, axis=-1, keepdims=True)
        rs = lax.rsqrt(ss * (1.0 / dm) + EPS)
        for c in range(n_chunks):
            cs = slice(c * LANES, (c + 1) * LANES)
            h_scr[:, cs] = (_dense_chunk(xbuf, (slot,), c) * rs * g_ref[:, cs]).astype(BF16)
        for r in range(tmb):
            _row_copy(xt_hbm, row_tok_ref[(i + 1) * tmb + r], xbuf.at[nxt], r // SUBLANES, r % SUBLANES,
                      sem.at[nxt]).start()
        h = h_scr[...]
        a = jnp.dot(h, w1_ref[...], preferred_element_type=F32)
        b = jnp.dot(h, w3_ref[...], preferred_element_type=F32)
        hid = (a * jax.nn.sigmoid(a) * b).astype(BF16)
        obuf[slot] = jnp.dot(hid, w2_ref[...], preferred_element_type=F32)

    @pl.when(i >= n_used)
    def _():
        obuf[slot] = jnp.zeros(obuf.shape[1:], F32)

    for cp in _tm_write_copies(obuf.at[slot], yb_hbm, i * tmb, osem.at[slot]):
        cp.start()

    @pl.when(i == last)
    def _():
        @pl.when(i < n_used)
        def _():
            _wait_rows(xbuf.at[nxt], sem.at[nxt])

        wait_out(slot)

        @pl.when(i >= 1)
        def _():
            wait_out(nxt)


def _experts(xt, g, w1, w3, w2, blk_e, n_used, row_tok, tmb, layer):
    n_tok, n_chunks, _ = xt.shape
    dm = n_chunks * LANES
    de = w1.shape[3]
    n_rows = row_tok.shape[0] - tmb
    nb = n_rows // tmb
    wmap = lambda i, be, nu, rt: (layer, be[i], 0, 0)
    return pl.pallas_call(
        functools.partial(_experts_kernel, tmb=tmb),
        grid_spec=pltpu.PrefetchScalarGridSpec(
            num_scalar_prefetch=3,
            grid=(nb,),
            in_specs=[
                pl.BlockSpec(memory_space=pl.ANY),
                pl.BlockSpec((1, dm), lambda i, be, nu, rt: (0, 0)),
                pl.BlockSpec((None, None, dm, de), wmap),
                pl.BlockSpec((None, None, dm, de), wmap),
                pl.BlockSpec((None, None, de, dm), wmap),
            ],
            out_specs=pl.BlockSpec(memory_space=pl.ANY),
            scratch_shapes=[
                pltpu.VMEM((2, tmb // SUBLANES, n_chunks, SUBLANES, LANES), F32),
                pltpu.VMEM((tmb, dm), BF16),
                pltpu.VMEM((2, tmb, dm), F32),
                pltpu.SemaphoreType.DMA((2,)), pltpu.SemaphoreType.DMA((2,))],
        ),
        out_shape=jax.ShapeDtypeStruct((n_rows, n_chunks, LANES), F32),
        compiler_params=pltpu.CompilerParams(
            dimension_semantics=("arbitrary",), vmem_limit_bytes=VMEM_LIMIT),
        name="experts",
    )(blk_e, n_used, row_tok, xt, g.reshape(1, dm), w1, w3, w2)


def _combine_kernel(dest_ref, x_ref, route_ref, gf_ref, yb_hbm, o_ref, buf, sem, *, tm, n_tok, final):
    i = pl.program_id(0)
    n_chunks = yb_hbm.shape[1]
    dm = n_chunks * LANES

    def issue(step, slot):
        for k in range(TOP_K):
            _issue_rows(dest_ref, k * n_tok + step * tm, yb_hbm, buf.at[slot, k], sem.at[slot, k], tm)

    @pl.when(i == 0)
    def _():
        issue(0, 0)

    @pl.when(i + 1 < pl.num_programs(0))
    def _():
        issue(i + 1, (i + 1) % 2)

    slot = i % 2
    for k in range(TOP_K):
        _wait_rows(buf.at[slot, k], sem.at[slot, k])
    gate0 = route_ref[:, 2:3]
    gate1 = route_ref[:, 3:4]
    ss = jnp.zeros((tm, 1), F32)
    for c in range(n_chunks):
        cs = slice(c * LANES, (c + 1) * LANES)
        oc = (x_ref[:, cs] + gate0 * _dense_chunk(buf, (slot, 0), c)
              + gate1 * _dense_chunk(buf, (slot, 1), c))
        o_ref[:, cs] = oc
        if final:
            ss = ss + jnp.sum(oc * oc, axis=-1, keepdims=True)
    if final:
        rs = lax.rsqrt(ss * (1.0 / dm) + EPS)
        for c in range(n_chunks):
            cs = slice(c * LANES, (c + 1) * LANES)
            o_ref[:, cs] = o_ref[:, cs] * rs * gf_ref[:, cs]


def _combine(x2, route, yb, dest, gf, final):
    n_tok, dm = x2.shape
    n_chunks = dm // LANES
    tm = _tile(n_tok, 256)
    return pl.pallas_call(
        functools.partial(_combine_kernel, tm=tm, n_tok=n_tok, final=final),
        grid_spec=pltpu.PrefetchScalarGridSpec(
            num_scalar_prefetch=1,
            grid=(n_tok // tm,),
            in_specs=[
                pl.BlockSpec((tm, dm), lambda i, d: (i, 0)),
                pl.BlockSpec((tm, LANES), lambda i, d: (i, 0)),
                pl.BlockSpec((1, dm), lambda i, d: (0, 0)),
                pl.BlockSpec(memory_space=pl.ANY),
            ],
            out_specs=pl.BlockSpec((tm, dm), lambda i, d: (i, 0)),
            scratch_shapes=[pltpu.VMEM((2, TOP_K, tm // SUBLANES, n_chunks, SUBLANES, LANES), F32),
                            pltpu.SemaphoreType.DMA((2, TOP_K))],
        ),
        out_shape=jax.ShapeDtypeStruct((n_tok, dm), F32),
        compiler_params=pltpu.CompilerParams(
            dimension_semantics=("arbitrary",), vmem_limit_bytes=VMEM_LIMIT),
        name="combine",
    )(dest, x2, route, gf.reshape(1, dm), yb)


def _plan(route, tmb, n_blocks):
    n_tok = route.shape[0]
    n_assign = n_tok * TOP_K
    e_flat = route[:, :TOP_K].astype(jnp.int32).reshape(n_assign)
    onehot = (e_flat[:, None] == jnp.arange(N_EXPERTS, dtype=jnp.int32)[None, :]).astype(jnp.int32)
    csum = jnp.cumsum(onehot, axis=0)
    counts = csum[-1]
    rank = jnp.take_along_axis(csum, e_flat[:, None], axis=1)[:, 0] - 1
    pcounts = (counts + tmb - 1) // tmb * tmb
    pends = jnp.cumsum(pcounts)
    pstarts = pends - pcounts
    dest = (pstarts[e_flat] + rank).astype(jnp.int32)
    tok = jnp.arange(n_assign, dtype=jnp.int32) // TOP_K
    n_rows = (n_blocks + 1) * tmb
    pad_tok = jnp.arange(n_rows, dtype=jnp.int32) % n_tok
    row_tok = pad_tok.at[dest].set(tok)
    blk_start = jnp.arange(n_blocks, dtype=jnp.int32) * tmb
    blk_e = jnp.minimum(jnp.sum(pends[None, :] <= blk_start[:, None], axis=1),
                        N_EXPERTS - 1).astype(jnp.int32)
    n_used = (pends[-1:] // tmb).astype(jnp.int32)
    dest_kmajor = dest.reshape(n_tok, TOP_K).T.reshape(n_assign)
    return dest_kmajor, row_tok, blk_e, n_used


def kernel(x, norm_mix_g, w_in, conv_w, w_out, norm_ffn_g, w_group, b_group, w_router, b_router,
           w_expert_gate, w_expert_up, w_expert_down, final_norm_g):
    bsz, seq, dm = x.shape
    depth = w_in.shape[0]
    n_tok = bsz * seq
    tmb = _tile(n_tok * TOP_K, 256)
    n_blocks = n_tok * TOP_K // tmb + N_EXPERTS
    x2 = x.reshape(n_tok, dm)
    w_in_b, w_out_b = w_in.astype(BF16), w_out.astype(BF16)
    w1_b, w3_b, w2_b = w_expert_gate.astype(BF16), w_expert_up.astype(BF16), w_expert_down.astype(BF16)
    for l in range(depth):
        z = _in_proj(x2, norm_mix_g[l], w_in_b, l)
        x2, xt = _mixer(z, x2, w_out_b, conv_w[l], bsz, seq, l)
        route = _router(x2, norm_ffn_g[l], w_group[l], b_group[l], w_router[l], b_router[l])
        dest, row_tok, blk_e, n_used = _plan(route, tmb, n_blocks)
        yb = _experts(xt, norm_ffn_g[l], w1_b, w3_b, w2_b, blk_e, n_used, row_tok, tmb, l)
        x2 = _combine(x2, route, yb, dest, final_norm_g, final=(l == depth - 1))
    return x2.reshape(bsz, seq, dm)
```

```python
import functools

import jax
import jax.numpy as jnp
from jax import lax
from jax.experimental import pallas as pl
from jax.experimental.pallas import tpu as pltpu

CHUNK = 64
RET_HEAD_DIM = 128
CONV_WIDTH = 3
ROPE_BASE = 10000.0
N_GROUPS = 4
EXPERTS_PER_GROUP = 8
N_EXPERTS = N_GROUPS * EXPERTS_PER_GROUP
TOP_K = 2
EPS = 1e-6

LANES = 128
SUBLANES = 8
VMEM_LIMIT = 56 * 1024 * 1024

F32 = jnp.float32
BF16 = jnp.bfloat16


def _tile(n, want):
    t = min(n, want)
    while n % t:
        t //= 2
    return t


def _rms(x, g):
    return x * lax.rsqrt(jnp.mean(x * x, axis=-1, keepdims=True) + EPS) * g


def _tm_write_copies(src_dense, dst_tm_hbm, row0, sem):
    rows = src_dense.shape[0]
    return [pltpu.make_async_copy(src_dense.at[:, pl.ds(c * LANES, LANES)],
                                  dst_tm_hbm.at[pl.ds(row0, rows), c, :], sem)
            for c in range(dst_tm_hbm.shape[1])]


def _dense_chunk(buf, lead, c):
    v = buf[(*lead, slice(None), c)]
    return v.reshape(v.shape[0] * SUBLANES, LANES)


def _in_proj_kernel(x_ref, g_ref, w_ref, o_ref, h_scr):
    @pl.when(pl.program_id(1) == 0)
    def _():
        h_scr[...] = _rms(x_ref[...], g_ref[...]).astype(BF16)

    o_ref[...] = jnp.dot(h_scr[...], w_ref[...], preferred_element_type=F32).astype(o_ref.dtype)


def _in_proj(x2, g, w_bf16, layer):
    n_tok, dm = x2.shape
    n_out = w_bf16.shape[2]
    tm = _tile(n_tok, 1024)
    tn = _tile(n_out, 1024)
    return pl.pallas_call(
        _in_proj_kernel,
        grid=(n_tok // tm, n_out // tn),
        in_specs=[
            pl.BlockSpec((tm, dm), lambda i, j: (i, 0)),
            pl.BlockSpec((1, dm), lambda i, j: (0, 0)),
            pl.BlockSpec((None, dm, tn), lambda i, j: (layer, 0, j)),
        ],
        out_specs=pl.BlockSpec((tm, tn), lambda i, j: (i, j)),
        out_shape=jax.ShapeDtypeStruct((n_tok, n_out), BF16),
        scratch_shapes=[pltpu.VMEM((tm, dm), BF16)],
        compiler_params=pltpu.CompilerParams(
            dimension_semantics=("parallel", "arbitrary"), vmem_limit_bytes=VMEM_LIMIT),
        name="in_proj",
    )(x2, g.reshape(1, dm), w_bf16)


def _mixer_kernel(z_ref, x_ref, wout_ref, convw_ref, cos_ref, sin_ref, dmask_ref, qdec_ref,
                  kdec_ref, sdec_ref, o_ref, ot_hbm, state_scr, u_scr, mixed_scr, tbuf, tsem,
                  *, ts, d_conv, n_heads):
    dh = RET_HEAD_DIM
    d_ret = n_heads * dh
    step = pl.program_id(0) * pl.num_programs(1) + pl.program_id(1)
    last = pl.num_programs(0) * pl.num_programs(1) - 1
    slot = step % 2

    @pl.when(step >= 2)
    def _():
        for cp in _tm_write_copies(tbuf.at[slot], ot_hbm, 0, tsem.at[slot]):
            cp.wait()

    @pl.when(pl.program_id(1) == 0)
    def _():
        state_scr[...] = jnp.zeros_like(state_scr)
        u_scr[0:SUBLANES, :] = jnp.zeros((SUBLANES, d_conv), F32)

    zf = lambda lo, hi: z_ref[:, lo:hi].astype(F32)
    u_scr[SUBLANES:SUBLANES + ts, :] = zf(d_conv, 2 * d_conv) * zf(2 * d_conv, 3 * d_conv)
    y = (convw_ref[2:3, :] * u_scr[SUBLANES:SUBLANES + ts, :]
         + convw_ref[1:2, :] * u_scr[SUBLANES - 1:SUBLANES - 1 + ts, :]
         + convw_ref[0:1, :] * u_scr[SUBLANES - 2:SUBLANES - 2 + ts, :])
    mixed_scr[:, 0:d_conv] = (zf(0, d_conv) * y).astype(BF16)
    u_scr[0:SUBLANES, :] = u_scr[ts:ts + SUBLANES, :]

    cos = cos_ref[...]
    sin = sin_ref[...]
    scale = RET_HEAD_DIM ** -0.5
    base = 3 * d_conv
    for h in range(n_heads):
        c0 = h * dh
        q = zf(base + c0, base + c0 + dh)
        k = zf(base + d_ret + c0, base + d_ret + c0 + dh)
        v = z_ref[:, base + 2 * d_ret + c0:base + 2 * d_ret + c0 + dh]
        g = zf(base + 3 * d_ret + c0, base + 3 * d_ret + c0 + dh)
        qr = q * cos + pltpu.roll(q, dh // 2, 1) * sin
        kr = (k * cos + pltpu.roll(k, dh // 2, 1) * sin) * scale
        s = lax.dot_general(qr.astype(BF16), kr.astype(BF16), (((1,), (1,)), ((), ())),
                            preferred_element_type=F32) * dmask_ref[h]
        o = jnp.dot(s.astype(BF16), v, preferred_element_type=F32)
        st = state_scr[h]
        o = o + jnp.dot((qr * qdec_ref[:, c0:c0 + dh]).astype(BF16), st.astype(BF16),
                        preferred_element_type=F32)
        kv = lax.dot_general((kr * kdec_ref[:, c0:c0 + dh]).astype(BF16), v,
                             (((0,), (0,)), ((), ())), preferred_element_type=F32)
        state_scr[h] = st * sdec_ref[:, c0:c0 + dh] + kv
        on = o * lax.rsqrt(jnp.mean(o * o, axis=-1, keepdims=True) + EPS)
        mixed_scr[:, d_conv + c0:d_conv + c0 + dh] = (on * (g * jax.nn.sigmoid(g))).astype(BF16)

    o_ref[...] = x_ref[...] + jnp.dot(mixed_scr[...], wout_ref[...], preferred_element_type=F32)
    tbuf[slot] = o_ref[...]
    for cp in _tm_write_copies(tbuf.at[slot], ot_hbm, step * ts, tsem.at[slot]):
        cp.start()

    @pl.when(step == last)
    def _():
        for s in range(2):
            for cp in _tm_write_copies(tbuf.at[s], ot_hbm, 0, tsem.at[s]):
                cp.wait()


def _retention_tables(seq, ts, n_heads):
    dh = RET_HEAD_DIM
    half = dh // 2
    pos = jnp.arange(seq, dtype=F32)
    inv = ROPE_BASE ** (-jnp.arange(half, dtype=F32) / half)
    ang = pos[:, None] * inv[None, :]
    cos = jnp.cos(ang)
    sin = jnp.sin(ang)
    cos_full = jnp.concatenate([cos, cos], axis=-1)
    sin_signed = jnp.concatenate([-sin, sin], axis=-1)
    log_g = jnp.log1p(-jnp.exp2(-5.0 - jnp.arange(n_heads, dtype=F32)))
    idx = jnp.arange(ts, dtype=F32)
    dist = jnp.abs(idx[:, None] - idx[None, :])
    chunk_id = jnp.arange(ts) // CHUNK
    visible = chunk_id[None, :] <= chunk_id[:, None]
    dmask = jnp.where(visible[None], jnp.exp(log_g[:, None, None] * dist[None]), 0.0)
    rep = lambda a: jnp.repeat(a, dh, axis=-1)
    qdec = rep(jnp.exp(log_g[None, :] * (idx[:, None] + 1.0)))
    kdec = rep(jnp.exp(log_g[None, :] * (ts - 1.0 - idx[:, None])))
    sdec = rep(jnp.exp(log_g * ts)[None, :])
    return cos_full, sin_signed, dmask.astype(F32), qdec, kdec, sdec


def _mixer(z, x2, wout_bf16, conv_w, bsz, seq, layer):
    n_tok, dm = x2.shape
    d_conv = conv_w.shape[1]
    d_ret = dm - d_conv
    n_heads = d_ret // RET_HEAD_DIM
    d_in = z.shape[1]
    ts = _tile(seq, 256)
    ns = seq // ts
    assert bsz * ns >= 2, "the token-major write pipeline needs at least two grid steps"
    cos_full, sin_signed, dmask, qdec, kdec, sdec = _retention_tables(seq, ts, n_heads)
    kern = functools.partial(_mixer_kernel, ts=ts, d_conv=d_conv, n_heads=n_heads)
    const2 = lambda b, s: (0, 0)
    return pl.pallas_call(
        kern,
        grid=(bsz, ns),
        in_specs=[
            pl.BlockSpec((ts, d_in), lambda b, s: (b * ns + s, 0)),
            pl.BlockSpec((ts, dm), lambda b, s: (b * ns + s, 0)),
            pl.BlockSpec((None, dm, dm), lambda b, s: (layer, 0, 0), pipeline_mode=pl.Buffered(1)),
            pl.BlockSpec((CONV_WIDTH, d_conv), const2),
            pl.BlockSpec((ts, RET_HEAD_DIM), lambda b, s: (s, 0)),
            pl.BlockSpec((ts, RET_HEAD_DIM), lambda b, s: (s, 0)),
            pl.BlockSpec((n_heads, ts, ts), lambda b, s: (0, 0, 0)),
            pl.BlockSpec((ts, d_ret), const2),
            pl.BlockSpec((ts, d_ret), const2),
            pl.BlockSpec((1, d_ret), const2),
        ],
        out_specs=[pl.BlockSpec((ts, dm), lambda b, s: (b * ns + s, 0)),
                   pl.BlockSpec(memory_space=pl.ANY)],
        out_shape=[jax.ShapeDtypeStruct((n_tok, dm), F32),
                   jax.ShapeDtypeStruct((n_tok, dm // LANES, LANES), F32)],
        scratch_shapes=[
            pltpu.VMEM((n_heads, RET_HEAD_DIM, RET_HEAD_DIM), F32),
            pltpu.VMEM((ts + SUBLANES, d_conv), F32),
            pltpu.VMEM((ts, dm), BF16),
            pltpu.VMEM((2, ts, dm), F32),
            pltpu.SemaphoreType.DMA((2,)),
        ],
        compiler_params=pltpu.CompilerParams(
            dimension_semantics=("arbitrary", "arbitrary"), vmem_limit_bytes=VMEM_LIMIT),
        name="mixer",
    )(z, x2, wout_bf16, conv_w, cos_full, sin_signed, dmask, qdec, kdec, sdec)


def _router_kernel(x_ref, g_ref, wr_ref, br_ref, route_ref):
    h = _rms(x_ref[...], g_ref[...]).astype(BF16)
    logits = jnp.dot(h, wr_ref[...], preferred_element_type=F32) + br_ref[...]
    lane = lax.broadcasted_iota(jnp.int32, logits.shape, 1)
    neg = jnp.float32(-jnp.inf)
    big = jnp.int32(LANES)

    gl = jnp.where(lane < N_GROUPS, logits, neg)
    gmax = jnp.max(gl, axis=-1, keepdims=True)
    g_sel = jnp.min(jnp.where(gl == gmax, lane, big), axis=-1, keepdims=True)
    p_g = 1.0 / jnp.sum(jnp.exp(gl - gmax), axis=-1, keepdims=True)

    lo = N_GROUPS + g_sel * EXPERTS_PER_GROUP
    el = jnp.where(lane >= lo, jnp.where(lane < lo + EXPERTS_PER_GROUP, logits, neg), neg)
    v1 = jnp.max(el, axis=-1, keepdims=True)
    i1 = jnp.min(jnp.where(el == v1, lane, big), axis=-1, keepdims=True)
    el2 = jnp.where(lane == i1, neg, el)
    v2 = jnp.max(el2, axis=-1, keepdims=True)
    i2 = jnp.min(jnp.where(el2 == v2, lane, big), axis=-1, keepdims=True)
    t = jnp.exp(v2 - v1)
    gate1 = p_g / (1.0 + t)
    gate2 = p_g * t / (1.0 + t)

    e1 = (i1 - N_GROUPS).astype(F32)
    e2 = (i2 - N_GROUPS).astype(F32)
    route_ref[...] = jnp.where(lane == 0, e1, jnp.where(lane == 1, e2, jnp.where(
        lane == 2, gate1, jnp.where(lane == 3, gate2, 0.0))))


def _router(x2, g, w_group, b_group, w_router, b_router):
    n_tok, dm = x2.shape
    n_logit = N_GROUPS + N_EXPERTS
    wr = jnp.zeros((dm, LANES), F32).at[:, :N_GROUPS].set(w_group).at[:, N_GROUPS:n_logit].set(w_router)
    br = jnp.zeros((1, LANES), F32).at[0, :N_GROUPS].set(b_group).at[0, N_GROUPS:n_logit].set(b_router)
    tm = _tile(n_tok, 512)
    return pl.pallas_call(
        _router_kernel,
        grid=(n_tok // tm,),
        in_specs=[
            pl.BlockSpec((tm, dm), lambda i: (i, 0)),
            pl.BlockSpec((1, dm), lambda i: (0, 0)),
            pl.BlockSpec((dm, LANES), lambda i: (0, 0)),
            pl.BlockSpec((1, LANES), lambda i: (0, 0)),
        ],
        out_specs=pl.BlockSpec((tm, LANES), lambda i: (i, 0)),
        out_shape=jax.ShapeDtypeStruct((n_tok, LANES), F32),
        compiler_params=pltpu.CompilerParams(
            dimension_semantics=("parallel",), vmem_limit_bytes=VMEM_LIMIT),
        name="router",
    )(x2, g.reshape(1, dm), wr.astype(BF16), br)


ISSUE_UNROLL = 16


def _row_copy(src_tm, idx, dst, r8, s, sem):
    return pltpu.make_async_copy(src_tm.at[idx], dst.at[r8, :, s, :], sem)


def _issue_rows(idx_ref, idx_base, src_tm, dst, sem, n_rows):
    def body(j, carry):
        for u in range(ISSUE_UNROLL):
            r8 = j * (ISSUE_UNROLL // SUBLANES) + u // SUBLANES
            _row_copy(src_tm, idx_ref[idx_base + j * ISSUE_UNROLL + u], dst, r8, u % SUBLANES, sem).start()
        return carry

    lax.fori_loop(0, n_rows // ISSUE_UNROLL, body, 0)


def _wait_rows(dst, sem):
    pltpu.make_async_copy(dst, dst, sem).wait()


def _experts_kernel(blk_e_ref, n_used_ref, row_tok_ref, xt_hbm, g_ref, w1_ref, w3_ref, w2_ref, yb_hbm,
                    xbuf, h_scr, obuf, sem, osem, *, tmb):
    del blk_e_ref
    i = pl.program_id(0)
    last = pl.num_programs(0) - 1
    n_used = n_used_ref[0]
    slot = i % 2
    nxt = (i + 1) % 2
    n_chunks = xt_hbm.shape[1]
    dm = n_chunks * LANES

    def wait_out(s):
        for cp in _tm_write_copies(obuf.at[s], yb_hbm, 0, osem.at[s]):
            cp.wait()

    @pl.when(i == 0)
    def _():
        _issue_rows(row_tok_ref, 0, xt_hbm, xbuf.at[0], sem.at[0], tmb)

    @pl.when(i <= n_used)
    def _():
        _wait_rows(xbuf.at[slot], sem.at[slot])

    @pl.when(i >= 2)
    def _():
        wait_out(slot)

    @pl.when(i < n_used)
    def _():
        _issue_rows(row_tok_ref, (i + 1) * tmb, xt_hbm, xbuf.at[nxt], sem.at[nxt], tmb)

    @pl.when(i < n_used)
    def _():
        ss = jnp.zeros((tmb, 1), F32)
        for c in range(n_chunks):
            xc = _dense_chunk(xbuf, (slot,), c)
            ss = ss + jnp.sum(xc * xc, axis=-1, keepdims=True)
        rs = lax.rsqrt(ss * (1.0 / dm) + EPS)
        for c in range(n_chunks):
            cs = slice(c * LANES, (c + 1) * LANES)
            h_scr[:, cs] = (_dense_chunk(xbuf, (slot,), c) * rs * g_ref[:, cs]).astype(BF16)
        h = h_scr[...]
        a = jnp.dot(h, w1_ref[...], preferred_element_type=F32)
        b = jnp.dot(h, w3_ref[...], preferred_element_type=F32)
        hid = (a * jax.nn.sigmoid(a) * b).astype(BF16)
        obuf[slot] = jnp.dot(hid, w2_ref[...], preferred_element_type=F32)

    @pl.when(i >= n_used)
    def _():
        obuf[slot] = jnp.zeros(obuf.shape[1:], F32)

    for cp in _tm_write_copies(obuf.at[slot], yb_hbm, i * tmb, osem.at[slot]):
        cp.start()

    @pl.when(i == last)
    def _():
        @pl.when(i < n_used)
        def _():
            _wait_rows(xbuf.at[nxt], sem.at[nxt])

        wait_out(slot)

        @pl.when(i >= 1)
        def _():
            wait_out(nxt)


def _experts(xt, g, w1, w3, w2, blk_e, n_used, row_tok, tmb, layer):
    n_tok, n_chunks, _ = xt.shape
    dm = n_chunks * LANES
    de = w1.shape[3]
    n_rows = row_tok.shape[0] - tmb
    nb = n_rows // tmb
    wmap = lambda i, be, nu, rt: (layer, be[i], 0, 0)
    return pl.pallas_call(
        functools.partial(_experts_kernel, tmb=tmb),
        grid_spec=pltpu.PrefetchScalarGridSpec(
            num_scalar_prefetch=3,
            grid=(nb,),
            in_specs=[
                pl.BlockSpec(memory_space=pl.ANY),
                pl.BlockSpec((1, dm), lambda i, be, nu, rt: (0, 0)),
                pl.BlockSpec((None, None, dm, de), wmap),
                pl.BlockSpec((None, None, dm, de), wmap),
                pl.BlockSpec((None, None, de, dm), wmap),
            ],
            out_specs=pl.BlockSpec(memory_space=pl.ANY),
            scratch_shapes=[
                pltpu.VMEM((2, tmb // SUBLANES, n_chunks, SUBLANES, LANES), F32),
                pltpu.VMEM((tmb, dm), BF16),
                pltpu.VMEM((2, tmb, dm), F32),
                pltpu.SemaphoreType.DMA((2,)), pltpu.SemaphoreType.DMA((2,))],
        ),
        out_shape=jax.ShapeDtypeStruct((n_rows, n_chunks, LANES), F32),
        compiler_params=pltpu.CompilerParams(
            dimension_semantics=("arbitrary",), vmem_limit_bytes=VMEM_LIMIT),
        name="experts",
    )(blk_e, n_used, row_tok, xt, g.reshape(1, dm), w1, w3, w2)


def _combine_kernel(dest_ref, x_ref, route_ref, gf_ref, yb_hbm, o_ref, buf, sem, *, tm, n_tok, final):
    i = pl.program_id(0)
    n_chunks = yb_hbm.shape[1]
    dm = n_chunks * LANES

    def issue(step, slot):
        for k in range(TOP_K):
            _issue_rows(dest_ref, k * n_tok + step * tm, yb_hbm, buf.at[slot, k], sem.at[slot, k], tm)

    @pl.when(i == 0)
    def _():
        issue(0, 0)

    @pl.when(i + 1 < pl.num_programs(0))
    def _():
        issue(i + 1, (i + 1) % 2)

    slot = i % 2
    for k in range(TOP_K):
        _wait_rows(buf.at[slot, k], sem.at[slot, k])
    gate0 = route_ref[:, 2:3]
    gate1 = route_ref[:, 3:4]
    ss = jnp.zeros((tm, 1), F32)
    for c in range(n_chunks):
        cs = slice(c * LANES, (c + 1) * LANES)
        oc = (x_ref[:, cs] + gate0 * _dense_chunk(buf, (slot, 0), c)
              + gate1 * _dense_chunk(buf, (slot, 1), c))
        o_ref[:, cs] = oc
        if final:
            ss = ss + jnp.sum(oc * oc, axis=-1, keepdims=True)
    if final:
        rs = lax.rsqrt(ss * (1.0 / dm) + EPS)
        for c in range(n_chunks):
            cs = slice(c * LANES, (c + 1) * LANES)
            o_ref[:, cs] = o_ref[:, cs] * rs * gf_ref[:, cs]


def _combine(x2, route, yb, dest, gf, final):
    n_tok, dm = x2.shape
    n_chunks = dm // LANES
    tm = _tile(n_tok, 256)
    return pl.pallas_call(
        functools.partial(_combine_kernel, tm=tm, n_tok=n_tok, final=final),
        grid_spec=pltpu.PrefetchScalarGridSpec(
            num_scalar_prefetch=1,
            grid=(n_tok // tm,),
            in_specs=[
                pl.BlockSpec((tm, dm), lambda i, d: (i, 0)),
                pl.BlockSpec((tm, LANES), lambda i, d: (i, 0)),
                pl.BlockSpec((1, dm), lambda i, d: (0, 0)),
                pl.BlockSpec(memory_space=pl.ANY),
            ],
            out_specs=pl.BlockSpec((tm, dm), lambda i, d: (i, 0)),
            scratch_shapes=[pltpu.VMEM((2, TOP_K, tm // SUBLANES, n_chunks, SUBLANES, LANES), F32),
                            pltpu.SemaphoreType.DMA((2, TOP_K))],
        ),
        out_shape=jax.ShapeDtypeStruct((n_tok, dm), F32),
        compiler_params=pltpu.CompilerParams(
            dimension_semantics=("arbitrary",), vmem_limit_bytes=VMEM_LIMIT),
        name="combine",
    )(dest, x2, route, gf.reshape(1, dm), yb)


def _plan(route, tmb, n_blocks):
    n_tok = route.shape[0]
    n_assign = n_tok * TOP_K
    e_flat = route[:, :TOP_K].astype(jnp.int32).reshape(n_assign)
    onehot = (e_flat[:, None] == jnp.arange(N_EXPERTS, dtype=jnp.int32)[None, :]).astype(jnp.int32)
    csum = jnp.cumsum(onehot, axis=0)
    counts = csum[-1]
    rank = jnp.take_along_axis(csum, e_flat[:, None], axis=1)[:, 0] - 1
    pcounts = (counts + tmb - 1) // tmb * tmb
    pends = jnp.cumsum(pcounts)
    pstarts = pends - pcounts
    dest = (pstarts[e_flat] + rank).astype(jnp.int32)
    tok = jnp.arange(n_assign, dtype=jnp.int32) // TOP_K
    n_rows = (n_blocks + 1) * tmb
    pad_tok = jnp.arange(n_rows, dtype=jnp.int32) % n_tok
    row_tok = pad_tok.at[dest].set(tok)
    blk_start = jnp.arange(n_blocks, dtype=jnp.int32) * tmb
    blk_e = jnp.minimum(jnp.sum(pends[None, :] <= blk_start[:, None], axis=1),
                        N_EXPERTS - 1).astype(jnp.int32)
    n_used = (pends[-1:] // tmb).astype(jnp.int32)
    dest_kmajor = dest.reshape(n_tok, TOP_K).T.reshape(n_assign)
    return dest_kmajor, row_tok, blk_e, n_used


def kernel(x, norm_mix_g, w_in, conv_w, w_out, norm_ffn_g, w_group, b_group, w_router, b_router,
           w_expert_gate, w_expert_up, w_expert_down, final_norm_g):
    bsz, seq, dm = x.shape
    depth = w_in.shape[0]
    n_tok = bsz * seq
    tmb = _tile(n_tok * TOP_K, 256)
    n_blocks = n_tok * TOP_K // tmb + N_EXPERTS
    x2 = x.reshape(n_tok, dm)
    w_in_b, w_out_b = w_in.astype(BF16), w_out.astype(BF16)
    w1_b, w3_b, w2_b = w_expert_gate.astype(BF16), w_expert_up.astype(BF16), w_expert_down.astype(BF16)
    for l in range(depth):
        z = _in_proj(x2, norm_mix_g[l], w_in_b, l)
        x2, xt = _mixer(z, x2, w_out_b, conv_w[l], bsz, seq, l)
        route = _router(x2, norm_ffn_g[l], w_group[l], b_group[l], w_router[l], b_router[l])
        dest, row_tok, blk_e, n_used = _plan(route, tmb, n_blocks)
        yb = _experts(xt, norm_ffn_g[l], w1_b, w3_b, w2_b, blk_e, n_used, row_tok, tmb, l)
        x2 = _combine(x2, route, yb, dest, final_norm_g, final=(l == depth - 1))
    return x2.reshape(bsz, seq, dm)
```

```python
import functools

import jax
import jax.numpy as jnp
from jax import lax
from jax.experimental import pallas as pl
from jax.experimental.pallas import tpu as pltpu

CHUNK = 64
RET_HEAD_DIM = 128
CONV_WIDTH = 3
ROPE_BASE = 10000.0
N_GROUPS = 4
EXPERTS_PER_GROUP = 8
N_EXPERTS = N_GROUPS * EXPERTS_PER_GROUP
TOP_K = 2
EPS = 1e-6

LANES = 128
SUBLANES = 8
VMEM_LIMIT = 56 * 1024 * 1024

F32 = jnp.float32
BF16 = jnp.bfloat16


def _tile(n, want):
    t = min(n, want)
    while n % t:
        t //= 2
    return t


def _rms(x, g):
    return x * lax.rsqrt(jnp.mean(x * x, axis=-1, keepdims=True) + EPS) * g


def _tm_write_copies(src_dense, dst_tm_hbm, row0, sem):
    rows = src_dense.shape[0]
    return [pltpu.make_async_copy(src_dense.at[:, pl.ds(c * LANES, LANES)],
                                  dst_tm_hbm.at[pl.ds(row0, rows), c, :], sem)
            for c in range(dst_tm_hbm.shape[1])]


def _dense_chunk(buf, lead, c):
    v = buf[(*lead, slice(None), c)]
    return v.reshape(v.shape[0] * SUBLANES, LANES)


def _in_proj_kernel(x_ref, g_ref, w_ref, o_ref, h_scr):
    @pl.when(pl.program_id(1) == 0)
    def _():
        h_scr[...] = _rms(x_ref[...], g_ref[...]).astype(BF16)

    o_ref[...] = jnp.dot(h_scr[...], w_ref[...], preferred_element_type=F32).astype(o_ref.dtype)


def _in_proj(x2, g, w_bf16, layer):
    n_tok, dm = x2.shape
    n_out = w_bf16.shape[2]
    tm = _tile(n_tok, 1024)
    tn = _tile(n_out, 1024)
    return pl.pallas_call(
        _in_proj_kernel,
        grid=(n_tok // tm, n_out // tn),
        in_specs=[
            pl.BlockSpec((tm, dm), lambda i, j: (i, 0)),
            pl.BlockSpec((1, dm), lambda i, j: (0, 0)),
            pl.BlockSpec((None, dm, tn), lambda i, j: (layer, 0, j)),
        ],
        out_specs=pl.BlockSpec((tm, tn), lambda i, j: (i, j)),
        out_shape=jax.ShapeDtypeStruct((n_tok, n_out), BF16),
        scratch_shapes=[pltpu.VMEM((tm, dm), BF16)],
        compiler_params=pltpu.CompilerParams(
            dimension_semantics=("parallel", "arbitrary"), vmem_limit_bytes=VMEM_LIMIT),
        name="in_proj",
    )(x2, g.reshape(1, dm), w_bf16)


def _mixer_kernel(z_ref, x_ref, wout_ref, convw_ref, cos_ref, sin_ref, dmask_ref, qdec_ref,
                  kdec_ref, sdec_ref, o_ref, ot_hbm, state_scr, u_scr, mixed_scr, tbuf, tsem,
                  *, ts, d_conv, n_heads):
    dh = RET_HEAD_DIM
    d_ret = n_heads * dh
    step = pl.program_id(0) * pl.num_programs(1) + pl.program_id(1)
    last = pl.num_programs(0) * pl.num_programs(1) - 1
    slot = step % 2

    @pl.when(step >= 2)
    def _():
        for cp in _tm_write_copies(tbuf.at[slot], ot_hbm, 0, tsem.at[slot]):
            cp.wait()

    @pl.when(pl.program_id(1) == 0)
    def _():
        state_scr[...] = jnp.zeros_like(state_scr)
        u_scr[0:SUBLANES, :] = jnp.zeros((SUBLANES, d_conv), F32)

    zf = lambda lo, hi: z_ref[:, lo:hi].astype(F32)
    u_scr[SUBLANES:SUBLANES + ts, :] = zf(d_conv, 2 * d_conv) * zf(2 * d_conv, 3 * d_conv)
    y = (convw_ref[2:3, :] * u_scr[SUBLANES:SUBLANES + ts, :]
         + convw_ref[1:2, :] * u_scr[SUBLANES - 1:SUBLANES - 1 + ts, :]
         + convw_ref[0:1, :] * u_scr[SUBLANES - 2:SUBLANES - 2 + ts, :])
    mixed_scr[:, 0:d_conv] = (zf(0, d_conv) * y).astype(BF16)
    u_scr[0:SUBLANES, :] = u_scr[ts:ts + SUBLANES, :]
    o_ref[...] = x_ref[...] + jnp.dot(mixed_scr[:, 0:d_conv], wout_ref[0:d_conv, :],
                                      preferred_element_type=F32)

    cos = cos_ref[...]
    sin = sin_ref[...]
    scale = RET_HEAD_DIM ** -0.5
    base = 3 * d_conv
    for h in range(n_heads):
        c0 = h * dh
        q = zf(base + c0, base + c0 + dh)
        k = zf(base + d_ret + c0, base + d_ret + c0 + dh)
        v = z_ref[:, base + 2 * d_ret + c0:base + 2 * d_ret + c0 + dh]
        g = zf(base + 3 * d_ret + c0, base + 3 * d_ret + c0 + dh)
        qr = q * cos + pltpu.roll(q, dh // 2, 1) * sin
        kr = (k * cos + pltpu.roll(k, dh // 2, 1) * sin) * scale
        s = lax.dot_general(qr.astype(BF16), kr.astype(BF16), (((1,), (1,)), ((), ())),
                            preferred_element_type=F32) * dmask_ref[h]
        o = jnp.dot(s.astype(BF16), v, preferred_element_type=F32)
        st = state_scr[h]
        o = o + jnp.dot((qr * qdec_ref[:, c0:c0 + dh]).astype(BF16), st.astype(BF16),
                        preferred_element_type=F32)
        kv = lax.dot_general((kr * kdec_ref[:, c0:c0 + dh]).astype(BF16), v,
                             (((0,), (0,)), ((), ())), preferred_element_type=F32)
        state_scr[h] = st * sdec_ref[:, c0:c0 + dh] + kv
        on = o * lax.rsqrt(jnp.mean(o * o, axis=-1, keepdims=True) + EPS)
        mixed_scr[:, d_conv + c0:d_conv + c0 + dh] = (on * (g * jax.nn.sigmoid(g))).astype(BF16)

    o_ref[...] += jnp.dot(mixed_scr[:, d_conv:], wout_ref[d_conv:, :], preferred_element_type=F32)
    tbuf[slot] = o_ref[...]
    for cp in _tm_write_copies(tbuf.at[slot], ot_hbm, step * ts, tsem.at[slot]):
        cp.start()

    @pl.when(step == last)
    def _():
        for s in range(2):
            for cp in _tm_write_copies(tbuf.at[s], ot_hbm, 0, tsem.at[s]):
                cp.wait()


def _retention_tables(seq, ts, n_heads):
    dh = RET_HEAD_DIM
    half = dh // 2
    pos = jnp.arange(seq, dtype=F32)
    inv = ROPE_BASE ** (-jnp.arange(half, dtype=F32) / half)
    ang = pos[:, None] * inv[None, :]
    cos = jnp.cos(ang)
    sin = jnp.sin(ang)
    cos_full = jnp.concatenate([cos, cos], axis=-1)
    sin_signed = jnp.concatenate([-sin, sin], axis=-1)
    log_g = jnp.log1p(-jnp.exp2(-5.0 - jnp.arange(n_heads, dtype=F32)))
    idx = jnp.arange(ts, dtype=F32)
    dist = jnp.abs(idx[:, None] - idx[None, :])
    chunk_id = jnp.arange(ts) // CHUNK
    visible = chunk_id[None, :] <= chunk_id[:, None]
    dmask = jnp.where(visible[None], jnp.exp(log_g[:, None, None] * dist[None]), 0.0)
    rep = lambda a: jnp.repeat(a, dh, axis=-1)
    qdec = rep(jnp.exp(log_g[None, :] * (idx[:, None] + 1.0)))
    kdec = rep(jnp.exp(log_g[None, :] * (ts - 1.0 - idx[:, None])))
    sdec = rep(jnp.exp(log_g * ts)[None, :])
    return cos_full, sin_signed, dmask.astype(F32), qdec, kdec, sdec


def _mixer(z, x2, wout_bf16, conv_w, bsz, seq, layer):
    n_tok, dm = x2.shape
    d_conv = conv_w.shape[1]
    d_ret = dm - d_conv
    n_heads = d_ret // RET_HEAD_DIM
    d_in = z.shape[1]
    ts = _tile(seq, 256)
    ns = seq // ts
    assert bsz * ns >= 2, "the token-major write pipeline needs at least two grid steps"
    cos_full, sin_signed, dmask, qdec, kdec, sdec = _retention_tables(seq, ts, n_heads)
    kern = functools.partial(_mixer_kernel, ts=ts, d_conv=d_conv, n_heads=n_heads)
    const2 = lambda b, s: (0, 0)
    return pl.pallas_call(
        kern,
        grid=(bsz, ns),
        in_specs=[
            pl.BlockSpec((ts, d_in), lambda b, s: (b * ns + s, 0)),
            pl.BlockSpec((ts, dm), lambda b, s: (b * ns + s, 0)),
            pl.BlockSpec((None, dm, dm), lambda b, s: (layer, 0, 0), pipeline_mode=pl.Buffered(1)),
            pl.BlockSpec((CONV_WIDTH, d_conv), const2),
            pl.BlockSpec((ts, RET_HEAD_DIM), lambda b, s: (s, 0)),
            pl.BlockSpec((ts, RET_HEAD_DIM), lambda b, s: (s, 0)),
            pl.BlockSpec((n_heads, ts, ts), lambda b, s: (0, 0, 0)),
            pl.BlockSpec((ts, d_ret), const2),
            pl.BlockSpec((ts, d_ret), const2),
            pl.BlockSpec((1, d_ret), const2),
        ],
        out_specs=[pl.BlockSpec((ts, dm), lambda b, s: (b * ns + s, 0)),
                   pl.BlockSpec(memory_space=pl.ANY)],
        out_shape=[jax.ShapeDtypeStruct((n_tok, dm), F32),
                   jax.ShapeDtypeStruct((n_tok, dm // LANES, LANES), F32)],
        scratch_shapes=[
            pltpu.VMEM((n_heads, RET_HEAD_DIM, RET_HEAD_DIM), F32),
            pltpu.VMEM((ts + SUBLANES, d_conv), F32),
            pltpu.VMEM((ts, dm), BF16),
            pltpu.VMEM((2, ts, dm), F32),
            pltpu.SemaphoreType.DMA((2,)),
        ],
        compiler_params=pltpu.CompilerParams(
            dimension_semantics=("arbitrary", "arbitrary"), vmem_limit_bytes=VMEM_LIMIT),
        name="mixer",
    )(z, x2, wout_bf16, conv_w, cos_full, sin_signed, dmask, qdec, kdec, sdec)


def _router_kernel(x_ref, g_ref, wr_ref, br_ref, tri_ref, route_ref, cnt_ref, cnt_scr):
    @pl.when(pl.program_id(0) == 0)
    def _():
        cnt_scr[...] = jnp.zeros_like(cnt_scr)

    h = _rms(x_ref[...], g_ref[...]).astype(BF16)
    logits = jnp.dot(h, wr_ref[...], preferred_element_type=F32) + br_ref[...]
    lane = lax.broadcasted_iota(jnp.int32, logits.shape, 1)
    neg = jnp.float32(-jnp.inf)
    big = jnp.int32(LANES)

    gl = jnp.where(lane < N_GROUPS, logits, neg)
    gmax = jnp.max(gl, axis=-1, keepdims=True)
    g_sel = jnp.min(jnp.where(gl == gmax, lane, big), axis=-1, keepdims=True)
    p_g = 1.0 / jnp.sum(jnp.exp(gl - gmax), axis=-1, keepdims=True)

    lo = N_GROUPS + g_sel * EXPERTS_PER_GROUP
    el = jnp.where(lane >= lo, jnp.where(lane < lo + EXPERTS_PER_GROUP, logits, neg), neg)
    v1 = jnp.max(el, axis=-1, keepdims=True)
    i1 = jnp.min(jnp.where(el == v1, lane, big), axis=-1, keepdims=True)
    el2 = jnp.where(lane == i1, neg, el)
    v2 = jnp.max(el2, axis=-1, keepdims=True)
    i2 = jnp.min(jnp.where(el2 == v2, lane, big), axis=-1, keepdims=True)
    t = jnp.exp(v2 - v1)
    gate1 = p_g / (1.0 + t)
    gate2 = p_g * t / (1.0 + t)

    tri = tri_ref[...]
    cnt = cnt_scr[...]
    hot1 = jnp.where(lane == i1, 1.0, 0.0)
    hot2 = jnp.where(lane == i2, 1.0, 0.0)
    before1 = jnp.dot(tri, hot1.astype(BF16), preferred_element_type=F32) + cnt
    cnt = cnt + jnp.sum(hot1, axis=0, keepdims=True)
    before2 = jnp.dot(tri, hot2.astype(BF16), preferred_element_type=F32) + cnt
    cnt = cnt + jnp.sum(hot2, axis=0, keepdims=True)
    rank1 = jnp.sum(hot1 * before1, axis=-1, keepdims=True)
    rank2 = jnp.sum(hot2 * before2, axis=-1, keepdims=True)
    cnt_scr[...] = cnt
    cnt_ref[...] = cnt

    e1 = (i1 - N_GROUPS).astype(F32)
    e2 = (i2 - N_GROUPS).astype(F32)
    cols = (e1, e2, gate1, gate2, rank1, rank2)
    route = jnp.zeros(logits.shape, F32)
    for j, col in enumerate(cols):
        route = jnp.where(lane == j, col, route)
    route_ref[...] = route


def _router(x2, g, w_group, b_group, w_router, b_router):
    n_tok, dm = x2.shape
    n_logit = N_GROUPS + N_EXPERTS
    wr = jnp.zeros((dm, LANES), F32).at[:, :N_GROUPS].set(w_group).at[:, N_GROUPS:n_logit].set(w_router)
    br = jnp.zeros((1, LANES), F32).at[0, :N_GROUPS].set(b_group).at[0, N_GROUPS:n_logit].set(b_router)
    tm = _tile(n_tok, 512)
    row = jnp.arange(tm, dtype=jnp.int32)
    tri = (row[None, :] < row[:, None]).astype(BF16)
    route, cnt = pl.pallas_call(
        _router_kernel,
        grid=(n_tok // tm,),
        in_specs=[
            pl.BlockSpec((tm, dm), lambda i: (i, 0)),
            pl.BlockSpec((1, dm), lambda i: (0, 0)),
            pl.BlockSpec((dm, LANES), lambda i: (0, 0)),
            pl.BlockSpec((1, LANES), lambda i: (0, 0)),
            pl.BlockSpec((tm, tm), lambda i: (0, 0)),
        ],
        out_specs=[pl.BlockSpec((tm, LANES), lambda i: (i, 0)),
                   pl.BlockSpec((1, LANES), lambda i: (0, 0))],
        out_shape=[jax.ShapeDtypeStruct((n_tok, LANES), F32),
                   jax.ShapeDtypeStruct((1, LANES), F32)],
        scratch_shapes=[pltpu.VMEM((1, LANES), F32)],
        compiler_params=pltpu.CompilerParams(
            dimension_semantics=("arbitrary",), vmem_limit_bytes=VMEM_LIMIT),
        name="router",
    )(x2, g.reshape(1, dm), wr.astype(BF16), br, tri)
    return route, cnt[0, N_GROUPS:n_logit].astype(jnp.int32)


def _row_copy(src_tm, idx, dst, r8, s, sem):
    return pltpu.make_async_copy(src_tm.at[idx], dst.at[r8, :, s, :], sem)


def _issue_rows(idx_ref, idx_base, src_tm, dst, sem, n_rows):
    for r in range(n_rows):
        _row_copy(src_tm, idx_ref[idx_base + r], dst, r // SUBLANES, r % SUBLANES, sem).start()


def _wait_rows(dst, sem):
    pltpu.make_async_copy(dst, dst, sem).wait()


def _experts_kernel(blk_e_ref, n_used_ref, row_tok_ref, xt_hbm, g_ref, w1_ref, w3_ref, w2_ref, yb_hbm,
                    xbuf, h_scr, obuf, sem, osem, *, tmb):
    del blk_e_ref
    i = pl.program_id(0)
    last = pl.num_programs(0) - 1
    n_used = n_used_ref[0]
    slot = i % 2
    nxt = (i + 1) % 2
    n_chunks = xt_hbm.shape[1]
    dm = n_chunks * LANES

    def wait_out(s):
        for cp in _tm_write_copies(obuf.at[s], yb_hbm, 0, osem.at[s]):
            cp.wait()

    @pl.when(i == 0)
    def _():
        _issue_rows(row_tok_ref, 0, xt_hbm, xbuf.at[0], sem.at[0], tmb)

    @pl.when(i <= n_used)
    def _():
        _wait_rows(xbuf.at[slot], sem.at[slot])

    @pl.when(i >= 2)
    def _():
        wait_out(slot)

    @pl.when(i < n_used)
    def _():
        _issue_rows(row_tok_ref, (i + 1) * tmb, xt_hbm, xbuf.at[nxt], sem.at[nxt], tmb)

    @pl.when(i < n_used)
    def _():
        ss = jnp.zeros((tmb, 1), F32)
        for c in range(n_chunks):
            xc = _dense_chunk(xbuf, (slot,), c)
            ss = ss + jnp.sum(xc * xc, axis=-1, keepdims=True)
        rs = lax.rsqrt(ss * (1.0 / dm) + EPS)
        for c in range(n_chunks):
            cs = slice(c * LANES, (c + 1) * LANES)
            h_scr[:, cs] = (_dense_chunk(xbuf, (slot,), c) * rs * g_ref[:, cs]).astype(BF16)
        h = h_scr[...]
        a = jnp.dot(h, w1_ref[...], preferred_element_type=F32)
        b = jnp.dot(h, w3_ref[...], preferred_element_type=F32)
        hid = (a * jax.nn.sigmoid(a) * b).astype(BF16)
        obuf[slot] = jnp.dot(hid, w2_ref[...], preferred_element_type=F32)

    @pl.when(i >= n_used)
    def _():
        obuf[slot] = jnp.zeros(obuf.shape[1:], F32)

    for cp in _tm_write_copies(obuf.at[slot], yb_hbm, i * tmb, osem.at[slot]):
        cp.start()

    @pl.when(i == last)
    def _():
        @pl.when(i < n_used)
        def _():
            _wait_rows(xbuf.at[nxt], sem.at[nxt])

        wait_out(slot)

        @pl.when(i >= 1)
        def _():
            wait_out(nxt)


def _experts(xt, g, w1, w3, w2, blk_e, n_used, row_tok, tmb, layer):
    n_tok, n_chunks, _ = xt.shape
    dm = n_chunks * LANES
    de = w1.shape[3]
    n_rows = row_tok.shape[0] - tmb
    nb = n_rows // tmb
    wmap = lambda i, be, nu, rt: (layer, be[i], 0, 0)
    return pl.pallas_call(
        functools.partial(_experts_kernel, tmb=tmb),
        grid_spec=pltpu.PrefetchScalarGridSpec(
            num_scalar_prefetch=3,
            grid=(nb,),
            in_specs=[
                pl.BlockSpec(memory_space=pl.ANY),
                pl.BlockSpec((1, dm), lambda i, be, nu, rt: (0, 0)),
                pl.BlockSpec((None, None, dm, de), wmap),
                pl.BlockSpec((None, None, dm, de), wmap),
                pl.BlockSpec((None, None, de, dm), wmap),
            ],
            out_specs=pl.BlockSpec(memory_space=pl.ANY),
            scratch_shapes=[
                pltpu.VMEM((2, tmb // SUBLANES, n_chunks, SUBLANES, LANES), F32),
                pltpu.VMEM((tmb, dm), BF16),
                pltpu.VMEM((2, tmb, dm), F32),
                pltpu.SemaphoreType.DMA((2,)), pltpu.SemaphoreType.DMA((2,))],
        ),
        out_shape=jax.ShapeDtypeStruct((n_rows, n_chunks, LANES), F32),
        compiler_params=pltpu.CompilerParams(
            dimension_semantics=("arbitrary",), vmem_limit_bytes=VMEM_LIMIT),
        name="experts",
    )(blk_e, n_used, row_tok, xt, g.reshape(1, dm), w1, w3, w2)


def _combine_kernel(dest_ref, x_ref, route_ref, gf_ref, yb_hbm, o_ref, buf, sem, *, tm, n_tok, final):
    i = pl.program_id(0)
    n_chunks = yb_hbm.shape[1]
    dm = n_chunks * LANES

    def issue(step, slot):
        for k in range(TOP_K):
            _issue_rows(dest_ref, k * n_tok + step * tm, yb_hbm, buf.at[slot, k], sem.at[slot, k], tm)

    @pl.when(i == 0)
    def _():
        issue(0, 0)

    @pl.when(i + 1 < pl.num_programs(0))
    def _():
        issue(i + 1, (i + 1) % 2)

    slot = i % 2
    for k in range(TOP_K):
        _wait_rows(buf.at[slot, k], sem.at[slot, k])
    gate0 = route_ref[:, 2:3]
    gate1 = route_ref[:, 3:4]
    ss = jnp.zeros((tm, 1), F32)
    for c in range(n_chunks):
        cs = slice(c * LANES, (c + 1) * LANES)
        oc = (x_ref[:, cs] + gate0 * _dense_chunk(buf, (slot, 0), c)
              + gate1 * _dense_chunk(buf, (slot, 1), c))
        o_ref[:, cs] = oc
        if final:
            ss = ss + jnp.sum(oc * oc, axis=-1, keepdims=True)
    if final:
        rs = lax.rsqrt(ss * (1.0 / dm) + EPS)
        for c in range(n_chunks):
            cs = slice(c * LANES, (c + 1) * LANES)
            o_ref[:, cs] = o_ref[:, cs] * rs * gf_ref[:, cs]


def _combine(x2, route, yb, dest, gf, final):
    n_tok, dm = x2.shape
    n_chunks = dm // LANES
    tm = _tile(n_tok, 256)
    return pl.pallas_call(
        functools.partial(_combine_kernel, tm=tm, n_tok=n_tok, final=final),
        grid_spec=pltpu.PrefetchScalarGridSpec(
            num_scalar_prefetch=1,
            grid=(n_tok // tm,),
            in_specs=[
                pl.BlockSpec((tm, dm), lambda i, d: (i, 0)),
                pl.BlockSpec((tm, LANES), lambda i, d: (i, 0)),
                pl.BlockSpec((1, dm), lambda i, d: (0, 0)),
                pl.BlockSpec(memory_space=pl.ANY),
            ],
            out_specs=pl.BlockSpec((tm, dm), lambda i, d: (i, 0)),
            scratch_shapes=[pltpu.VMEM((2, TOP_K, tm // SUBLANES, n_chunks, SUBLANES, LANES), F32),
                            pltpu.SemaphoreType.DMA((2, TOP_K))],
        ),
        out_shape=jax.ShapeDtypeStruct((n_tok, dm), F32),
        compiler_params=pltpu.CompilerParams(
            dimension_semantics=("arbitrary",), vmem_limit_bytes=VMEM_LIMIT),
        name="combine",
    )(dest, x2, route, gf.reshape(1, dm), yb)


SCATTER_UNROLL = 16


def _row_tok_kernel(dest_ref, pad_hbm, row_tok_hbm, rt_smem, sem, *, n_tok):
    fill = pltpu.make_async_copy(pad_hbm, rt_smem, sem)
    fill.start()
    fill.wait()
    for k in range(TOP_K):
        def scatter(j, carry):
            for u in range(SCATTER_UNROLL):
                t = j * SCATTER_UNROLL + u
                rt_smem[dest_ref[k * n_tok + t]] = t
            return carry

        lax.fori_loop(0, n_tok // SCATTER_UNROLL, scatter, 0)
    out = pltpu.make_async_copy(rt_smem, row_tok_hbm, sem)
    out.start()
    out.wait()


def _row_tok(dest_kmajor, n_tok, n_rows):
    assert n_tok % SCATTER_UNROLL == 0
    pad_tok = jnp.arange(n_rows, dtype=jnp.int32) % n_tok
    return pl.pallas_call(
        functools.partial(_row_tok_kernel, n_tok=n_tok),
        in_specs=[pl.BlockSpec(memory_space=pltpu.SMEM), pl.BlockSpec(memory_space=pl.ANY)],
        out_specs=pl.BlockSpec(memory_space=pl.ANY),
        out_shape=jax.ShapeDtypeStruct((n_rows,), jnp.int32),
        scratch_shapes=[pltpu.SMEM((n_rows,), jnp.int32), pltpu.SemaphoreType.DMA],
        name="row_tok",
    )(dest_kmajor, pad_tok)


def _plan(route, counts, tmb, n_blocks):
    n_tok = route.shape[0]
    e = route[:, 0:TOP_K].astype(jnp.int32)
    rank = route[:, 4:4 + TOP_K].astype(jnp.int32)
    pcounts = (counts + tmb - 1) // tmb * tmb
    pends = jnp.cumsum(pcounts)
    pstarts = pends - pcounts
    dest_kmajor = (pstarts[e] + rank).T.reshape(n_tok * TOP_K)
    row_tok = _row_tok(dest_kmajor, n_tok, (n_blocks + 1) * tmb)
    blk_start = jnp.arange(n_blocks, dtype=jnp.int32) * tmb
    blk_e = jnp.minimum(jnp.sum(pends[None, :] <= blk_start[:, None], axis=1),
                        N_EXPERTS - 1).astype(jnp.int32)
    n_used = (pends[-1:] // tmb).astype(jnp.int32)
    return dest_kmajor, row_tok, blk_e, n_used


def kernel(x, norm_mix_g, w_in, conv_w, w_out, norm_ffn_g, w_group, b_group, w_router, b_router,
           w_expert_gate, w_expert_up, w_expert_down, final_norm_g):
    bsz, seq, dm = x.shape
    depth = w_in.shape[0]
    n_tok = bsz * seq
    tmb = _tile(n_tok * TOP_K, 256)
    n_blocks = n_tok * TOP_K // tmb + N_EXPERTS
    x2 = x.reshape(n_tok, dm)
    w_in_b, w_out_b = w_in.astype(BF16), w_out.astype(BF16)
    w1_b, w3_b, w2_b = w_expert_gate.astype(BF16), w_expert_up.astype(BF16), w_expert_down.astype(BF16)
    for l in range(depth):
        z = _in_proj(x2, norm_mix_g[l], w_in_b, l)
        x2, xt = _mixer(z, x2, w_out_b, conv_w[l], bsz, seq, l)
        route, counts = _router(x2, norm_ffn_g[l], w_group[l], b_group[l], w_router[l], b_router[l])
        dest, row_tok, blk_e, n_used = _plan(route, counts, tmb, n_blocks)
        yb = _experts(xt, norm_ffn_g[l], w1_b, w3_b, w2_b, blk_e, n_used, row_tok, tmb, l)
        x2 = _combine(x2, route, yb, dest, final_norm_g, final=(l == depth - 1))
    return x2.reshape(bsz, seq, dm)
```

```python
import functools

import jax
import jax.numpy as jnp
from jax import lax
from jax.experimental import pallas as pl
from jax.experimental.pallas import tpu as pltpu

CHUNK = 64
RET_HEAD_DIM = 128
CONV_WIDTH = 3
ROPE_BASE = 10000.0
N_GROUPS = 4
EXPERTS_PER_GROUP = 8
N_EXPERTS = N_GROUPS * EXPERTS_PER_GROUP
TOP_K = 2
EPS = 1e-6

LANES = 128
SUBLANES = 8
VMEM_LIMIT = 56 * 1024 * 1024

F32 = jnp.float32
BF16 = jnp.bfloat16


def _tile(n, want):
    t = min(n, want)
    while n % t:
        t //= 2
    return t


def _rms(x, g):
    return x * lax.rsqrt(jnp.mean(x * x, axis=-1, keepdims=True) + EPS) * g


def _tm_write_copies(src_dense, dst_tm_hbm, row0, sem):
    rows = src_dense.shape[0]
    return [pltpu.make_async_copy(src_dense.at[:, pl.ds(c * LANES, LANES)],
                                  dst_tm_hbm.at[pl.ds(row0, rows), c, :], sem)
            for c in range(dst_tm_hbm.shape[1])]


def _dense_chunk(buf, lead, c):
    v = buf[(*lead, slice(None), c)]
    return v.reshape(v.shape[0] * SUBLANES, LANES)


def _in_proj_kernel(x_ref, g_ref, w_ref, o_ref, h_scr):
    @pl.when(pl.program_id(1) == 0)
    def _():
        h_scr[...] = _rms(x_ref[...], g_ref[...]).astype(BF16)

    o_ref[...] = jnp.dot(h_scr[...], w_ref[...], preferred_element_type=F32).astype(o_ref.dtype)


def _in_proj(x2, g, w_bf16, layer):
    n_tok, dm = x2.shape
    n_out = w_bf16.shape[2]
    tm = _tile(n_tok, 1024)
    tn = _tile(n_out, 1024)
    return pl.pallas_call(
        _in_proj_kernel,
        grid=(n_tok // tm, n_out // tn),
        in_specs=[
            pl.BlockSpec((tm, dm), lambda i, j: (i, 0)),
            pl.BlockSpec((1, dm), lambda i, j: (0, 0)),
            pl.BlockSpec((None, dm, tn), lambda i, j: (layer, 0, j)),
        ],
        out_specs=pl.BlockSpec((tm, tn), lambda i, j: (i, j)),
        out_shape=jax.ShapeDtypeStruct((n_tok, n_out), BF16),
        scratch_shapes=[pltpu.VMEM((tm, dm), BF16)],
        compiler_params=pltpu.CompilerParams(
            dimension_semantics=("parallel", "arbitrary"), vmem_limit_bytes=VMEM_LIMIT),
        name="in_proj",
    )(x2, g.reshape(1, dm), w_bf16)


def _mixer_kernel(z_ref, x_ref, wout_ref, convw_ref, cos_ref, sin_ref, dmask_ref, qdec_ref,
                  kdec_ref, sdec_ref, o_ref, ot_hbm, state_scr, u_scr, mixed_scr, tbuf, tsem,
                  *, ts, d_conv, n_heads):
    dh = RET_HEAD_DIM
    d_ret = n_heads * dh
    step = pl.program_id(0) * pl.num_programs(1) + pl.program_id(1)
    last = pl.num_programs(0) * pl.num_programs(1) - 1
    slot = step % 2

    @pl.when(step >= 2)
    def _():
        for cp in _tm_write_copies(tbuf.at[slot], ot_hbm, 0, tsem.at[slot]):
            cp.wait()

    @pl.when(pl.program_id(1) == 0)
    def _():
        state_scr[...] = jnp.zeros_like(state_scr)
        u_scr[0:SUBLANES, :] = jnp.zeros((SUBLANES, d_conv), F32)

    zf = lambda lo, hi: z_ref[:, lo:hi].astype(F32)
    u_scr[SUBLANES:SUBLANES + ts, :] = zf(d_conv, 2 * d_conv) * zf(2 * d_conv, 3 * d_conv)
    y = (convw_ref[2:3, :] * u_scr[SUBLANES:SUBLANES + ts, :]
         + convw_ref[1:2, :] * u_scr[SUBLANES - 1:SUBLANES - 1 + ts, :]
         + convw_ref[0:1, :] * u_scr[SUBLANES - 2:SUBLANES - 2 + ts, :])
    mixed_scr[:, 0:d_conv] = (zf(0, d_conv) * y).astype(BF16)
    u_scr[0:SUBLANES, :] = u_scr[ts:ts + SUBLANES, :]
    o_ref[...] = x_ref[...] + jnp.dot(mixed_scr[:, 0:d_conv], wout_ref[0:d_conv, :],
                                      preferred_element_type=F32)

    cos = cos_ref[...]
    sin = sin_ref[...]
    scale = RET_HEAD_DIM ** -0.5
    base = 3 * d_conv
    for h in range(n_heads):
        c0 = h * dh
        q = zf(base + c0, base + c0 + dh)
        k = zf(base + d_ret + c0, base + d_ret + c0 + dh)
        v = z_ref[:, base + 2 * d_ret + c0:base + 2 * d_ret + c0 + dh]
        g = zf(base + 3 * d_ret + c0, base + 3 * d_ret + c0 + dh)
        qr = q * cos + pltpu.roll(q, dh // 2, 1) * sin
        kr = (k * cos + pltpu.roll(k, dh // 2, 1) * sin) * scale
        s = lax.dot_general(qr.astype(BF16), kr.astype(BF16), (((1,), (1,)), ((), ())),
                            preferred_element_type=F32) * dmask_ref[h]
        o = jnp.dot(s.astype(BF16), v, preferred_element_type=F32)
        st = state_scr[h]
        o = o + jnp.dot((qr * qdec_ref[:, c0:c0 + dh]).astype(BF16), st.astype(BF16),
                        preferred_element_type=F32)
        kv = lax.dot_general((kr * kdec_ref[:, c0:c0 + dh]).astype(BF16), v,
                             (((0,), (0,)), ((), ())), preferred_element_type=F32)
        state_scr[h] = st * sdec_ref[:, c0:c0 + dh] + kv
        on = o * lax.rsqrt(jnp.mean(o * o, axis=-1, keepdims=True) + EPS)
        mixed_scr[:, d_conv + c0:d_conv + c0 + dh] = (on * (g * jax.nn.sigmoid(g))).astype(BF16)

    o_ref[...] += jnp.dot(mixed_scr[:, d_conv:], wout_ref[d_conv:, :], preferred_element_type=F32)
    tbuf[slot] = o_ref[...]
    for cp in _tm_write_copies(tbuf.at[slot], ot_hbm, step * ts, tsem.at[slot]):
        cp.start()

    @pl.when(step == last)
    def _():
        for s in range(2):
            for cp in _tm_write_copies(tbuf.at[s], ot_hbm, 0, tsem.at[s]):
                cp.wait()


def _retention_tables(seq, ts, n_heads):
    dh = RET_HEAD_DIM
    half = dh // 2
    pos = jnp.arange(seq, dtype=F32)
    inv = ROPE_BASE ** (-jnp.arange(half, dtype=F32) / half)
    ang = pos[:, None] * inv[None, :]
    cos = jnp.cos(ang)
    sin = jnp.sin(ang)
    cos_full = jnp.concatenate([cos, cos], axis=-1)
    sin_signed = jnp.concatenate([-sin, sin], axis=-1)
    log_g = jnp.log1p(-jnp.exp2(-5.0 - jnp.arange(n_heads, dtype=F32)))
    idx = jnp.arange(ts, dtype=F32)
    dist = jnp.abs(idx[:, None] - idx[None, :])
    chunk_id = jnp.arange(ts) // CHUNK
    visible = chunk_id[None, :] <= chunk_id[:, None]
    dmask = jnp.where(visible[None], jnp.exp(log_g[:, None, None] * dist[None]), 0.0)
    rep = lambda a: jnp.repeat(a, dh, axis=-1)
    qdec = rep(jnp.exp(log_g[None, :] * (idx[:, None] + 1.0)))
    kdec = rep(jnp.exp(log_g[None, :] * (ts - 1.0 - idx[:, None])))
    sdec = rep(jnp.exp(log_g * ts)[None, :])
    return cos_full, sin_signed, dmask.astype(F32), qdec, kdec, sdec


def _mixer(z, x2, wout_bf16, conv_w, bsz, seq, layer):
    n_tok, dm = x2.shape
    d_conv = conv_w.shape[1]
    d_ret = dm - d_conv
    n_heads = d_ret // RET_HEAD_DIM
    d_in = z.shape[1]
    ts = _tile(seq, 256)
    ns = seq // ts
    assert bsz * ns >= 2, "the token-major write pipeline needs at least two grid steps"
    cos_full, sin_signed, dmask, qdec, kdec, sdec = _retention_tables(seq, ts, n_heads)
    kern = functools.partial(_mixer_kernel, ts=ts, d_conv=d_conv, n_heads=n_heads)
    const2 = lambda b, s: (0, 0)
    return pl.pallas_call(
        kern,
        grid=(bsz, ns),
        in_specs=[
            pl.BlockSpec((ts, d_in), lambda b, s: (b * ns + s, 0)),
            pl.BlockSpec((ts, dm), lambda b, s: (b * ns + s, 0)),
            pl.BlockSpec((None, dm, dm), lambda b, s: (layer, 0, 0), pipeline_mode=pl.Buffered(1)),
            pl.BlockSpec((CONV_WIDTH, d_conv), const2),
            pl.BlockSpec((ts, RET_HEAD_DIM), lambda b, s: (s, 0)),
            pl.BlockSpec((ts, RET_HEAD_DIM), lambda b, s: (s, 0)),
            pl.BlockSpec((n_heads, ts, ts), lambda b, s: (0, 0, 0)),
            pl.BlockSpec((ts, d_ret), const2),
            pl.BlockSpec((ts, d_ret), const2),
            pl.BlockSpec((1, d_ret), const2),
        ],
        out_specs=[pl.BlockSpec((ts, dm), lambda b, s: (b * ns + s, 0)),
                   pl.BlockSpec(memory_space=pl.ANY)],
        out_shape=[jax.ShapeDtypeStruct((n_tok, dm), F32),
                   jax.ShapeDtypeStruct((n_tok, dm // LANES, LANES), F32)],
        scratch_shapes=[
            pltpu.VMEM((n_heads, RET_HEAD_DIM, RET_HEAD_DIM), F32),
            pltpu.VMEM((ts + SUBLANES, d_conv), F32),
            pltpu.VMEM((ts, dm), BF16),
            pltpu.VMEM((2, ts, dm), F32),
            pltpu.SemaphoreType.DMA((2,)),
        ],
        compiler_params=pltpu.CompilerParams(
            dimension_semantics=("arbitrary", "arbitrary"), vmem_limit_bytes=VMEM_LIMIT),
        name="mixer",
    )(z, x2, wout_bf16, conv_w, cos_full, sin_signed, dmask, qdec, kdec, sdec)


def _router_kernel(x_ref, g_ref, wr_ref, br_ref, tri_ref, route_ref, cnt_ref, cnt_scr):
    @pl.when(pl.program_id(0) == 0)
    def _():
        cnt_scr[...] = jnp.zeros_like(cnt_scr)

    h = _rms(x_ref[...], g_ref[...]).astype(BF16)
    logits = jnp.dot(h, wr_ref[...], preferred_element_type=F32) + br_ref[...]
    lane = lax.broadcasted_iota(jnp.int32, logits.shape, 1)
    neg = jnp.float32(-jnp.inf)
    big = jnp.int32(LANES)

    gl = jnp.where(lane < N_GROUPS, logits, neg)
    gmax = jnp.max(gl, axis=-1, keepdims=True)
    g_sel = jnp.min(jnp.where(gl == gmax, lane, big), axis=-1, keepdims=True)
    p_g = 1.0 / jnp.sum(jnp.exp(gl - gmax), axis=-1, keepdims=True)

    lo = N_GROUPS + g_sel * EXPERTS_PER_GROUP
    el = jnp.where(lane >= lo, jnp.where(lane < lo + EXPERTS_PER_GROUP, logits, neg), neg)
    v1 = jnp.max(el, axis=-1, keepdims=True)
    i1 = jnp.min(jnp.where(el == v1, lane, big), axis=-1, keepdims=True)
    el2 = jnp.where(lane == i1, neg, el)
    v2 = jnp.max(el2, axis=-1, keepdims=True)
    i2 = jnp.min(jnp.where(el2 == v2, lane, big), axis=-1, keepdims=True)
    t = jnp.exp(v2 - v1)
    gate1 = p_g / (1.0 + t)
    gate2 = p_g * t / (1.0 + t)

    tri = tri_ref[...]
    cnt = cnt_scr[...]
    hot1 = jnp.where(lane == i1, 1.0, 0.0)
    hot2 = jnp.where(lane == i2, 1.0, 0.0)
    before1 = jnp.dot(tri, hot1.astype(BF16), preferred_element_type=F32) + cnt
    cnt = cnt + jnp.sum(hot1, axis=0, keepdims=True)
    before2 = jnp.dot(tri, hot2.astype(BF16), preferred_element_type=F32) + cnt
    cnt = cnt + jnp.sum(hot2, axis=0, keepdims=True)
    rank1 = jnp.sum(hot1 * before1, axis=-1, keepdims=True)
    rank2 = jnp.sum(hot2 * before2, axis=-1, keepdims=True)
    cnt_scr[...] = cnt
    cnt_ref[...] = cnt

    e1 = (i1 - N_GROUPS).astype(F32)
    e2 = (i2 - N_GROUPS).astype(F32)
    cols = (e1, e2, gate1, gate2, rank1, rank2)
    route = jnp.zeros(logits.shape, F32)
    for j, col in enumerate(cols):
        route = jnp.where(lane == j, col, route)
    route_ref[...] = route


def _router(x2, g, w_group, b_group, w_router, b_router):
    n_tok, dm = x2.shape
    n_logit = N_GROUPS + N_EXPERTS
    wr = jnp.zeros((dm, LANES), F32).at[:, :N_GROUPS].set(w_group).at[:, N_GROUPS:n_logit].set(w_router)
    br = jnp.zeros((1, LANES), F32).at[0, :N_GROUPS].set(b_group).at[0, N_GROUPS:n_logit].set(b_router)
    tm = _tile(n_tok, 512)
    row = jnp.arange(tm, dtype=jnp.int32)
    tri = (row[None, :] < row[:, None]).astype(BF16)
    route, cnt = pl.pallas_call(
        _router_kernel,
        grid=(n_tok // tm,),
        in_specs=[
            pl.BlockSpec((tm, dm), lambda i: (i, 0)),
            pl.BlockSpec((1, dm), lambda i: (0, 0)),
            pl.BlockSpec((dm, LANES), lambda i: (0, 0)),
            pl.BlockSpec((1, LANES), lambda i: (0, 0)),
            pl.BlockSpec((tm, tm), lambda i: (0, 0)),
        ],
        out_specs=[pl.BlockSpec((tm, LANES), lambda i: (i, 0)),
                   pl.BlockSpec((1, LANES), lambda i: (0, 0))],
        out_shape=[jax.ShapeDtypeStruct((n_tok, LANES), F32),
                   jax.ShapeDtypeStruct((1, LANES), F32)],
        scratch_shapes=[pltpu.VMEM((1, LANES), F32)],
        compiler_params=pltpu.CompilerParams(
            dimension_semantics=("arbitrary",), vmem_limit_bytes=VMEM_LIMIT),
        name="router",
    )(x2, g.reshape(1, dm), wr.astype(BF16), br, tri)
    return route, cnt[0, N_GROUPS:n_logit].astype(jnp.int32)


def _row_copy(src_tm, idx, dst, r8, s, sem):
    return pltpu.make_async_copy(src_tm.at[idx], dst.at[r8, :, s, :], sem)


def _issue_rows(idx_ref, idx_base, src_tm, dst, sem, n_rows):
    for r in range(n_rows):
        _row_copy(src_tm, idx_ref[idx_base + r], dst, r // SUBLANES, r % SUBLANES, sem).start()


def _wait_rows(dst, sem):
    pltpu.make_async_copy(dst, dst, sem).wait()


CAST_STEPS = 4
MODE_IDLE, MODE_COMPUTE, MODE_ZERO = 0, 1, 2


def _expert_schedule(counts, tmb, n_blocks):
    cs = CAST_STEPS
    n_steps = cs + n_blocks + N_EXPERTS * (cs - 1)
    nblk = (counts + tmb - 1) // tmb
    used = nblk > 0
    steps_e = jnp.where(used, jnp.maximum(nblk, cs), 0)
    step_end = cs + jnp.cumsum(steps_e)
    step_start = step_end - steps_e
    blk_start = jnp.cumsum(nblk) - nblk
    n_used_blk = jnp.sum(nblk)
    total = step_end[-1]
    eidx = jnp.arange(N_EXPERTS, dtype=jnp.int32)
    next_ge = lax.cummin(jnp.where(used, eidx, N_EXPERTS), reverse=True)
    next_gt = jnp.concatenate([next_ge[1:], jnp.full((1,), N_EXPERTS, jnp.int32)])
    seg_ord = jnp.cumsum(used.astype(jnp.int32)) - used.astype(jnp.int32)

    i = jnp.arange(n_steps + 1, dtype=jnp.int32)
    e_i = jnp.minimum(jnp.sum(step_end[None, :] <= i[:, None], axis=1), N_EXPERTS - 1)
    p = i - step_start[e_i]
    warm = i < cs
    in_seg = jnp.logical_and(~warm, i < total)
    compute = jnp.logical_and(in_seg, p < nblk[e_i])
    zidx = i - total
    zero = (i >= total) & (i < n_steps) & (n_used_blk + zidx < n_blocks)
    mode = jnp.where(compute, MODE_COMPUTE, jnp.where(zero, MODE_ZERO, MODE_IDLE))
    blk = jnp.where(compute, blk_start[e_i] + p, jnp.where(zero, n_used_blk + zidx, 0))
    par = jnp.where(in_seg, seg_ord[e_i] % 2, 1)
    target = jnp.where(warm, next_ge[0], next_gt[e_i])
    chunk = jnp.where(warm, i, p)
    cast = (target < N_EXPERTS) & (warm | (in_seg & (p < cs)))
    last_cast = lax.cummax(jnp.where(cast, i, 0))
    cexp = jnp.minimum(target[last_cast], N_EXPERTS - 1)
    cidx = chunk[last_cast]
    as_i32 = lambda a: a.astype(jnp.int32)
    return n_steps, tuple(map(as_i32, (mode, blk, par, cexp, cidx, cast)))


def _experts_kernel(mode_ref, blk_ref, par_ref, cexp_ref, cidx_ref, cast_ref, row_tok_ref,
                    xt_hbm, g_ref, w1f_ref, w3f_ref, w2f_ref, yb_hbm,
                    xbuf, h_scr, obuf, w1a, w3a, w2a, w1b, w3b, w2b, sem, osem, *, tmb):
    del cexp_ref
    i = pl.program_id(0)
    last = pl.num_programs(0) - 1
    slot = i % 2
    nxt = (i + 1) % 2
    n_chunks = xt_hbm.shape[1]
    dm = n_chunks * LANES
    mode = mode_ref[i]
    par = par_ref[i]
    weights = ((w1a, w3a, w2a), (w1b, w3b, w2b))

    def wait_out(s):
        for cp in _tm_write_copies(obuf.at[s], yb_hbm, 0, osem.at[s]):
            cp.wait()

    @pl.when(jnp.logical_and(i >= 2, mode_ref[jnp.maximum(i - 2, 0)] != MODE_IDLE))
    def _():
        wait_out(slot)

    @pl.when(mode_ref[i + 1] == MODE_COMPUTE)
    def _():
        _issue_rows(row_tok_ref, blk_ref[i + 1] * tmb, xt_hbm, xbuf.at[nxt], sem.at[nxt], tmb)

    for v in range(2):
        @pl.when(jnp.logical_and(cast_ref[i] == 1, par == v))
        def _(v=v):
            w1n, w3n, w2n = weights[1 - v]
            r13, r2 = w1f_ref.shape[0], w2f_ref.shape[0]
            c = cidx_ref[i]
            w1n[pl.ds(pl.multiple_of(c * r13, r13), r13), :] = w1f_ref[...].astype(BF16)
            w3n[pl.ds(pl.multiple_of(c * r13, r13), r13), :] = w3f_ref[...].astype(BF16)
            w2n[pl.ds(pl.multiple_of(c * r2, r2), r2), :] = w2f_ref[...].astype(BF16)

    for v in range(2):
        @pl.when(jnp.logical_and(mode == MODE_COMPUTE, par == v))
        def _(v=v):
            w1c, w3c, w2c = weights[v]
            _wait_rows(xbuf.at[slot], sem.at[slot])
            ss = jnp.zeros((tmb, 1), F32)
            for c in range(n_chunks):
                xc = _dense_chunk(xbuf, (slot,), c)
                ss = ss + jnp.sum(xc * xc, axis=-1, keepdims=True)
            rs = lax.rsqrt(ss * (1.0 / dm) + EPS)
            for c in range(n_chunks):
                cs = slice(c * LANES, (c + 1) * LANES)
                h_scr[:, cs] = (_dense_chunk(xbuf, (slot,), c) * rs * g_ref[:, cs]).astype(BF16)
            h = h_scr[...]
            a = jnp.dot(h, w1c[...], preferred_element_type=F32)
            b = jnp.dot(h, w3c[...], preferred_element_type=F32)
            hid = (a * jax.nn.sigmoid(a) * b).astype(BF16)
            obuf[slot] = jnp.dot(hid, w2c[...], preferred_element_type=F32)

    @pl.when(mode == MODE_ZERO)
    def _():
        obuf[slot] = jnp.zeros(obuf.shape[1:], F32)

    @pl.when(mode != MODE_IDLE)
    def _():
        for cp in _tm_write_copies(obuf.at[slot], yb_hbm, blk_ref[i] * tmb, osem.at[slot]):
            cp.start()

    @pl.when(i == last)
    def _():
        @pl.when(mode != MODE_IDLE)
        def _():
            wait_out(slot)

        @pl.when(jnp.logical_and(i >= 1, mode_ref[jnp.maximum(i - 1, 0)] != MODE_IDLE))
        def _():
            wait_out(nxt)


def _experts(xt, g, w1, w3, w2, counts, row_tok, tmb, n_blocks, layer):
    n_tok, n_chunks, _ = xt.shape
    dm = n_chunks * LANES
    de = w1.shape[3]
    assert dm % CAST_STEPS == 0 and de % CAST_STEPS == 0
    n_steps, sched = _expert_schedule(counts, tmb, n_blocks)
    wmap = lambda i, mode, blk, par, cexp, cidx, cast, rt: (layer, cexp[i], cidx[i], 0)
    const = lambda i, *_: (0, 0)
    bf16_slot = [pltpu.VMEM((dm, de), BF16), pltpu.VMEM((dm, de), BF16), pltpu.VMEM((de, dm), BF16)]
    return pl.pallas_call(
        functools.partial(_experts_kernel, tmb=tmb),
        grid_spec=pltpu.PrefetchScalarGridSpec(
            num_scalar_prefetch=7,
            grid=(n_steps,),
            in_specs=[
                pl.BlockSpec(memory_space=pl.ANY),
                pl.BlockSpec((1, dm), const),
                pl.BlockSpec((None, None, dm // CAST_STEPS, de), wmap),
                pl.BlockSpec((None, None, dm // CAST_STEPS, de), wmap),
                pl.BlockSpec((None, None, de // CAST_STEPS, dm), wmap),
            ],
            out_specs=pl.BlockSpec(memory_space=pl.ANY),
            scratch_shapes=[
                pltpu.VMEM((2, tmb // SUBLANES, n_chunks, SUBLANES, LANES), F32),
                pltpu.VMEM((tmb, dm), BF16),
                pltpu.VMEM((2, tmb, dm), F32),
                *bf16_slot, *bf16_slot,
                pltpu.SemaphoreType.DMA((2,)), pltpu.SemaphoreType.DMA((2,))],
        ),
        out_shape=jax.ShapeDtypeStruct((n_blocks * tmb, n_chunks, LANES), F32),
        compiler_params=pltpu.CompilerParams(
            dimension_semantics=("arbitrary",), vmem_limit_bytes=VMEM_LIMIT),
        name="experts",
    )(*sched, row_tok, xt, g.reshape(1, dm), w1, w3, w2)


def _combine_kernel(dest_ref, x_ref, route_ref, gf_ref, yb_hbm, o_ref, buf, sem, *, tm, n_tok, final):
    i = pl.program_id(0)
    n_chunks = yb_hbm.shape[1]
    dm = n_chunks * LANES

    def issue(step, slot):
        for k in range(TOP_K):
            _issue_rows(dest_ref, k * n_tok + step * tm, yb_hbm, buf.at[slot, k], sem.at[slot, k], tm)

    @pl.when(i == 0)
    def _():
        issue(0, 0)

    @pl.when(i + 1 < pl.num_programs(0))
    def _():
        issue(i + 1, (i + 1) % 2)

    slot = i % 2
    for k in range(TOP_K):
        _wait_rows(buf.at[slot, k], sem.at[slot, k])
    gate0 = route_ref[:, 2:3]
    gate1 = route_ref[:, 3:4]
    ss = jnp.zeros((tm, 1), F32)
    for c in range(n_chunks):
        cs = slice(c * LANES, (c + 1) * LANES)
        oc = (x_ref[:, cs] + gate0 * _dense_chunk(buf, (slot, 0), c)
              + gate1 * _dense_chunk(buf, (slot, 1), c))
        o_ref[:, cs] = oc
        if final:
            ss = ss + jnp.sum(oc * oc, axis=-1, keepdims=True)
    if final:
        rs = lax.rsqrt(ss * (1.0 / dm) + EPS)
        for c in range(n_chunks):
            cs = slice(c * LANES, (c + 1) * LANES)
            o_ref[:, cs] = o_ref[:, cs] * rs * gf_ref[:, cs]


def _combine(x2, route, yb, dest, gf, final):
    n_tok, dm = x2.shape
    n_chunks = dm // LANES
    tm = _tile(n_tok, 256)
    return pl.pallas_call(
        functools.partial(_combine_kernel, tm=tm, n_tok=n_tok, final=final),
        grid_spec=pltpu.PrefetchScalarGridSpec(
            num_scalar_prefetch=1,
            grid=(n_tok // tm,),
            in_specs=[
                pl.BlockSpec((tm, dm), lambda i, d: (i, 0)),
                pl.BlockSpec((tm, LANES), lambda i, d: (i, 0)),
                pl.BlockSpec((1, dm), lambda i, d: (0, 0)),
                pl.BlockSpec(memory_space=pl.ANY),
            ],
            out_specs=pl.BlockSpec((tm, dm), lambda i, d: (i, 0)),
            scratch_shapes=[pltpu.VMEM((2, TOP_K, tm // SUBLANES, n_chunks, SUBLANES, LANES), F32),
                            pltpu.SemaphoreType.DMA((2, TOP_K))],
        ),
        out_shape=jax.ShapeDtypeStruct((n_tok, dm), F32),
        compiler_params=pltpu.CompilerParams(
            dimension_semantics=("arbitrary",), vmem_limit_bytes=VMEM_LIMIT),
        name="combine",
    )(dest, x2, route, gf.reshape(1, dm), yb)


SCATTER_UNROLL = 16


def _row_tok_kernel(dest_ref, pad_hbm, row_tok_hbm, rt_smem, sem, *, n_tok):
    fill = pltpu.make_async_copy(pad_hbm, rt_smem, sem)
    fill.start()
    fill.wait()
    for k in range(TOP_K):
        def scatter(j, carry):
            for u in range(SCATTER_UNROLL):
                t = j * SCATTER_UNROLL + u
                rt_smem[dest_ref[k * n_tok + t]] = t
            return carry

        lax.fori_loop(0, n_tok // SCATTER_UNROLL, scatter, 0)
    out = pltpu.make_async_copy(rt_smem, row_tok_hbm, sem)
    out.start()
    out.wait()


def _row_tok(dest_kmajor, n_tok, n_rows):
    assert n_tok % SCATTER_UNROLL == 0
    pad_tok = jnp.arange(n_rows, dtype=jnp.int32) % n_tok
    return pl.pallas_call(
        functools.partial(_row_tok_kernel, n_tok=n_tok),
        in_specs=[pl.BlockSpec(memory_space=pltpu.SMEM), pl.BlockSpec(memory_space=pl.ANY)],
        out_specs=pl.BlockSpec(memory_space=pl.ANY),
        out_shape=jax.ShapeDtypeStruct((n_rows,), jnp.int32),
        scratch_shapes=[pltpu.SMEM((n_rows,), jnp.int32), pltpu.SemaphoreType.DMA],
        name="row_tok",
    )(dest_kmajor, pad_tok)


def _plan(route, counts, tmb, n_blocks):
    n_tok = route.shape[0]
    e = route[:, 0:TOP_K].astype(jnp.int32)
    rank = route[:, 4:4 + TOP_K].astype(jnp.int32)
    pcounts = (counts + tmb - 1) // tmb * tmb
    pends = jnp.cumsum(pcounts)
    pstarts = pends - pcounts
    dest_kmajor = (pstarts[e] + rank).T.reshape(n_tok * TOP_K)
    row_tok = _row_tok(dest_kmajor, n_tok, n_blocks * tmb)
    return dest_kmajor, row_tok


def kernel(x, norm_mix_g, w_in, conv_w, w_out, norm_ffn_g, w_group, b_group, w_router, b_router,
           w_expert_gate, w_expert_up, w_expert_down, final_norm_g):
    bsz, seq, dm = x.shape
    depth = w_in.shape[0]
    n_tok = bsz * seq
    tmb = _tile(n_tok * TOP_K, 256)
    n_blocks = n_tok * TOP_K // tmb + N_EXPERTS
    x2 = x.reshape(n_tok, dm)
    w_in_b, w_out_b = w_in.astype(BF16), w_out.astype(BF16)
    for l in range(depth):
        z = _in_proj(x2, norm_mix_g[l], w_in_b, l)
        x2, xt = _mixer(z, x2, w_out_b, conv_w[l], bsz, seq, l)
        route, counts = _router(x2, norm_ffn_g[l], w_group[l], b_group[l], w_router[l], b_router[l])
        dest, row_tok = _plan(route, counts, tmb, n_blocks)
        yb = _experts(xt, norm_ffn_g[l], w_expert_gate, w_expert_up, w_expert_down, counts, row_tok,
                      tmb, n_blocks, l)
        x2 = _combine(x2, route, yb, dest, final_norm_g, final=(l == depth - 1))
    return x2.reshape(bsz, seq, dm)
```

```python
import functools

import jax
import jax.numpy as jnp
from jax import lax
from jax.experimental import pallas as pl
from jax.experimental.pallas import tpu as pltpu

CHUNK = 64
RET_HEAD_DIM = 128
CONV_WIDTH = 3
ROPE_BASE = 10000.0
N_GROUPS = 4
EXPERTS_PER_GROUP = 8
N_EXPERTS = N_GROUPS * EXPERTS_PER_GROUP
TOP_K = 2
EPS = 1e-6

LANES = 128
SUBLANES = 8
VMEM_LIMIT = 56 * 1024 * 1024

F32 = jnp.float32
BF16 = jnp.bfloat16


def _tile(n, want):
    t = min(n, want)
    while n % t:
        t //= 2
    return t


def _rms(x, g):
    return x * lax.rsqrt(jnp.mean(x * x, axis=-1, keepdims=True) + EPS) * g


def _tm_write_copies(src_dense, dst_tm_hbm, row0, sem):
    rows = src_dense.shape[0]
    return [pltpu.make_async_copy(src_dense.at[:, pl.ds(c * LANES, LANES)],
                                  dst_tm_hbm.at[pl.ds(row0, rows), c, :], sem)
            for c in range(dst_tm_hbm.shape[1])]


def _dense_chunk(buf, lead, c):
    v = buf[(*lead, slice(None), c)]
    return v.reshape(v.shape[0] * SUBLANES, LANES)


def _in_proj_kernel(x_ref, g_ref, w_ref, o_ref, h_scr):
    @pl.when(pl.program_id(1) == 0)
    def _():
        h_scr[...] = _rms(x_ref[...], g_ref[...]).astype(BF16)

    o_ref[...] = jnp.dot(h_scr[...], w_ref[...], preferred_element_type=F32).astype(o_ref.dtype)


def _in_proj(x2, g, w_bf16, layer):
    n_tok, dm = x2.shape
    n_out = w_bf16.shape[2]
    tm = _tile(n_tok, 1024)
    tn = _tile(n_out, 1024)
    return pl.pallas_call(
        _in_proj_kernel,
        grid=(n_tok // tm, n_out // tn),
        in_specs=[
            pl.BlockSpec((tm, dm), lambda i, j: (i, 0)),
            pl.BlockSpec((1, dm), lambda i, j: (0, 0)),
            pl.BlockSpec((None, dm, tn), lambda i, j: (layer, 0, j)),
        ],
        out_specs=pl.BlockSpec((tm, tn), lambda i, j: (i, j)),
        out_shape=jax.ShapeDtypeStruct((n_tok, n_out), BF16),
        scratch_shapes=[pltpu.VMEM((tm, dm), BF16)],
        compiler_params=pltpu.CompilerParams(
            dimension_semantics=("parallel", "arbitrary"), vmem_limit_bytes=VMEM_LIMIT),
        name="in_proj",
    )(x2, g.reshape(1, dm), w_bf16)


def _mixer_kernel(z_ref, x_ref, wout_ref, convw_ref, cos_ref, sin_ref, dmask_ref, qdec_ref,
                  kdec_ref, sdec_ref, o_ref, state_scr, u_scr, mixed_scr, *, ts, d_conv, n_heads):
    dh = RET_HEAD_DIM
    d_ret = n_heads * dh

    @pl.when(pl.program_id(1) == 0)
    def _():
        state_scr[...] = jnp.zeros_like(state_scr)
        u_scr[0:SUBLANES, :] = jnp.zeros((SUBLANES, d_conv), F32)

    zf = lambda lo, hi: z_ref[:, lo:hi].astype(F32)
    u_scr[SUBLANES:SUBLANES + ts, :] = zf(d_conv, 2 * d_conv) * zf(2 * d_conv, 3 * d_conv)
    y = (convw_ref[2:3, :] * u_scr[SUBLANES:SUBLANES + ts, :]
         + convw_ref[1:2, :] * u_scr[SUBLANES - 1:SUBLANES - 1 + ts, :]
         + convw_ref[0:1, :] * u_scr[SUBLANES - 2:SUBLANES - 2 + ts, :])
    mixed_scr[:, 0:d_conv] = (zf(0, d_conv) * y).astype(BF16)
    u_scr[0:SUBLANES, :] = u_scr[ts:ts + SUBLANES, :]
    o_ref[...] = x_ref[...] + jnp.dot(mixed_scr[:, 0:d_conv], wout_ref[0:d_conv, :],
                                      preferred_element_type=F32)

    cos = cos_ref[...]
    sin = sin_ref[...]
    scale = RET_HEAD_DIM ** -0.5
    base = 3 * d_conv
    for h in range(n_heads):
        c0 = h * dh
        q = zf(base + c0, base + c0 + dh)
        k = zf(base + d_ret + c0, base + d_ret + c0 + dh)
        v = z_ref[:, base + 2 * d_ret + c0:base + 2 * d_ret + c0 + dh]
        g = zf(base + 3 * d_ret + c0, base + 3 * d_ret + c0 + dh)
        qr = q * cos + pltpu.roll(q, dh // 2, 1) * sin
        kr = (k * cos + pltpu.roll(k, dh // 2, 1) * sin) * scale
        s = lax.dot_general(qr.astype(BF16), kr.astype(BF16), (((1,), (1,)), ((), ())),
                            preferred_element_type=F32) * dmask_ref[h]
        o = jnp.dot(s.astype(BF16), v, preferred_element_type=F32)
        st = state_scr[h]
        o = o + jnp.dot((qr * qdec_ref[:, c0:c0 + dh]).astype(BF16), st.astype(BF16),
                        preferred_element_type=F32)
        kv = lax.dot_general((kr * kdec_ref[:, c0:c0 + dh]).astype(BF16), v,
                             (((0,), (0,)), ((), ())), preferred_element_type=F32)
        state_scr[h] = st * sdec_ref[:, c0:c0 + dh] + kv
        on = o * lax.rsqrt(jnp.mean(o * o, axis=-1, keepdims=True) + EPS)
        mixed_scr[:, d_conv + c0:d_conv + c0 + dh] = (on * (g * jax.nn.sigmoid(g))).astype(BF16)

    o_ref[...] += jnp.dot(mixed_scr[:, d_conv:], wout_ref[d_conv:, :], preferred_element_type=F32)


def _retention_tables(seq, ts, n_heads):
    dh = RET_HEAD_DIM
    half = dh // 2
    pos = jnp.arange(seq, dtype=F32)
    inv = ROPE_BASE ** (-jnp.arange(half, dtype=F32) / half)
    ang = pos[:, None] * inv[None, :]
    cos = jnp.cos(ang)
    sin = jnp.sin(ang)
    cos_full = jnp.concatenate([cos, cos], axis=-1)
    sin_signed = jnp.concatenate([-sin, sin], axis=-1)
    log_g = jnp.log1p(-jnp.exp2(-5.0 - jnp.arange(n_heads, dtype=F32)))
    idx = jnp.arange(ts, dtype=F32)
    dist = jnp.abs(idx[:, None] - idx[None, :])
    chunk_id = jnp.arange(ts) // CHUNK
    visible = chunk_id[None, :] <= chunk_id[:, None]
    dmask = jnp.where(visible[None], jnp.exp(log_g[:, None, None] * dist[None]), 0.0)
    rep = lambda a: jnp.repeat(a, dh, axis=-1)
    qdec = rep(jnp.exp(log_g[None, :] * (idx[:, None] + 1.0)))
    kdec = rep(jnp.exp(log_g[None, :] * (ts - 1.0 - idx[:, None])))
    sdec = rep(jnp.exp(log_g * ts)[None, :])
    return cos_full, sin_signed, dmask.astype(F32), qdec, kdec, sdec


def _mixer(z, x2, wout_bf16, conv_w, bsz, seq, layer):
    n_tok, dm = x2.shape
    d_conv = conv_w.shape[1]
    d_ret = dm - d_conv
    n_heads = d_ret // RET_HEAD_DIM
    d_in = z.shape[1]
    ts = _tile(seq, 256)
    ns = seq // ts
    cos_full, sin_signed, dmask, qdec, kdec, sdec = _retention_tables(seq, ts, n_heads)
    kern = functools.partial(_mixer_kernel, ts=ts, d_conv=d_conv, n_heads=n_heads)
    const2 = lambda b, s: (0, 0)
    return pl.pallas_call(
        kern,
        grid=(bsz, ns),
        in_specs=[
            pl.BlockSpec((ts, d_in), lambda b, s: (b * ns + s, 0)),
            pl.BlockSpec((ts, dm), lambda b, s: (b * ns + s, 0)),
            pl.BlockSpec((None, dm, dm), lambda b, s: (layer, 0, 0), pipeline_mode=pl.Buffered(1)),
            pl.BlockSpec((CONV_WIDTH, d_conv), const2),
            pl.BlockSpec((ts, RET_HEAD_DIM), lambda b, s: (s, 0)),
            pl.BlockSpec((ts, RET_HEAD_DIM), lambda b, s: (s, 0)),
            pl.BlockSpec((n_heads, ts, ts), lambda b, s: (0, 0, 0)),
            pl.BlockSpec((ts, d_ret), const2),
            pl.BlockSpec((ts, d_ret), const2),
            pl.BlockSpec((1, d_ret), const2),
        ],
        out_specs=pl.BlockSpec((ts, dm), lambda b, s: (b * ns + s, 0)),
        out_shape=jax.ShapeDtypeStruct((n_tok, dm), F32),
        scratch_shapes=[
            pltpu.VMEM((n_heads, RET_HEAD_DIM, RET_HEAD_DIM), F32),
            pltpu.VMEM((ts + SUBLANES, d_conv), F32),
            pltpu.VMEM((ts, dm), BF16),
        ],
        compiler_params=pltpu.CompilerParams(
            dimension_semantics=("parallel", "arbitrary"), vmem_limit_bytes=VMEM_LIMIT),
        name="mixer",
    )(z, x2, wout_bf16, conv_w, cos_full, sin_signed, dmask, qdec, kdec, sdec)


ROUTE_ROWS = SUBLANES


def _router_kernel(x_ref, g_ref, wr_ref, br_ref, tri_ref, route_ref, route_t_ref, cnt_ref, ht_hbm,
                   cnt_scr, tbuf, tsem):
    step = pl.program_id(0)
    last = pl.num_programs(0) - 1
    slot = step % 2
    tm = x_ref.shape[0]

    @pl.when(step >= 2)
    def _():
        for cp in _tm_write_copies(tbuf.at[slot], ht_hbm, 0, tsem.at[slot]):
            cp.wait()

    @pl.when(step == 0)
    def _():
        cnt_scr[...] = jnp.zeros_like(cnt_scr)

    tbuf[slot] = _rms(x_ref[...], g_ref[...])
    for cp in _tm_write_copies(tbuf.at[slot], ht_hbm, step * tm, tsem.at[slot]):
        cp.start()
    h = tbuf[slot].astype(BF16)
    logits = jnp.dot(h, wr_ref[...], preferred_element_type=F32) + br_ref[...]
    lane = lax.broadcasted_iota(jnp.int32, logits.shape, 1)
    neg = jnp.float32(-jnp.inf)
    big = jnp.int32(LANES)

    gl = jnp.where(lane < N_GROUPS, logits, neg)
    gmax = jnp.max(gl, axis=-1, keepdims=True)
    g_sel = jnp.min(jnp.where(gl == gmax, lane, big), axis=-1, keepdims=True)
    p_g = 1.0 / jnp.sum(jnp.exp(gl - gmax), axis=-1, keepdims=True)

    lo = N_GROUPS + g_sel * EXPERTS_PER_GROUP
    el = jnp.where(lane >= lo, jnp.where(lane < lo + EXPERTS_PER_GROUP, logits, neg), neg)
    v1 = jnp.max(el, axis=-1, keepdims=True)
    i1 = jnp.min(jnp.where(el == v1, lane, big), axis=-1, keepdims=True)
    el2 = jnp.where(lane == i1, neg, el)
    v2 = jnp.max(el2, axis=-1, keepdims=True)
    i2 = jnp.min(jnp.where(el2 == v2, lane, big), axis=-1, keepdims=True)
    t = jnp.exp(v2 - v1)
    gate1 = p_g / (1.0 + t)
    gate2 = p_g * t / (1.0 + t)

    tri = tri_ref[...]
    cnt = cnt_scr[...]
    hot1 = jnp.where(lane == i1, 1.0, 0.0)
    hot2 = jnp.where(lane == i2, 1.0, 0.0)
    before1 = jnp.dot(tri, hot1.astype(BF16), preferred_element_type=F32) + cnt
    cnt = cnt + jnp.sum(hot1, axis=0, keepdims=True)
    before2 = jnp.dot(tri, hot2.astype(BF16), preferred_element_type=F32) + cnt
    cnt = cnt + jnp.sum(hot2, axis=0, keepdims=True)
    rank1 = jnp.sum(hot1 * before1, axis=-1, keepdims=True)
    rank2 = jnp.sum(hot2 * before2, axis=-1, keepdims=True)
    cnt_scr[...] = cnt
    cnt_ref[...] = cnt

    e1 = (i1 - N_GROUPS).astype(F32)
    e2 = (i2 - N_GROUPS).astype(F32)
    cols = (e1, e2, gate1, gate2, rank1, rank2)
    route = jnp.zeros(logits.shape, F32)
    for j, col in enumerate(cols):
        route = jnp.where(lane == j, col, route)
    route_ref[...] = route
    route_t_ref[...] = jnp.transpose(route)[0:ROUTE_ROWS, :]

    @pl.when(step == last)
    def _():
        for s in range(2):
            for cp in _tm_write_copies(tbuf.at[s], ht_hbm, 0, tsem.at[s]):
                cp.wait()


def _router(x2, g, w_group, b_group, w_router, b_router):
    n_tok, dm = x2.shape
    n_logit = N_GROUPS + N_EXPERTS
    wr = jnp.zeros((dm, LANES), F32).at[:, :N_GROUPS].set(w_group).at[:, N_GROUPS:n_logit].set(w_router)
    br = jnp.zeros((1, LANES), F32).at[0, :N_GROUPS].set(b_group).at[0, N_GROUPS:n_logit].set(b_router)
    tm = _tile(n_tok, 512)
    assert n_tok // tm >= 2, "the token-major write pipeline needs at least two grid steps"
    row = jnp.arange(tm, dtype=jnp.int32)
    tri = (row[None, :] < row[:, None]).astype(BF16)
    route, route_t, cnt, ht = pl.pallas_call(
        _router_kernel,
        grid=(n_tok // tm,),
        in_specs=[
            pl.BlockSpec((tm, dm), lambda i: (i, 0)),
            pl.BlockSpec((1, dm), lambda i: (0, 0)),
            pl.BlockSpec((dm, LANES), lambda i: (0, 0)),
            pl.BlockSpec((1, LANES), lambda i: (0, 0)),
            pl.BlockSpec((tm, tm), lambda i: (0, 0)),
        ],
        out_specs=[pl.BlockSpec((tm, LANES), lambda i: (i, 0)),
                   pl.BlockSpec((ROUTE_ROWS, tm), lambda i: (0, i)),
                   pl.BlockSpec((1, LANES), lambda i: (0, 0)),
                   pl.BlockSpec(memory_space=pl.ANY)],
        out_shape=[jax.ShapeDtypeStruct((n_tok, LANES), F32),
                   jax.ShapeDtypeStruct((ROUTE_ROWS, n_tok), F32),
                   jax.ShapeDtypeStruct((1, LANES), F32),
                   jax.ShapeDtypeStruct((n_tok, dm // LANES, LANES), F32)],
        scratch_shapes=[pltpu.VMEM((1, LANES), F32),
                        pltpu.VMEM((2, tm, dm), F32),
                        pltpu.SemaphoreType.DMA((2,))],
        compiler_params=pltpu.CompilerParams(
            dimension_semantics=("arbitrary",), vmem_limit_bytes=VMEM_LIMIT),
        name="router",
    )(x2, g.reshape(1, dm), wr.astype(BF16), br, tri)
    return route, route_t, cnt[0, N_GROUPS:n_logit].astype(jnp.int32), ht


def _row_copy(src_tm, idx, dst, r8, s, sem):
    return pltpu.make_async_copy(src_tm.at[idx], dst.at[r8, :, s, :], sem)


def _issue_rows(idx_ref, idx_base, src_tm, dst, sem, n_rows):
    for r in range(n_rows):
        _row_copy(src_tm, idx_ref[idx_base + r], dst, r // SUBLANES, r % SUBLANES, sem).start()


def _wait_rows(dst, sem):
    pltpu.make_async_copy(dst, dst, sem).wait()


CAST_STEPS = 4
MODE_IDLE, MODE_COMPUTE, MODE_ZERO = 0, 1, 2


def _expert_schedule(counts, tmb, n_blocks):
    cs = CAST_STEPS
    n_steps = cs + n_blocks + N_EXPERTS * (cs - 1)
    nblk = (counts + tmb - 1) // tmb
    used = nblk > 0
    steps_e = jnp.where(used, jnp.maximum(nblk, cs), 0)
    step_end = cs + jnp.cumsum(steps_e)
    step_start = step_end - steps_e
    blk_start = jnp.cumsum(nblk) - nblk
    n_used_blk = jnp.sum(nblk)
    total = step_end[-1]
    eidx = jnp.arange(N_EXPERTS, dtype=jnp.int32)
    next_ge = lax.cummin(jnp.where(used, eidx, N_EXPERTS), reverse=True)
    next_gt = jnp.concatenate([next_ge[1:], jnp.full((1,), N_EXPERTS, jnp.int32)])
    seg_ord = jnp.cumsum(used.astype(jnp.int32)) - used.astype(jnp.int32)

    i = jnp.arange(n_steps + 1, dtype=jnp.int32)
    e_i = jnp.minimum(jnp.sum(step_end[None, :] <= i[:, None], axis=1), N_EXPERTS - 1)
    p = i - step_start[e_i]
    warm = i < cs
    in_seg = jnp.logical_and(~warm, i < total)
    compute = jnp.logical_and(in_seg, p < nblk[e_i])
    zidx = i - total
    zero = (i >= total) & (i < n_steps) & (n_used_blk + zidx < n_blocks)
    mode = jnp.where(compute, MODE_COMPUTE, jnp.where(zero, MODE_ZERO, MODE_IDLE))
    blk = jnp.where(compute, blk_start[e_i] + p, jnp.where(zero, n_used_blk + zidx, 0))
    par = jnp.where(in_seg, seg_ord[e_i] % 2, 1)
    target = jnp.where(warm, next_ge[0], next_gt[e_i])
    chunk = jnp.where(warm, i, p)
    cast = (target < N_EXPERTS) & (warm | (in_seg & (p < cs)))
    last_cast = lax.cummax(jnp.where(cast, i, 0))
    cexp = jnp.minimum(target[last_cast], N_EXPERTS - 1)
    cidx = chunk[last_cast]
    as_i32 = lambda a: a.astype(jnp.int32)
    return n_steps, tuple(map(as_i32, (mode, blk, par, cexp, cidx, cast)))


def _experts_kernel(mode_ref, blk_ref, par_ref, cexp_ref, cidx_ref, cast_ref, row_tok_ref,
                    xt_hbm, w1f_ref, w3f_ref, w2f_ref, yb_hbm,
                    xbuf, h_scr, obuf, w1a, w3a, w2a, w1b, w3b, w2b, sem, osem, *, tmb):
    del cexp_ref
    i = pl.program_id(0)
    last = pl.num_programs(0) - 1
    slot = i % 2
    nxt = (i + 1) % 2
    n_chunks = xt_hbm.shape[1]
    mode = mode_ref[i]
    par = par_ref[i]
    weights = ((w1a, w3a, w2a), (w1b, w3b, w2b))

    def wait_out(s):
        for cp in _tm_write_copies(obuf.at[s], yb_hbm, 0, osem.at[s]):
            cp.wait()

    @pl.when(jnp.logical_and(i >= 2, mode_ref[jnp.maximum(i - 2, 0)] != MODE_IDLE))
    def _():
        wait_out(slot)

    @pl.when(mode_ref[i + 1] == MODE_COMPUTE)
    def _():
        _issue_rows(row_tok_ref, blk_ref[i + 1] * tmb, xt_hbm, xbuf.at[nxt], sem.at[nxt], tmb)

    for v in range(2):
        @pl.when(jnp.logical_and(cast_ref[i] == 1, par == v))
        def _(v=v):
            w1n, w3n, w2n = weights[1 - v]
            r13, r2 = w1f_ref.shape[0], w2f_ref.shape[0]
            c = cidx_ref[i]
            w1n[pl.ds(pl.multiple_of(c * r13, r13), r13), :] = w1f_ref[...].astype(BF16)
            w3n[pl.ds(pl.multiple_of(c * r13, r13), r13), :] = w3f_ref[...].astype(BF16)
            w2n[pl.ds(pl.multiple_of(c * r2, r2), r2), :] = w2f_ref[...].astype(BF16)

    for v in range(2):
        @pl.when(jnp.logical_and(mode == MODE_COMPUTE, par == v))
        def _(v=v):
            w1c, w3c, w2c = weights[v]
            _wait_rows(xbuf.at[slot], sem.at[slot])
            for c in range(n_chunks):
                h_scr[:, c * LANES:(c + 1) * LANES] = _dense_chunk(xbuf, (slot,), c).astype(BF16)
            h = h_scr[...]
            a = jnp.dot(h, w1c[...], preferred_element_type=F32)
            b = jnp.dot(h, w3c[...], preferred_element_type=F32)
            hid = (a * jax.nn.sigmoid(a) * b).astype(BF16)
            obuf[slot] = jnp.dot(hid, w2c[...], preferred_element_type=F32)

    @pl.when(mode == MODE_ZERO)
    def _():
        obuf[slot] = jnp.zeros(obuf.shape[1:], F32)

    @pl.when(mode != MODE_IDLE)
    def _():
        for cp in _tm_write_copies(obuf.at[slot], yb_hbm, blk_ref[i] * tmb, osem.at[slot]):
            cp.start()

    @pl.when(i == last)
    def _():
        @pl.when(mode != MODE_IDLE)
        def _():
            wait_out(slot)

        @pl.when(jnp.logical_and(i >= 1, mode_ref[jnp.maximum(i - 1, 0)] != MODE_IDLE))
        def _():
            wait_out(nxt)


def _experts(xt, w1, w3, w2, counts, row_tok, tmb, n_blocks, layer):
    n_tok, n_chunks, _ = xt.shape
    dm = n_chunks * LANES
    de = w1.shape[3]
    assert dm % CAST_STEPS == 0 and de % CAST_STEPS == 0
    n_steps, sched = _expert_schedule(counts, tmb, n_blocks)
    wmap = lambda i, mode, blk, par, cexp, cidx, cast, rt: (layer, cexp[i], cidx[i], 0)
    bf16_slot = [pltpu.VMEM((dm, de), BF16), pltpu.VMEM((dm, de), BF16), pltpu.VMEM((de, dm), BF16)]
    return pl.pallas_call(
        functools.partial(_experts_kernel, tmb=tmb),
        grid_spec=pltpu.PrefetchScalarGridSpec(
            num_scalar_prefetch=7,
            grid=(n_steps,),
            in_specs=[
                pl.BlockSpec(memory_space=pl.ANY),
                pl.BlockSpec((None, None, dm // CAST_STEPS, de), wmap),
                pl.BlockSpec((None, None, dm // CAST_STEPS, de), wmap),
                pl.BlockSpec((None, None, de // CAST_STEPS, dm), wmap),
            ],
            out_specs=pl.BlockSpec(memory_space=pl.ANY),
            scratch_shapes=[
                pltpu.VMEM((2, tmb // SUBLANES, n_chunks, SUBLANES, LANES), F32),
                pltpu.VMEM((tmb, dm), BF16),
                pltpu.VMEM((2, tmb, dm), F32),
                *bf16_slot, *bf16_slot,
                pltpu.SemaphoreType.DMA((2,)), pltpu.SemaphoreType.DMA((2,))],
        ),
        out_shape=jax.ShapeDtypeStruct((n_blocks * tmb, n_chunks, LANES), F32),
        compiler_params=pltpu.CompilerParams(
            dimension_semantics=("arbitrary",), vmem_limit_bytes=VMEM_LIMIT),
        name="experts",
    )(*sched, row_tok, xt, w1, w3, w2)


def _combine_kernel(dest_ref, x_ref, route_ref, gf_ref, yb_hbm, o_ref, buf, sem, *, tm, n_tok, final):
    i = pl.program_id(0)
    n_chunks = yb_hbm.shape[1]
    dm = n_chunks * LANES

    def issue(step, slot):
        for k in range(TOP_K):
            _issue_rows(dest_ref, k * n_tok + step * tm, yb_hbm, buf.at[slot, k], sem.at[slot, k], tm)

    @pl.when(i == 0)
    def _():
        issue(0, 0)

    @pl.when(i + 1 < pl.num_programs(0))
    def _():
        issue(i + 1, (i + 1) % 2)

    slot = i % 2
    for k in range(TOP_K):
        _wait_rows(buf.at[slot, k], sem.at[slot, k])
    gate0 = route_ref[:, 2:3]
    gate1 = route_ref[:, 3:4]
    ss = jnp.zeros((tm, 1), F32)
    for c in range(n_chunks):
        cs = slice(c * LANES, (c + 1) * LANES)
        oc = (x_ref[:, cs] + gate0 * _dense_chunk(buf, (slot, 0), c)
              + gate1 * _dense_chunk(buf, (slot, 1), c))
        o_ref[:, cs] = oc
        if final:
            ss = ss + jnp.sum(oc * oc, axis=-1, keepdims=True)
    if final:
        rs = lax.rsqrt(ss * (1.0 / dm) + EPS)
        for c in range(n_chunks):
            cs = slice(c * LANES, (c + 1) * LANES)
            o_ref[:, cs] = o_ref[:, cs] * rs * gf_ref[:, cs]


def _combine(x2, route, yb, dest, gf, final):
    n_tok, dm = x2.shape
    n_chunks = dm // LANES
    tm = _tile(n_tok, 256)
    return pl.pallas_call(
        functools.partial(_combine_kernel, tm=tm, n_tok=n_tok, final=final),
        grid_spec=pltpu.PrefetchScalarGridSpec(
            num_scalar_prefetch=1,
            grid=(n_tok // tm,),
            in_specs=[
                pl.BlockSpec((tm, dm), lambda i, d: (i, 0)),
                pl.BlockSpec((tm, LANES), lambda i, d: (i, 0)),
                pl.BlockSpec((1, dm), lambda i, d: (0, 0)),
                pl.BlockSpec(memory_space=pl.ANY),
            ],
            out_specs=pl.BlockSpec((tm, dm), lambda i, d: (i, 0)),
            scratch_shapes=[pltpu.VMEM((2, TOP_K, tm // SUBLANES, n_chunks, SUBLANES, LANES), F32),
                            pltpu.SemaphoreType.DMA((2, TOP_K))],
        ),
        out_shape=jax.ShapeDtypeStruct((n_tok, dm), F32),
        compiler_params=pltpu.CompilerParams(
            dimension_semantics=("arbitrary",), vmem_limit_bytes=VMEM_LIMIT),
        name="combine",
    )(dest, x2, route, gf.reshape(1, dm), yb)


SCATTER_UNROLL = 16


def _row_tok_kernel(dest_ref, pad_hbm, row_tok_hbm, rt_smem, sem, *, n_tok):
    fill = pltpu.make_async_copy(pad_hbm, rt_smem, sem)
    fill.start()
    fill.wait()
    for k in range(TOP_K):
        def scatter(j, carry):
            for u in range(SCATTER_UNROLL):
                t = j * SCATTER_UNROLL + u
                rt_smem[dest_ref[k * n_tok + t]] = t
            return carry

        lax.fori_loop(0, n_tok // SCATTER_UNROLL, scatter, 0)
    out = pltpu.make_async_copy(rt_smem, row_tok_hbm, sem)
    out.start()
    out.wait()


def _row_tok(dest_kmajor, n_tok, n_rows):
    assert n_tok % SCATTER_UNROLL == 0
    pad_tok = jnp.arange(n_rows, dtype=jnp.int32) % n_tok
    return pl.pallas_call(
        functools.partial(_row_tok_kernel, n_tok=n_tok),
        in_specs=[pl.BlockSpec(memory_space=pltpu.SMEM), pl.BlockSpec(memory_space=pl.ANY)],
        out_specs=pl.BlockSpec(memory_space=pl.ANY),
        out_shape=jax.ShapeDtypeStruct((n_rows,), jnp.int32),
        scratch_shapes=[pltpu.SMEM((n_rows,), jnp.int32), pltpu.SemaphoreType.DMA],
        name="row_tok",
    )(dest_kmajor, pad_tok)


def _plan(route_t, counts, tmb, n_blocks):
    n_tok = route_t.shape[1]
    e = route_t[0:TOP_K].astype(jnp.int32)
    rank = route_t[4:4 + TOP_K].astype(jnp.int32)
    pcounts = (counts + tmb - 1) // tmb * tmb
    pends = jnp.cumsum(pcounts)
    pstarts = pends - pcounts
    dest_kmajor = (pstarts[e] + rank).reshape(n_tok * TOP_K)
    row_tok = _row_tok(dest_kmajor, n_tok, n_blocks * tmb)
    return dest_kmajor, row_tok


def kernel(x, norm_mix_g, w_in, conv_w, w_out, norm_ffn_g, w_group, b_group, w_router, b_router,
           w_expert_gate, w_expert_up, w_expert_down, final_norm_g):
    bsz, seq, dm = x.shape
    depth = w_in.shape[0]
    n_tok = bsz * seq
    tmb = _tile(n_tok * TOP_K, 256)
    n_blocks = n_tok * TOP_K // tmb + N_EXPERTS
    x2 = x.reshape(n_tok, dm)
    w_in_b, w_out_b = w_in.astype(BF16), w_out.astype(BF16)
    for l in range(depth):
        z = _in_proj(x2, norm_mix_g[l], w_in_b, l)
        x2 = _mixer(z, x2, w_out_b, conv_w[l], bsz, seq, l)
        route, route_t, counts, ht = _router(x2, norm_ffn_g[l], w_group[l], b_group[l], w_router[l],
                                             b_router[l])
        dest, row_tok = _plan(route_t, counts, tmb, n_blocks)
        yb = _experts(ht, w_expert_gate, w_expert_up, w_expert_down, counts, row_tok, tmb, n_blocks, l)
        x2 = _combine(x2, route, yb, dest, final_norm_g, final=(l == depth - 1))
    return x2.reshape(bsz, seq, dm)
```

```python
import functools

import jax
import jax.numpy as jnp
from jax import lax
from jax.experimental import pallas as pl
from jax.experimental.pallas import tpu as pltpu

CHUNK = 64
RET_HEAD_DIM = 128
CONV_WIDTH = 3
ROPE_BASE = 10000.0
N_GROUPS = 4
EXPERTS_PER_GROUP = 8
N_EXPERTS = N_GROUPS * EXPERTS_PER_GROUP
TOP_K = 2
EPS = 1e-6

LANES = 128
SUBLANES = 8
VMEM_LIMIT = 56 * 1024 * 1024

F32 = jnp.float32
BF16 = jnp.bfloat16


def _tile(n, want):
    t = min(n, want)
    while n % t:
        t //= 2
    return t


def _rms(x, g):
    return x * lax.rsqrt(jnp.mean(x * x, axis=-1, keepdims=True) + EPS) * g


def _tm_write_copies(src_dense, dst_tm_hbm, row0, sem):
    rows = src_dense.shape[0]
    return [pltpu.make_async_copy(src_dense.at[:, pl.ds(c * LANES, LANES)],
                                  dst_tm_hbm.at[pl.ds(row0, rows), c, :], sem)
            for c in range(dst_tm_hbm.shape[1])]


def _dense_chunk(buf, lead, c):
    v = buf[(*lead, slice(None), c)]
    return v.reshape(v.shape[0] * SUBLANES, LANES)


def _in_proj_kernel(x_ref, g_ref, w_ref, o_ref, h_scr):
    @pl.when(pl.program_id(1) == 0)
    def _():
        h_scr[...] = _rms(x_ref[...], g_ref[...]).astype(BF16)

    o_ref[...] = jnp.dot(h_scr[...], w_ref[...], preferred_element_type=F32).astype(o_ref.dtype)


def _in_proj(x2, g, w_bf16, layer):
    n_tok, dm = x2.shape
    n_out = w_bf16.shape[2]
    tm = _tile(n_tok, 1024)
    tn = _tile(n_out, 1024)
    return pl.pallas_call(
        _in_proj_kernel,
        grid=(n_tok // tm, n_out // tn),
        in_specs=[
            pl.BlockSpec((tm, dm), lambda i, j: (i, 0)),
            pl.BlockSpec((1, dm), lambda i, j: (0, 0)),
            pl.BlockSpec((None, dm, tn), lambda i, j: (layer, 0, j)),
        ],
        out_specs=pl.BlockSpec((tm, tn), lambda i, j: (i, j)),
        out_shape=jax.ShapeDtypeStruct((n_tok, n_out), BF16),
        scratch_shapes=[pltpu.VMEM((tm, dm), BF16)],
        compiler_params=pltpu.CompilerParams(
            dimension_semantics=("parallel", "arbitrary"), vmem_limit_bytes=VMEM_LIMIT),
        name="in_proj",
    )(x2, g.reshape(1, dm), w_bf16)


def _mixer_kernel(z_ref, x_ref, wout_ref, convw_ref, cos_ref, sin_ref, dmask_ref, qdec_ref,
                  kdec_ref, sdec_ref, o_ref, state_scr, u_scr, mixed_scr, *, ts, d_conv, n_heads):
    dh = RET_HEAD_DIM
    d_ret = n_heads * dh

    @pl.when(pl.program_id(1) == 0)
    def _():
        state_scr[...] = jnp.zeros_like(state_scr)
        u_scr[0:SUBLANES, :] = jnp.zeros((SUBLANES, d_conv), F32)

    zf = lambda lo, hi: z_ref[:, lo:hi].astype(F32)
    u_scr[SUBLANES:SUBLANES + ts, :] = zf(d_conv, 2 * d_conv) * zf(2 * d_conv, 3 * d_conv)
    y = (convw_ref[2:3, :] * u_scr[SUBLANES:SUBLANES + ts, :]
         + convw_ref[1:2, :] * u_scr[SUBLANES - 1:SUBLANES - 1 + ts, :]
         + convw_ref[0:1, :] * u_scr[SUBLANES - 2:SUBLANES - 2 + ts, :])
    mixed_scr[:, 0:d_conv] = (zf(0, d_conv) * y).astype(BF16)
    u_scr[0:SUBLANES, :] = u_scr[ts:ts + SUBLANES, :]
    o_ref[...] = x_ref[...] + jnp.dot(mixed_scr[:, 0:d_conv], wout_ref[0:d_conv, :],
                                      preferred_element_type=F32)

    cos = cos_ref[...]
    sin = sin_ref[...]
    scale = RET_HEAD_DIM ** -0.5
    base = 3 * d_conv
    for h in range(n_heads):
        c0 = h * dh
        q = zf(base + c0, base + c0 + dh)
        k = zf(base + d_ret + c0, base + d_ret + c0 + dh)
        v = z_ref[:, base + 2 * d_ret + c0:base + 2 * d_ret + c0 + dh]
        g = zf(base + 3 * d_ret + c0, base + 3 * d_ret + c0 + dh)
        qr = q * cos + pltpu.roll(q, dh // 2, 1) * sin
        kr = (k * cos + pltpu.roll(k, dh // 2, 1) * sin) * scale
        s = lax.dot_general(qr.astype(BF16), kr.astype(BF16), (((1,), (1,)), ((), ())),
                            preferred_element_type=F32) * dmask_ref[h]
        o = jnp.dot(s.astype(BF16), v, preferred_element_type=F32)
        st = state_scr[h]
        o = o + jnp.dot((qr * qdec_ref[:, c0:c0 + dh]).astype(BF16), st.astype(BF16),
                        preferred_element_type=F32)
        kv = lax.dot_general((kr * kdec_ref[:, c0:c0 + dh]).astype(BF16), v,
                             (((0,), (0,)), ((), ())), preferred_element_type=F32)
        state_scr[h] = st * sdec_ref[:, c0:c0 + dh] + kv
        on = o * lax.rsqrt(jnp.mean(o * o, axis=-1, keepdims=True) + EPS)
        mixed_scr[:, d_conv + c0:d_conv + c0 + dh] = (on * (g * jax.nn.sigmoid(g))).astype(BF16)

    o_ref[...] += jnp.dot(mixed_scr[:, d_conv:], wout_ref[d_conv:, :], preferred_element_type=F32)


def _retention_tables(seq, ts, n_heads):
    dh = RET_HEAD_DIM
    half = dh // 2
    pos = jnp.arange(seq, dtype=F32)
    inv = ROPE_BASE ** (-jnp.arange(half, dtype=F32) / half)
    ang = pos[:, None] * inv[None, :]
    cos = jnp.cos(ang)
    sin = jnp.sin(ang)
    cos_full = jnp.concatenate([cos, cos], axis=-1)
    sin_signed = jnp.concatenate([-sin, sin], axis=-1)
    log_g = jnp.log1p(-jnp.exp2(-5.0 - jnp.arange(n_heads, dtype=F32)))
    idx = jnp.arange(ts, dtype=F32)
    dist = jnp.abs(idx[:, None] - idx[None, :])
    chunk_id = jnp.arange(ts) // CHUNK
    visible = chunk_id[None, :] <= chunk_id[:, None]
    dmask = jnp.where(visible[None], jnp.exp(log_g[:, None, None] * dist[None]), 0.0)
    rep = lambda a: jnp.repeat(a, dh, axis=-1)
    qdec = rep(jnp.exp(log_g[None, :] * (idx[:, None] + 1.0)))
    kdec = rep(jnp.exp(log_g[None, :] * (ts - 1.0 - idx[:, None])))
    sdec = rep(jnp.exp(log_g * ts)[None, :])
    return cos_full, sin_signed, dmask.astype(F32), qdec, kdec, sdec


def _mixer(z, x2, wout_bf16, conv_w, bsz, seq, layer):
    n_tok, dm = x2.shape
    d_conv = conv_w.shape[1]
    d_ret = dm - d_conv
    n_heads = d_ret // RET_HEAD_DIM
    d_in = z.shape[1]
    ts = _tile(seq, 256)
    ns = seq // ts
    cos_full, sin_signed, dmask, qdec, kdec, sdec = _retention_tables(seq, ts, n_heads)
    kern = functools.partial(_mixer_kernel, ts=ts, d_conv=d_conv, n_heads=n_heads)
    const2 = lambda b, s: (0, 0)
    return pl.pallas_call(
        kern,
        grid=(bsz, ns),
        in_specs=[
            pl.BlockSpec((ts, d_in), lambda b, s: (b * ns + s, 0)),
            pl.BlockSpec((ts, dm), lambda b, s: (b * ns + s, 0)),
            pl.BlockSpec((None, dm, dm), lambda b, s: (layer, 0, 0), pipeline_mode=pl.Buffered(1)),
            pl.BlockSpec((CONV_WIDTH, d_conv), const2),
            pl.BlockSpec((ts, RET_HEAD_DIM), lambda b, s: (s, 0)),
            pl.BlockSpec((ts, RET_HEAD_DIM), lambda b, s: (s, 0)),
            pl.BlockSpec((n_heads, ts, ts), lambda b, s: (0, 0, 0)),
            pl.BlockSpec((ts, d_ret), const2),
            pl.BlockSpec((ts, d_ret), const2),
            pl.BlockSpec((1, d_ret), const2),
        ],
        out_specs=pl.BlockSpec((ts, dm), lambda b, s: (b * ns + s, 0)),
        out_shape=jax.ShapeDtypeStruct((n_tok, dm), F32),
        scratch_shapes=[
            pltpu.VMEM((n_heads, RET_HEAD_DIM, RET_HEAD_DIM), F32),
            pltpu.VMEM((ts + SUBLANES, d_conv), F32),
            pltpu.VMEM((ts, dm), BF16),
        ],
        compiler_params=pltpu.CompilerParams(
            dimension_semantics=("parallel", "arbitrary"), vmem_limit_bytes=VMEM_LIMIT),
        name="mixer",
    )(z, x2, wout_bf16, conv_w, cos_full, sin_signed, dmask, qdec, kdec, sdec)


ROUTE_ROWS = SUBLANES


def _router_kernel(x_ref, g_ref, wr_ref, br_ref, tri_ref, route_ref, route_t_ref, cnt_ref, ht_hbm,
                   cnt_scr, tbuf, tsem):
    step = pl.program_id(0)
    last = pl.num_programs(0) - 1
    slot = step % 2
    tm = x_ref.shape[0]

    @pl.when(step >= 2)
    def _():
        for cp in _tm_write_copies(tbuf.at[slot], ht_hbm, 0, tsem.at[slot]):
            cp.wait()

    @pl.when(step == 0)
    def _():
        cnt_scr[...] = jnp.zeros_like(cnt_scr)

    tbuf[slot] = _rms(x_ref[...], g_ref[...])
    for cp in _tm_write_copies(tbuf.at[slot], ht_hbm, step * tm, tsem.at[slot]):
        cp.start()
    h = tbuf[slot].astype(BF16)
    logits = jnp.dot(h, wr_ref[...], preferred_element_type=F32) + br_ref[...]
    lane = lax.broadcasted_iota(jnp.int32, logits.shape, 1)
    neg = jnp.float32(-jnp.inf)
    big = jnp.int32(LANES)

    gl = jnp.where(lane < N_GROUPS, logits, neg)
    gmax = jnp.max(gl, axis=-1, keepdims=True)
    g_sel = jnp.min(jnp.where(gl == gmax, lane, big), axis=-1, keepdims=True)
    p_g = 1.0 / jnp.sum(jnp.exp(gl - gmax), axis=-1, keepdims=True)

    lo = N_GROUPS + g_sel * EXPERTS_PER_GROUP
    el = jnp.where(lane >= lo, jnp.where(lane < lo + EXPERTS_PER_GROUP, logits, neg), neg)
    v1 = jnp.max(el, axis=-1, keepdims=True)
    i1 = jnp.min(jnp.where(el == v1, lane, big), axis=-1, keepdims=True)
    el2 = jnp.where(lane == i1, neg, el)
    v2 = jnp.max(el2, axis=-1, keepdims=True)
    i2 = jnp.min(jnp.where(el2 == v2, lane, big), axis=-1, keepdims=True)
    t = jnp.exp(v2 - v1)
    gate1 = p_g / (1.0 + t)
    gate2 = p_g * t / (1.0 + t)

    tri = tri_ref[...]
    cnt = cnt_scr[...]
    hot1 = jnp.where(lane == i1, 1.0, 0.0)
    hot2 = jnp.where(lane == i2, 1.0, 0.0)
    before1 = jnp.dot(tri, hot1.astype(BF16), preferred_element_type=F32) + cnt
    cnt = cnt + jnp.sum(hot1, axis=0, keepdims=True)
    before2 = jnp.dot(tri, hot2.astype(BF16), preferred_element_type=F32) + cnt
    cnt = cnt + jnp.sum(hot2, axis=0, keepdims=True)
    rank1 = jnp.sum(hot1 * before1, axis=-1, keepdims=True)
    rank2 = jnp.sum(hot2 * before2, axis=-1, keepdims=True)
    cnt_scr[...] = cnt
    cnt_ref[...] = cnt

    e1 = (i1 - N_GROUPS).astype(F32)
    e2 = (i2 - N_GROUPS).astype(F32)
    cols = (e1, e2, gate1, gate2, rank1, rank2)
    route = jnp.zeros(logits.shape, F32)
    for j, col in enumerate(cols):
        route = jnp.where(lane == j, col, route)
    route_ref[...] = route
    route_t_ref[...] = jnp.transpose(route)[0:ROUTE_ROWS, :]

    @pl.when(step == last)
    def _():
        for s in range(2):
            for cp in _tm_write_copies(tbuf.at[s], ht_hbm, 0, tsem.at[s]):
                cp.wait()


def _router(x2, g, w_group, b_group, w_router, b_router):
    n_tok, dm = x2.shape
    n_logit = N_GROUPS + N_EXPERTS
    wr = jnp.zeros((dm, LANES), F32).at[:, :N_GROUPS].set(w_group).at[:, N_GROUPS:n_logit].set(w_router)
    br = jnp.zeros((1, LANES), F32).at[0, :N_GROUPS].set(b_group).at[0, N_GROUPS:n_logit].set(b_router)
    tm = _tile(n_tok, 512)
    assert n_tok // tm >= 2, "the token-major write pipeline needs at least two grid steps"
    row = jnp.arange(tm, dtype=jnp.int32)
    tri = (row[None, :] < row[:, None]).astype(BF16)
    route, route_t, cnt, ht = pl.pallas_call(
        _router_kernel,
        grid=(n_tok // tm,),
        in_specs=[
            pl.BlockSpec((tm, dm), lambda i: (i, 0)),
            pl.BlockSpec((1, dm), lambda i: (0, 0)),
            pl.BlockSpec((dm, LANES), lambda i: (0, 0)),
            pl.BlockSpec((1, LANES), lambda i: (0, 0)),
            pl.BlockSpec((tm, tm), lambda i: (0, 0)),
        ],
        out_specs=[pl.BlockSpec((tm, LANES), lambda i: (i, 0)),
                   pl.BlockSpec((ROUTE_ROWS, tm), lambda i: (0, i)),
                   pl.BlockSpec((1, LANES), lambda i: (0, 0)),
                   pl.BlockSpec(memory_space=pl.ANY)],
        out_shape=[jax.ShapeDtypeStruct((n_tok, LANES), F32),
                   jax.ShapeDtypeStruct((ROUTE_ROWS, n_tok), F32),
                   jax.ShapeDtypeStruct((1, LANES), F32),
                   jax.ShapeDtypeStruct((n_tok, dm // LANES, LANES), F32)],
        scratch_shapes=[pltpu.VMEM((1, LANES), F32),
                        pltpu.VMEM((2, tm, dm), F32),
                        pltpu.SemaphoreType.DMA((2,))],
        compiler_params=pltpu.CompilerParams(
            dimension_semantics=("arbitrary",), vmem_limit_bytes=VMEM_LIMIT),
        name="router",
    )(x2, g.reshape(1, dm), wr.astype(BF16), br, tri)
    return route, route_t, cnt[0, N_GROUPS:n_logit].astype(jnp.int32), ht


def _row_copy(src_tm, idx, dst, r8, s, sem):
    return pltpu.make_async_copy(src_tm.at[idx], dst.at[r8, :, s, :], sem)


def _issue_rows(idx_ref, idx_base, src_tm, dst, sem, n_rows):
    for r in range(n_rows):
        _row_copy(src_tm, idx_ref[idx_base + r], dst, r // SUBLANES, r % SUBLANES, sem).start()


def _wait_rows(dst, sem):
    pltpu.make_async_copy(dst, dst, sem).wait()


CAST_STEPS = 4
MODE_IDLE, MODE_COMPUTE, MODE_ZERO = 0, 1, 2


def _lookup(table, idx):
    pos = jnp.arange(table.shape[0], dtype=jnp.int32)
    return jnp.sum(jnp.where(idx[..., None] == pos, table, 0), axis=-1)


def _expert_schedule(counts, tmb, n_blocks):
    cs = CAST_STEPS
    n_steps = cs + n_blocks + N_EXPERTS * (cs - 1)
    nblk = (counts + tmb - 1) // tmb
    used = nblk > 0
    steps_e = jnp.where(used, jnp.maximum(nblk, cs), 0)
    step_end = cs + jnp.cumsum(steps_e)
    step_start = step_end - steps_e
    blk_start = jnp.cumsum(nblk) - nblk
    n_used_blk = jnp.sum(nblk)
    total = step_end[-1]
    eidx = jnp.arange(N_EXPERTS, dtype=jnp.int32)
    next_ge = lax.cummin(jnp.where(used, eidx, N_EXPERTS), reverse=True)
    next_gt = jnp.concatenate([next_ge[1:], jnp.full((1,), N_EXPERTS, jnp.int32)])
    seg_ord = jnp.cumsum(used.astype(jnp.int32)) - used.astype(jnp.int32)

    last_used = jnp.max(jnp.where(used, eidx, 0))

    i = jnp.arange(n_steps + 1, dtype=jnp.int32)
    e_i = jnp.minimum(jnp.sum(step_end[None, :] <= i[:, None], axis=1), N_EXPERTS - 1)
    at_e = lambda table: _lookup(table, e_i)
    p = i - at_e(step_start)
    warm = i < cs
    in_seg = jnp.logical_and(~warm, i < total)
    compute = jnp.logical_and(in_seg, p < at_e(nblk))
    zidx = i - total
    zero = (i >= total) & (i < n_steps) & (n_used_blk + zidx < n_blocks)
    mode = jnp.where(compute, MODE_COMPUTE, jnp.where(zero, MODE_ZERO, MODE_IDLE))
    blk = jnp.where(compute, at_e(blk_start) + p, jnp.where(zero, n_used_blk + zidx, 0))
    par = jnp.where(in_seg, at_e(seg_ord) % 2, 1)
    nxt = at_e(next_gt)
    has_next = jnp.logical_and(in_seg, nxt < N_EXPERTS)
    cast = warm | (has_next & (p < cs))
    cexp = jnp.where(warm, next_ge[0], jnp.where(has_next, nxt, last_used))
    cidx = jnp.where(warm, i, jnp.where(has_next, jnp.minimum(p, cs - 1), cs - 1))
    as_i32 = lambda a: a.astype(jnp.int32)
    return n_steps, tuple(map(as_i32, (mode, blk, par, cexp, cidx, cast)))


def _experts_kernel(mode_ref, blk_ref, par_ref, cexp_ref, cidx_ref, cast_ref, row_tok_ref,
                    xt_hbm, w1f_ref, w3f_ref, w2f_ref, yb_hbm,
                    xbuf, h_scr, obuf, w1a, w3a, w2a, w1b, w3b, w2b, sem, osem, *, tmb):
    del cexp_ref
    i = pl.program_id(0)
    last = pl.num_programs(0) - 1
    slot = i % 2
    nxt = (i + 1) % 2
    n_chunks = xt_hbm.shape[1]
    mode = mode_ref[i]
    par = par_ref[i]
    weights = ((w1a, w3a, w2a), (w1b, w3b, w2b))

    def wait_out(s):
        for cp in _tm_write_copies(obuf.at[s], yb_hbm, 0, osem.at[s]):
            cp.wait()

    @pl.when(jnp.logical_and(i >= 2, mode_ref[jnp.maximum(i - 2, 0)] != MODE_IDLE))
    def _():
        wait_out(slot)

    @pl.when(mode_ref[i + 1] == MODE_COMPUTE)
    def _():
        _issue_rows(row_tok_ref, blk_ref[i + 1] * tmb, xt_hbm, xbuf.at[nxt], sem.at[nxt], tmb)

    for v in range(2):
        @pl.when(jnp.logical_and(cast_ref[i] == 1, par == v))
        def _(v=v):
            w1n, w3n, w2n = weights[1 - v]
            r13, r2 = w1f_ref.shape[0], w2f_ref.shape[0]
            c = cidx_ref[i]
            w1n[pl.ds(pl.multiple_of(c * r13, r13), r13), :] = w1f_ref[...].astype(BF16)
            w3n[pl.ds(pl.multiple_of(c * r13, r13), r13), :] = w3f_ref[...].astype(BF16)
            w2n[pl.ds(pl.multiple_of(c * r2, r2), r2), :] = w2f_ref[...].astype(BF16)

    for v in range(2):
        @pl.when(jnp.logical_and(mode == MODE_COMPUTE, par == v))
        def _(v=v):
            w1c, w3c, w2c = weights[v]
            _wait_rows(xbuf.at[slot], sem.at[slot])
            for c in range(n_chunks):
                h_scr[:, c * LANES:(c + 1) * LANES] = _dense_chunk(xbuf, (slot,), c).astype(BF16)
            h = h_scr[...]
            a = jnp.dot(h, w1c[...], preferred_element_type=F32)
            b = jnp.dot(h, w3c[...], preferred_element_type=F32)
            hid = (a * jax.nn.sigmoid(a) * b).astype(BF16)
            obuf[slot] = jnp.dot(hid, w2c[...], preferred_element_type=F32)

    @pl.when(mode == MODE_ZERO)
    def _():
        obuf[slot] = jnp.zeros(obuf.shape[1:], F32)

    @pl.when(mode != MODE_IDLE)
    def _():
        for cp in _tm_write_copies(obuf.at[slot], yb_hbm, blk_ref[i] * tmb, osem.at[slot]):
            cp.start()

    @pl.when(i == last)
    def _():
        @pl.when(mode != MODE_IDLE)
        def _():
            wait_out(slot)

        @pl.when(jnp.logical_and(i >= 1, mode_ref[jnp.maximum(i - 1, 0)] != MODE_IDLE))
        def _():
            wait_out(nxt)


def _experts(xt, w1, w3, w2, counts, row_tok, tmb, n_blocks, layer):
    n_tok, n_chunks, _ = xt.shape
    dm = n_chunks * LANES
    de = w1.shape[3]
    assert dm % CAST_STEPS == 0 and de % CAST_STEPS == 0
    n_steps, sched = _expert_schedule(counts, tmb, n_blocks)
    wmap = lambda i, mode, blk, par, cexp, cidx, cast, rt: (layer, cexp[i], cidx[i], 0)
    bf16_slot = [pltpu.VMEM((dm, de), BF16), pltpu.VMEM((dm, de), BF16), pltpu.VMEM((de, dm), BF16)]
    return pl.pallas_call(
        functools.partial(_experts_kernel, tmb=tmb),
        grid_spec=pltpu.PrefetchScalarGridSpec(
            num_scalar_prefetch=7,
            grid=(n_steps,),
            in_specs=[
                pl.BlockSpec(memory_space=pl.ANY),
                pl.BlockSpec((None, None, dm // CAST_STEPS, de), wmap),
                pl.BlockSpec((None, None, dm // CAST_STEPS, de), wmap),
                pl.BlockSpec((None, None, de // CAST_STEPS, dm), wmap),
            ],
            out_specs=pl.BlockSpec(memory_space=pl.ANY),
            scratch_shapes=[
                pltpu.VMEM((2, tmb // SUBLANES, n_chunks, SUBLANES, LANES), F32),
                pltpu.VMEM((tmb, dm), BF16),
                pltpu.VMEM((2, tmb, dm), F32),
                *bf16_slot, *bf16_slot,
                pltpu.SemaphoreType.DMA((2,)), pltpu.SemaphoreType.DMA((2,))],
        ),
        out_shape=jax.ShapeDtypeStruct((n_blocks * tmb, n_chunks, LANES), F32),
        compiler_params=pltpu.CompilerParams(
            dimension_semantics=("arbitrary",), vmem_limit_bytes=VMEM_LIMIT),
        name="experts",
    )(*sched, row_tok, xt, w1, w3, w2)


def _combine_kernel(dest_ref, x_ref, route_ref, gf_ref, yb_hbm, o_ref, buf, sem, *, tm, n_tok, final):
    i = pl.program_id(0)
    n_chunks = yb_hbm.shape[1]
    dm = n_chunks * LANES

    def issue(step, slot):
        for k in range(TOP_K):
            _issue_rows(dest_ref, k * n_tok + step * tm, yb_hbm, buf.at[slot, k], sem.at[slot, k], tm)

    @pl.when(i == 0)
    def _():
        issue(0, 0)

    @pl.when(i + 1 < pl.num_programs(0))
    def _():
        issue(i + 1, (i + 1) % 2)

    slot = i % 2
    for k in range(TOP_K):
        _wait_rows(buf.at[slot, k], sem.at[slot, k])
    gate0 = route_ref[:, 2:3]
    gate1 = route_ref[:, 3:4]
    ss = jnp.zeros((tm, 1), F32)
    for c in range(n_chunks):
        cs = slice(c * LANES, (c + 1) * LANES)
        oc = (x_ref[:, cs] + gate0 * _dense_chunk(buf, (slot, 0), c)
              + gate1 * _dense_chunk(buf, (slot, 1), c))
        o_ref[:, cs] = oc
        if final:
            ss = ss + jnp.sum(oc * oc, axis=-1, keepdims=True)
    if final:
        rs = lax.rsqrt(ss * (1.0 / dm) + EPS)
        for c in range(n_chunks):
            cs = slice(c * LANES, (c + 1) * LANES)
            o_ref[:, cs] = o_ref[:, cs] * rs * gf_ref[:, cs]


def _combine(x2, route, yb, dest, gf, final):
    n_tok, dm = x2.shape
    n_chunks = dm // LANES
    tm = _tile(n_tok, 256)
    return pl.pallas_call(
        functools.partial(_combine_kernel, tm=tm, n_tok=n_tok, final=final),
        grid_spec=pltpu.PrefetchScalarGridSpec(
            num_scalar_prefetch=1,
            grid=(n_tok // tm,),
            in_specs=[
                pl.BlockSpec((tm, dm), lambda i, d: (i, 0)),
                pl.BlockSpec((tm, LANES), lambda i, d: (i, 0)),
                pl.BlockSpec((1, dm), lambda i, d: (0, 0)),
                pl.BlockSpec(memory_space=pl.ANY),
            ],
            out_specs=pl.BlockSpec((tm, dm), lambda i, d: (i, 0)),
            scratch_shapes=[pltpu.VMEM((2, TOP_K, tm // SUBLANES, n_chunks, SUBLANES, LANES), F32),
                            pltpu.SemaphoreType.DMA((2, TOP_K))],
        ),
        out_shape=jax.ShapeDtypeStruct((n_tok, dm), F32),
        compiler_params=pltpu.CompilerParams(
            dimension_semantics=("arbitrary",), vmem_limit_bytes=VMEM_LIMIT),
        name="combine",
    )(dest, x2, route, gf.reshape(1, dm), yb)


SCATTER_UNROLL = 16


def _row_tok_kernel(dest_ref, pad_hbm, row_tok_hbm, rt_smem, sem, *, n_tok):
    fill = pltpu.make_async_copy(pad_hbm, rt_smem, sem)
    fill.start()
    fill.wait()
    for k in range(TOP_K):
        def scatter(j, carry):
            for u in range(SCATTER_UNROLL):
                t = j * SCATTER_UNROLL + u
                rt_smem[dest_ref[k * n_tok + t]] = t
            return carry

        lax.fori_loop(0, n_tok // SCATTER_UNROLL, scatter, 0)
    out = pltpu.make_async_copy(rt_smem, row_tok_hbm, sem)
    out.start()
    out.wait()


def _row_tok(dest_kmajor, n_tok, n_rows):
    assert n_tok % SCATTER_UNROLL == 0
    pad_tok = jnp.arange(n_rows, dtype=jnp.int32) % n_tok
    return pl.pallas_call(
        functools.partial(_row_tok_kernel, n_tok=n_tok),
        in_specs=[pl.BlockSpec(memory_space=pltpu.SMEM), pl.BlockSpec(memory_space=pl.ANY)],
        out_specs=pl.BlockSpec(memory_space=pl.ANY),
        out_shape=jax.ShapeDtypeStruct((n_rows,), jnp.int32),
        scratch_shapes=[pltpu.SMEM((n_rows,), jnp.int32), pltpu.SemaphoreType.DMA],
        name="row_tok",
    )(dest_kmajor, pad_tok)


def _plan(route_t, counts, tmb, n_blocks):
    n_tok = route_t.shape[1]
    e = route_t[0:TOP_K].astype(jnp.int32)
    rank = route_t[4:4 + TOP_K].astype(jnp.int32)
    pcounts = (counts + tmb - 1) // tmb * tmb
    pends = jnp.cumsum(pcounts)
    pstarts = pends - pcounts
    dest_kmajor = (_lookup(pstarts, e) + rank).reshape(n_tok * TOP_K)
    row_tok = _row_tok(dest_kmajor, n_tok, n_blocks * tmb)
    return dest_kmajor, row_tok


def kernel(x, norm_mix_g, w_in, conv_w, w_out, norm_ffn_g, w_group, b_group, w_router, b_router,
           w_expert_gate, w_expert_up, w_expert_down, final_norm_g):
    bsz, seq, dm = x.shape
    depth = w_in.shape[0]
    n_tok = bsz * seq
    tmb = _tile(n_tok * TOP_K, 256)
    n_blocks = n_tok * TOP_K // tmb + N_EXPERTS
    x2 = x.reshape(n_tok, dm)
    w_in_b, w_out_b = w_in.astype(BF16), w_out.astype(BF16)
    for l in range(depth):
        z = _in_proj(x2, norm_mix_g[l], w_in_b, l)
        x2 = _mixer(z, x2, w_out_b, conv_w[l], bsz, seq, l)
        route, route_t, counts, ht = _router(x2, norm_ffn_g[l], w_group[l], b_group[l], w_router[l],
                                             b_router[l])
        dest, row_tok = _plan(route_t, counts, tmb, n_blocks)
        yb = _experts(ht, w_expert_gate, w_expert_up, w_expert_down, counts, row_tok, tmb, n_blocks, l)
        x2 = _combine(x2, route, yb, dest, final_norm_g, final=(l == depth - 1))
    return x2.reshape(bsz, seq, dm)
```

```python
import functools

import jax
import jax.numpy as jnp
from jax import lax
from jax.experimental import pallas as pl
from jax.experimental.pallas import tpu as pltpu

CHUNK = 64
RET_HEAD_DIM = 128
CONV_WIDTH = 3
ROPE_BASE = 10000.0
N_GROUPS = 4
EXPERTS_PER_GROUP = 8
N_EXPERTS = N_GROUPS * EXPERTS_PER_GROUP
TOP_K = 2
EPS = 1e-6

LANES = 128
SUBLANES = 8
VMEM_LIMIT = 56 * 1024 * 1024

F32 = jnp.float32
BF16 = jnp.bfloat16


def _tile(n, want):
    t = min(n, want)
    while n % t:
        t //= 2
    return t


def _rms(x, g):
    return x * lax.rsqrt(jnp.mean(x * x, axis=-1, keepdims=True) + EPS) * g


def _tm_write_copies(src_dense, dst_tm_hbm, row0, sem):
    rows = src_dense.shape[0]
    return [pltpu.make_async_copy(src_dense.at[:, pl.ds(c * LANES, LANES)],
                                  dst_tm_hbm.at[pl.ds(row0, rows), c, :], sem)
            for c in range(dst_tm_hbm.shape[1])]


def _dense_chunk(buf, lead, c):
    v = buf[(*lead, slice(None), c)]
    return v.reshape(v.shape[0] * SUBLANES, LANES)


def _in_proj_kernel(x_ref, g_ref, w_ref, o_ref, h_scr):
    @pl.when(pl.program_id(1) == 0)
    def _():
        h_scr[...] = _rms(x_ref[...], g_ref[...]).astype(BF16)

    o_ref[...] = jnp.dot(h_scr[...], w_ref[...], preferred_element_type=F32).astype(o_ref.dtype)


def _in_proj(x2, g, w_bf16, layer):
    n_tok, dm = x2.shape
    n_out = w_bf16.shape[2]
    tm = _tile(n_tok, 1024)
    tn = _tile(n_out, 1792)
    return pl.pallas_call(
        _in_proj_kernel,
        grid=(n_tok // tm, n_out // tn),
        in_specs=[
            pl.BlockSpec((tm, dm), lambda i, j: (i, 0)),
            pl.BlockSpec((1, dm), lambda i, j: (0, 0)),
            pl.BlockSpec((None, dm, tn), lambda i, j: (layer, 0, j)),
        ],
        out_specs=pl.BlockSpec((tm, tn), lambda i, j: (i, j)),
        out_shape=jax.ShapeDtypeStruct((n_tok, n_out), BF16),
        scratch_shapes=[pltpu.VMEM((tm, dm), BF16)],
        compiler_params=pltpu.CompilerParams(
            dimension_semantics=("parallel", "arbitrary"), vmem_limit_bytes=VMEM_LIMIT),
        name="in_proj",
    )(x2, g.reshape(1, dm), w_bf16)


def _mixer_kernel(z_ref, x_ref, wout_ref, convw_ref, cos_ref, sin_ref, dmask_ref, qdec_ref,
                  kdec_ref, sdec_ref, o_ref, state_scr, u_scr, mixed_scr, *, ts, d_conv, n_heads):
    dh = RET_HEAD_DIM
    d_ret = n_heads * dh

    @pl.when(pl.program_id(1) == 0)
    def _():
        state_scr[...] = jnp.zeros_like(state_scr)
        u_scr[0:SUBLANES, :] = jnp.zeros((SUBLANES, d_conv), F32)

    zf = lambda lo, hi: z_ref[:, lo:hi].astype(F32)
    u_scr[SUBLANES:SUBLANES + ts, :] = zf(d_conv, 2 * d_conv) * zf(2 * d_conv, 3 * d_conv)
    y = (convw_ref[2:3, :] * u_scr[SUBLANES:SUBLANES + ts, :]
         + convw_ref[1:2, :] * u_scr[SUBLANES - 1:SUBLANES - 1 + ts, :]
         + convw_ref[0:1, :] * u_scr[SUBLANES - 2:SUBLANES - 2 + ts, :])
    mixed_scr[:, 0:d_conv] = (zf(0, d_conv) * y).astype(BF16)
    u_scr[0:SUBLANES, :] = u_scr[ts:ts + SUBLANES, :]
    o_ref[...] = x_ref[...] + jnp.dot(mixed_scr[:, 0:d_conv], wout_ref[0:d_conv, :],
                                      preferred_element_type=F32)

    cos = cos_ref[...]
    sin = sin_ref[...]
    scale = RET_HEAD_DIM ** -0.5
    base = 3 * d_conv
    for h in range(n_heads):
        c0 = h * dh
        q = zf(base + c0, base + c0 + dh)
        k = zf(base + d_ret + c0, base + d_ret + c0 + dh)
        v = z_ref[:, base + 2 * d_ret + c0:base + 2 * d_ret + c0 + dh]
        g = zf(base + 3 * d_ret + c0, base + 3 * d_ret + c0 + dh)
        qr = q * cos + pltpu.roll(q, dh // 2, 1) * sin
        kr = (k * cos + pltpu.roll(k, dh // 2, 1) * sin) * scale
        s = lax.dot_general(qr.astype(BF16), kr.astype(BF16), (((1,), (1,)), ((), ())),
                            preferred_element_type=F32) * dmask_ref[h]
        o = jnp.dot(s.astype(BF16), v, preferred_element_type=F32)
        st = state_scr[h]
        o = o + jnp.dot((qr * qdec_ref[:, c0:c0 + dh]).astype(BF16), st.astype(BF16),
                        preferred_element_type=F32)
        kv = lax.dot_general((kr * kdec_ref[:, c0:c0 + dh]).astype(BF16), v,
                             (((0,), (0,)), ((), ())), preferred_element_type=F32)
        state_scr[h] = st * sdec_ref[:, c0:c0 + dh] + kv
        on = o * lax.rsqrt(jnp.mean(o * o, axis=-1, keepdims=True) + EPS)
        mixed_scr[:, d_conv + c0:d_conv + c0 + dh] = (on * (g * jax.nn.sigmoid(g))).astype(BF16)

    o_ref[...] += jnp.dot(mixed_scr[:, d_conv:], wout_ref[d_conv:, :], preferred_element_type=F32)


def _retention_tables(seq, ts, n_heads):
    dh = RET_HEAD_DIM
    half = dh // 2
    pos = jnp.arange(seq, dtype=F32)
    inv = ROPE_BASE ** (-jnp.arange(half, dtype=F32) / half)
    ang = pos[:, None] * inv[None, :]
    cos = jnp.cos(ang)
    sin = jnp.sin(ang)
    cos_full = jnp.concatenate([cos, cos], axis=-1)
    sin_signed = jnp.concatenate([-sin, sin], axis=-1)
    log_g = jnp.log1p(-jnp.exp2(-5.0 - jnp.arange(n_heads, dtype=F32)))
    idx = jnp.arange(ts, dtype=F32)
    dist = jnp.abs(idx[:, None] - idx[None, :])
    chunk_id = jnp.arange(ts) // CHUNK
    visible = chunk_id[None, :] <= chunk_id[:, None]
    dmask = jnp.where(visible[None], jnp.exp(log_g[:, None, None] * dist[None]), 0.0)
    rep = lambda a: jnp.repeat(a, dh, axis=-1)
    qdec = rep(jnp.exp(log_g[None, :] * (idx[:, None] + 1.0)))
    kdec = rep(jnp.exp(log_g[None, :] * (ts - 1.0 - idx[:, None])))
    sdec = rep(jnp.exp(log_g * ts)[None, :])
    return cos_full, sin_signed, dmask.astype(F32), qdec, kdec, sdec


def _mixer(z, x2, wout_bf16, conv_w, bsz, seq, layer):
    n_tok, dm = x2.shape
    d_conv = conv_w.shape[1]
    d_ret = dm - d_conv
    n_heads = d_ret // RET_HEAD_DIM
    d_in = z.shape[1]
    ts = _tile(seq, 256)
    ns = seq // ts
    cos_full, sin_signed, dmask, qdec, kdec, sdec = _retention_tables(seq, ts, n_heads)
    kern = functools.partial(_mixer_kernel, ts=ts, d_conv=d_conv, n_heads=n_heads)
    const2 = lambda b, s: (0, 0)
    return pl.pallas_call(
        kern,
        grid=(bsz, ns),
        in_specs=[
            pl.BlockSpec((ts, d_in), lambda b, s: (b * ns + s, 0)),
            pl.BlockSpec((ts, dm), lambda b, s: (b * ns + s, 0)),
            pl.BlockSpec((None, dm, dm), lambda b, s: (layer, 0, 0), pipeline_mode=pl.Buffered(1)),
            pl.BlockSpec((CONV_WIDTH, d_conv), const2),
            pl.BlockSpec((ts, RET_HEAD_DIM), lambda b, s: (s, 0)),
            pl.BlockSpec((ts, RET_HEAD_DIM), lambda b, s: (s, 0)),
            pl.BlockSpec((n_heads, ts, ts), lambda b, s: (0, 0, 0)),
            pl.BlockSpec((ts, d_ret), const2),
            pl.BlockSpec((ts, d_ret), const2),
            pl.BlockSpec((1, d_ret), const2),
        ],
        out_specs=pl.BlockSpec((ts, dm), lambda b, s: (b * ns + s, 0)),
        out_shape=jax.ShapeDtypeStruct((n_tok, dm), F32),
        scratch_shapes=[
            pltpu.VMEM((n_heads, RET_HEAD_DIM, RET_HEAD_DIM), F32),
            pltpu.VMEM((ts + SUBLANES, d_conv), F32),
            pltpu.VMEM((ts, dm), BF16),
        ],
        compiler_params=pltpu.CompilerParams(
            dimension_semantics=("parallel", "arbitrary"), vmem_limit_bytes=VMEM_LIMIT),
        name="mixer",
    )(z, x2, wout_bf16, conv_w, cos_full, sin_signed, dmask, qdec, kdec, sdec)


ROUTE_ROWS = SUBLANES


def _router_kernel(x_ref, g_ref, wr_ref, br_ref, tri_ref, route_ref, route_t_ref, cnt_ref, ht_hbm,
                   cnt_scr, tbuf, tsem):
    step = pl.program_id(0)
    last = pl.num_programs(0) - 1
    slot = step % 2
    tm = x_ref.shape[0]

    @pl.when(step >= 2)
    def _():
        for cp in _tm_write_copies(tbuf.at[slot], ht_hbm, 0, tsem.at[slot]):
            cp.wait()

    @pl.when(step == 0)
    def _():
        cnt_scr[...] = jnp.zeros_like(cnt_scr)

    tbuf[slot] = _rms(x_ref[...], g_ref[...])
    for cp in _tm_write_copies(tbuf.at[slot], ht_hbm, step * tm, tsem.at[slot]):
        cp.start()
    h = tbuf[slot].astype(BF16)
    logits = jnp.dot(h, wr_ref[...], preferred_element_type=F32) + br_ref[...]
    lane = lax.broadcasted_iota(jnp.int32, logits.shape, 1)
    neg = jnp.float32(-jnp.inf)
    big = jnp.int32(LANES)

    gl = jnp.where(lane < N_GROUPS, logits, neg)
    gmax = jnp.max(gl, axis=-1, keepdims=True)
    g_sel = jnp.min(jnp.where(gl == gmax, lane, big), axis=-1, keepdims=True)
    p_g = 1.0 / jnp.sum(jnp.exp(gl - gmax), axis=-1, keepdims=True)

    lo = N_GROUPS + g_sel * EXPERTS_PER_GROUP
    el = jnp.where(lane >= lo, jnp.where(lane < lo + EXPERTS_PER_GROUP, logits, neg), neg)
    v1 = jnp.max(el, axis=-1, keepdims=True)
    i1 = jnp.min(jnp.where(el == v1, lane, big), axis=-1, keepdims=True)
    el2 = jnp.where(lane == i1, neg, el)
    v2 = jnp.max(el2, axis=-1, keepdims=True)
    i2 = jnp.min(jnp.where(el2 == v2, lane, big), axis=-1, keepdims=True)
    t = jnp.exp(v2 - v1)
    gate1 = p_g / (1.0 + t)
    gate2 = p_g * t / (1.0 + t)

    tri = tri_ref[...]
    cnt = cnt_scr[...]
    hot1 = jnp.where(lane == i1, 1.0, 0.0)
    hot2 = jnp.where(lane == i2, 1.0, 0.0)
    before1 = jnp.dot(tri, hot1.astype(BF16), preferred_element_type=F32) + cnt
    cnt = cnt + jnp.sum(hot1, axis=0, keepdims=True)
    before2 = jnp.dot(tri, hot2.astype(BF16), preferred_element_type=F32) + cnt
    cnt = cnt + jnp.sum(hot2, axis=0, keepdims=True)
    rank1 = jnp.sum(hot1 * before1, axis=-1, keepdims=True)
    rank2 = jnp.sum(hot2 * before2, axis=-1, keepdims=True)
    cnt_scr[...] = cnt
    cnt_ref[...] = cnt

    e1 = (i1 - N_GROUPS).astype(F32)
    e2 = (i2 - N_GROUPS).astype(F32)
    cols = (e1, e2, gate1, gate2, rank1, rank2)
    route = jnp.zeros(logits.shape, F32)
    for j, col in enumerate(cols):
        route = jnp.where(lane == j, col, route)
    route_ref[...] = route
    route_t_ref[...] = jnp.transpose(route)[0:ROUTE_ROWS, :]

    @pl.when(step == last)
    def _():
        for s in range(2):
            for cp in _tm_write_copies(tbuf.at[s], ht_hbm, 0, tsem.at[s]):
                cp.wait()


def _router(x2, g, w_group, b_group, w_router, b_router):
    n_tok, dm = x2.shape
    n_logit = N_GROUPS + N_EXPERTS
    wr = jnp.zeros((dm, LANES), F32).at[:, :N_GROUPS].set(w_group).at[:, N_GROUPS:n_logit].set(w_router)
    br = jnp.zeros((1, LANES), F32).at[0, :N_GROUPS].set(b_group).at[0, N_GROUPS:n_logit].set(b_router)
    tm = _tile(n_tok, 512)
    assert n_tok // tm >= 2, "the token-major write pipeline needs at least two grid steps"
    row = jnp.arange(tm, dtype=jnp.int32)
    tri = (row[None, :] < row[:, None]).astype(BF16)
    route, route_t, cnt, ht = pl.pallas_call(
        _router_kernel,
        grid=(n_tok // tm,),
        in_specs=[
            pl.BlockSpec((tm, dm), lambda i: (i, 0)),
            pl.BlockSpec((1, dm), lambda i: (0, 0)),
            pl.BlockSpec((dm, LANES), lambda i: (0, 0)),
            pl.BlockSpec((1, LANES), lambda i: (0, 0)),
            pl.BlockSpec((tm, tm), lambda i: (0, 0)),
        ],
        out_specs=[pl.BlockSpec((tm, LANES), lambda i: (i, 0)),
                   pl.BlockSpec((ROUTE_ROWS, tm), lambda i: (0, i)),
                   pl.BlockSpec((1, LANES), lambda i: (0, 0)),
                   pl.BlockSpec(memory_space=pl.ANY)],
        out_shape=[jax.ShapeDtypeStruct((n_tok, LANES), F32),
                   jax.ShapeDtypeStruct((ROUTE_ROWS, n_tok), F32),
                   jax.ShapeDtypeStruct((1, LANES), F32),
                   jax.ShapeDtypeStruct((n_tok, dm // LANES, LANES), F32)],
        scratch_shapes=[pltpu.VMEM((1, LANES), F32),
                        pltpu.VMEM((2, tm, dm), F32),
                        pltpu.SemaphoreType.DMA((2,))],
        compiler_params=pltpu.CompilerParams(
            dimension_semantics=("arbitrary",), vmem_limit_bytes=VMEM_LIMIT),
        name="router",
    )(x2, g.reshape(1, dm), wr.astype(BF16), br, tri)
    return route, route_t, cnt[0, N_GROUPS:n_logit].astype(jnp.int32), ht


def _row_copy(src_tm, idx, dst, r8, s, sem):
    return pltpu.make_async_copy(src_tm.at[idx], dst.at[r8, :, s, :], sem)


def _issue_rows(idx_ref, idx_base, src_tm, dst, sem, n_rows):
    for r in range(n_rows):
        _row_copy(src_tm, idx_ref[idx_base + r], dst, r // SUBLANES, r % SUBLANES, sem).start()


def _wait_rows(dst, sem):
    pltpu.make_async_copy(dst, dst, sem).wait()


CAST_STEPS = 4
MODE_IDLE, MODE_COMPUTE, MODE_ZERO = 0, 1, 2


def _lookup(table, idx):
    pos = jnp.arange(table.shape[0], dtype=jnp.int32)
    return jnp.sum(jnp.where(idx[..., None] == pos, table, 0), axis=-1)


def _expert_schedule(counts, tmb, n_blocks):
    cs = CAST_STEPS
    n_steps = cs + n_blocks + N_EXPERTS * (cs - 1)
    nblk = (counts + tmb - 1) // tmb
    used = nblk > 0
    steps_e = jnp.where(used, jnp.maximum(nblk, cs), 0)
    step_end = cs + jnp.cumsum(steps_e)
    step_start = step_end - steps_e
    blk_start = jnp.cumsum(nblk) - nblk
    n_used_blk = jnp.sum(nblk)
    total = step_end[-1]
    eidx = jnp.arange(N_EXPERTS, dtype=jnp.int32)
    next_ge = lax.cummin(jnp.where(used, eidx, N_EXPERTS), reverse=True)
    next_gt = jnp.concatenate([next_ge[1:], jnp.full((1,), N_EXPERTS, jnp.int32)])
    seg_ord = jnp.cumsum(used.astype(jnp.int32)) - used.astype(jnp.int32)

    last_used = jnp.max(jnp.where(used, eidx, 0))

    i = jnp.arange(n_steps + 1, dtype=jnp.int32)
    e_i = jnp.minimum(jnp.sum(step_end[None, :] <= i[:, None], axis=1), N_EXPERTS - 1)
    at_e = lambda table: _lookup(table, e_i)
    p = i - at_e(step_start)
    warm = i < cs
    in_seg = jnp.logical_and(~warm, i < total)
    compute = jnp.logical_and(in_seg, p < at_e(nblk))
    zidx = i - total
    zero = (i >= total) & (i < n_steps) & (n_used_blk + zidx < n_blocks)
    mode = jnp.where(compute, MODE_COMPUTE, jnp.where(zero, MODE_ZERO, MODE_IDLE))
    blk = jnp.where(compute, at_e(blk_start) + p, jnp.where(zero, n_used_blk + zidx, 0))
    par = jnp.where(in_seg, at_e(seg_ord) % 2, 1)
    nxt = at_e(next_gt)
    has_next = jnp.logical_and(in_seg, nxt < N_EXPERTS)
    cast = warm | (has_next & (p < cs))
    cexp = jnp.where(warm, next_ge[0], jnp.where(has_next, nxt, last_used))
    cidx = jnp.where(warm, i, jnp.where(has_next, jnp.minimum(p, cs - 1), cs - 1))
    as_i32 = lambda a: a.astype(jnp.int32)
    return n_steps, tuple(map(as_i32, (mode, blk, par, cexp, cidx, cast)))


def _experts_kernel(mode_ref, blk_ref, par_ref, cexp_ref, cidx_ref, cast_ref, row_tok_ref,
                    xt_hbm, w1f_ref, w3f_ref, w2f_ref, yb_hbm,
                    xbuf, h_scr, obuf, w1a, w3a, w2a, w1b, w3b, w2b, sem, osem, *, tmb):
    del cexp_ref
    i = pl.program_id(0)
    last = pl.num_programs(0) - 1
    slot = i % 2
    nxt = (i + 1) % 2
    n_chunks = xt_hbm.shape[1]
    mode = mode_ref[i]
    par = par_ref[i]
    weights = ((w1a, w3a, w2a), (w1b, w3b, w2b))

    def wait_out(s):
        for cp in _tm_write_copies(obuf.at[s], yb_hbm, 0, osem.at[s]):
            cp.wait()

    @pl.when(jnp.logical_and(i >= 2, mode_ref[jnp.maximum(i - 2, 0)] != MODE_IDLE))
    def _():
        wait_out(slot)

    @pl.when(mode_ref[i + 1] == MODE_COMPUTE)
    def _():
        _issue_rows(row_tok_ref, blk_ref[i + 1] * tmb, xt_hbm, xbuf.at[nxt], sem.at[nxt], tmb)

    for v in range(2):
        @pl.when(jnp.logical_and(cast_ref[i] == 1, par == v))
        def _(v=v):
            w1n, w3n, w2n = weights[1 - v]
            r13, r2 = w1f_ref.shape[0], w2f_ref.shape[0]
            c = cidx_ref[i]
            w1n[pl.ds(pl.multiple_of(c * r13, r13), r13), :] = w1f_ref[...].astype(BF16)
            w3n[pl.ds(pl.multiple_of(c * r13, r13), r13), :] = w3f_ref[...].astype(BF16)
            w2n[pl.ds(pl.multiple_of(c * r2, r2), r2), :] = w2f_ref[...].astype(BF16)

    for v in range(2):
        @pl.when(jnp.logical_and(mode == MODE_COMPUTE, par == v))
        def _(v=v):
            w1c, w3c, w2c = weights[v]
            _wait_rows(xbuf.at[slot], sem.at[slot])
            for c in range(n_chunks):
                h_scr[:, c * LANES:(c + 1) * LANES] = _dense_chunk(xbuf, (slot,), c).astype(BF16)
            h = h_scr[...]
            a = jnp.dot(h, w1c[...], preferred_element_type=F32)
            b = jnp.dot(h, w3c[...], preferred_element_type=F32)
            hid = (a * jax.nn.sigmoid(a) * b).astype(BF16)
            obuf[slot] = jnp.dot(hid, w2c[...], preferred_element_type=F32)

    @pl.when(mode == MODE_ZERO)
    def _():
        obuf[slot] = jnp.zeros(obuf.shape[1:], F32)

    @pl.when(mode != MODE_IDLE)
    def _():
        for cp in _tm_write_copies(obuf.at[slot], yb_hbm, blk_ref[i] * tmb, osem.at[slot]):
            cp.start()

    @pl.when(i == last)
    def _():
        @pl.when(mode != MODE_IDLE)
        def _():
            wait_out(slot)

        @pl.when(jnp.logical_and(i >= 1, mode_ref[jnp.maximum(i - 1, 0)] != MODE_IDLE))
        def _():
            wait_out(nxt)


def _experts(xt, w1, w3, w2, counts, row_tok, tmb, n_blocks, layer):
    n_tok, n_chunks, _ = xt.shape
    dm = n_chunks * LANES
    de = w1.shape[3]
    assert dm % CAST_STEPS == 0 and de % CAST_STEPS == 0
    n_steps, sched = _expert_schedule(counts, tmb, n_blocks)
    wmap = lambda i, mode, blk, par, cexp, cidx, cast, rt: (layer, cexp[i], cidx[i], 0)
    bf16_slot = [pltpu.VMEM((dm, de), BF16), pltpu.VMEM((dm, de), BF16), pltpu.VMEM((de, dm), BF16)]
    return pl.pallas_call(
        functools.partial(_experts_kernel, tmb=tmb),
        grid_spec=pltpu.PrefetchScalarGridSpec(
            num_scalar_prefetch=7,
            grid=(n_steps,),
            in_specs=[
                pl.BlockSpec(memory_space=pl.ANY),
                pl.BlockSpec((None, None, dm // CAST_STEPS, de), wmap),
                pl.BlockSpec((None, None, dm // CAST_STEPS, de), wmap),
                pl.BlockSpec((None, None, de // CAST_STEPS, dm), wmap),
            ],
            out_specs=pl.BlockSpec(memory_space=pl.ANY),
            scratch_shapes=[
                pltpu.VMEM((2, tmb // SUBLANES, n_chunks, SUBLANES, LANES), F32),
                pltpu.VMEM((tmb, dm), BF16),
                pltpu.VMEM((2, tmb, dm), F32),
                *bf16_slot, *bf16_slot,
                pltpu.SemaphoreType.DMA((2,)), pltpu.SemaphoreType.DMA((2,))],
        ),
        out_shape=jax.ShapeDtypeStruct((n_blocks * tmb, n_chunks, LANES), F32),
        compiler_params=pltpu.CompilerParams(
            dimension_semantics=("arbitrary",), vmem_limit_bytes=VMEM_LIMIT),
        name="experts",
    )(*sched, row_tok, xt, w1, w3, w2)


def _combine_kernel(dest_ref, x_ref, route_ref, gf_ref, yb_hbm, o_ref, buf, sem, *, tm, n_tok, final):
    i = pl.program_id(0)
    n_chunks = yb_hbm.shape[1]
    dm = n_chunks * LANES

    def issue(step, slot):
        for k in range(TOP_K):
            _issue_rows(dest_ref, k * n_tok + step * tm, yb_hbm, buf.at[slot, k], sem.at[slot, k], tm)

    @pl.when(i == 0)
    def _():
        issue(0, 0)

    @pl.when(i + 1 < pl.num_programs(0))
    def _():
        issue(i + 1, (i + 1) % 2)

    slot = i % 2
    for k in range(TOP_K):
        _wait_rows(buf.at[slot, k], sem.at[slot, k])
    gate0 = route_ref[:, 2:3]
    gate1 = route_ref[:, 3:4]
    ss = jnp.zeros((tm, 1), F32)
    for c in range(n_chunks):
        cs = slice(c * LANES, (c + 1) * LANES)
        oc = (x_ref[:, cs] + gate0 * _dense_chunk(buf, (slot, 0), c)
              + gate1 * _dense_chunk(buf, (slot, 1), c))
        o_ref[:, cs] = oc
        if final:
            ss = ss + jnp.sum(oc * oc, axis=-1, keepdims=True)
    if final:
        rs = lax.rsqrt(ss * (1.0 / dm) + EPS)
        for c in range(n_chunks):
            cs = slice(c * LANES, (c + 1) * LANES)
            o_ref[:, cs] = o_ref[:, cs] * rs * gf_ref[:, cs]


def _combine(x2, route, yb, dest, gf, final):
    n_tok, dm = x2.shape
    n_chunks = dm // LANES
    tm = _tile(n_tok, 256)
    return pl.pallas_call(
        functools.partial(_combine_kernel, tm=tm, n_tok=n_tok, final=final),
        grid_spec=pltpu.PrefetchScalarGridSpec(
            num_scalar_prefetch=1,
            grid=(n_tok // tm,),
            in_specs=[
                pl.BlockSpec((tm, dm), lambda i, d: (i, 0)),
                pl.BlockSpec((tm, LANES), lambda i, d: (i, 0)),
                pl.BlockSpec((1, dm), lambda i, d: (0, 0)),
                pl.BlockSpec(memory_space=pl.ANY),
            ],
            out_specs=pl.BlockSpec((tm, dm), lambda i, d: (i, 0)),
            scratch_shapes=[pltpu.VMEM((2, TOP_K, tm // SUBLANES, n_chunks, SUBLANES, LANES), F32),
                            pltpu.SemaphoreType.DMA((2, TOP_K))],
        ),
        out_shape=jax.ShapeDtypeStruct((n_tok, dm), F32),
        compiler_params=pltpu.CompilerParams(
            dimension_semantics=("arbitrary",), vmem_limit_bytes=VMEM_LIMIT),
        name="combine",
    )(dest, x2, route, gf.reshape(1, dm), yb)


SCATTER_UNROLL = 32


def _row_tok_kernel(dest_ref, pad_hbm, row_tok_hbm, rt_smem, sem, *, n_tok):
    fill = pltpu.make_async_copy(pad_hbm, rt_smem, sem)
    fill.start()
    fill.wait()
    for k in range(TOP_K):
        def scatter(j, carry):
            for u in range(SCATTER_UNROLL):
                t = j * SCATTER_UNROLL + u
                rt_smem[dest_ref[k * n_tok + t]] = t
            return carry

        lax.fori_loop(0, n_tok // SCATTER_UNROLL, scatter, 0)
    out = pltpu.make_async_copy(rt_smem, row_tok_hbm, sem)
    out.start()
    out.wait()


def _row_tok(dest_kmajor, n_tok, n_rows):
    assert n_tok % SCATTER_UNROLL == 0
    pad_tok = jnp.arange(n_rows, dtype=jnp.int32) % n_tok
    return pl.pallas_call(
        functools.partial(_row_tok_kernel, n_tok=n_tok),
        in_specs=[pl.BlockSpec(memory_space=pltpu.SMEM), pl.BlockSpec(memory_space=pl.ANY)],
        out_specs=pl.BlockSpec(memory_space=pl.ANY),
        out_shape=jax.ShapeDtypeStruct((n_rows,), jnp.int32),
        scratch_shapes=[pltpu.SMEM((n_rows,), jnp.int32), pltpu.SemaphoreType.DMA],
        name="row_tok",
    )(dest_kmajor, pad_tok)


def _plan(route_t, counts, tmb, n_blocks):
    n_tok = route_t.shape[1]
    e = route_t[0:TOP_K].astype(jnp.int32)
    rank = route_t[4:4 + TOP_K].astype(jnp.int32)
    pcounts = (counts + tmb - 1) // tmb * tmb
    pends = jnp.cumsum(pcounts)
    pstarts = pends - pcounts
    dest_kmajor = (_lookup(pstarts, e) + rank).reshape(n_tok * TOP_K)
    row_tok = _row_tok(dest_kmajor, n_tok, n_blocks * tmb)
    return dest_kmajor, row_tok


def kernel(x, norm_mix_g, w_in, conv_w, w_out, norm_ffn_g, w_group, b_group, w_router, b_router,
           w_expert_gate, w_expert_up, w_expert_down, final_norm_g):
    bsz, seq, dm = x.shape
    depth = w_in.shape[0]
    n_tok = bsz * seq
    tmb = _tile(n_tok * TOP_K, 256)
    n_blocks = n_tok * TOP_K // tmb + N_EXPERTS
    x2 = x.reshape(n_tok, dm)
    w_in_b, w_out_b = w_in.astype(BF16), w_out.astype(BF16)
    for l in range(depth):
        z = _in_proj(x2, norm_mix_g[l], w_in_b, l)
        x2 = _mixer(z, x2, w_out_b, conv_w[l], bsz, seq, l)
        route, route_t, counts, ht = _router(x2, norm_ffn_g[l], w_group[l], b_group[l], w_router[l],
                                             b_router[l])
        dest, row_tok = _plan(route_t, counts, tmb, n_blocks)
        yb = _experts(ht, w_expert_gate, w_expert_up, w_expert_down, counts, row_tok, tmb, n_blocks, l)
        x2 = _combine(x2, route, yb, dest, final_norm_g, final=(l == depth - 1))
    return x2.reshape(bsz, seq, dm)
```

```python
import functools

import jax
import jax.numpy as jnp
from jax import lax
from jax.experimental import pallas as pl
from jax.experimental.pallas import tpu as pltpu

CHUNK = 64
RET_HEAD_DIM = 128
CONV_WIDTH = 3
ROPE_BASE = 10000.0
N_GROUPS = 4
EXPERTS_PER_GROUP = 8
N_EXPERTS = N_GROUPS * EXPERTS_PER_GROUP
TOP_K = 2
EPS = 1e-6

LANES = 128
SUBLANES = 8
VMEM_LIMIT = 56 * 1024 * 1024

F32 = jnp.float32
BF16 = jnp.bfloat16


def _tile(n, want):
    t = min(n, want)
    while n % t:
        t //= 2
    return t


def _rms(x, g):
    return x * lax.rsqrt(jnp.mean(x * x, axis=-1, keepdims=True) + EPS) * g


def _tm_write_copies(src_dense, dst_tm_hbm, row0, sem):
    rows = src_dense.shape[0]
    return [pltpu.make_async_copy(src_dense.at[:, pl.ds(c * LANES, LANES)],
                                  dst_tm_hbm.at[pl.ds(row0, rows), c, :], sem)
            for c in range(dst_tm_hbm.shape[1])]


def _pack_bf16_halves(x):
    half = x.shape[1] // 2
    lo = lax.bitcast_convert_type(x[:, :half].astype(BF16).astype(F32), jnp.uint32)
    hi = lax.bitcast_convert_type(x[:, half:].astype(BF16).astype(F32), jnp.uint32)
    return lax.shift_right_logical(lo, jnp.uint32(16)) | hi


def _unpack_bf16_halves(p):
    lo = lax.bitcast_convert_type(lax.shift_left(p, jnp.uint32(16)), F32)
    hi = lax.bitcast_convert_type(p & jnp.uint32(0xFFFF0000), F32)
    return lo, hi


def _dense_chunk(buf, lead, c):
    v = buf[(*lead, slice(None), c)]
    return v.reshape(v.shape[0] * SUBLANES, LANES)


def _in_proj_kernel(x_ref, g_ref, w_ref, o_ref, h_scr):
    @pl.when(pl.program_id(1) == 0)
    def _():
        h_scr[...] = _rms(x_ref[...], g_ref[...]).astype(BF16)

    o_ref[...] = jnp.dot(h_scr[...], w_ref[...], preferred_element_type=F32).astype(o_ref.dtype)


def _in_proj(x2, g, w_bf16, layer):
    n_tok, dm = x2.shape
    n_out = w_bf16.shape[2]
    tm = _tile(n_tok, 1024)
    tn = _tile(n_out, 1792)
    return pl.pallas_call(
        _in_proj_kernel,
        grid=(n_tok // tm, n_out // tn),
        in_specs=[
            pl.BlockSpec((tm, dm), lambda i, j: (i, 0)),
            pl.BlockSpec((1, dm), lambda i, j: (0, 0)),
            pl.BlockSpec((None, dm, tn), lambda i, j: (layer, 0, j)),
        ],
        out_specs=pl.BlockSpec((tm, tn), lambda i, j: (i, j)),
        out_shape=jax.ShapeDtypeStruct((n_tok, n_out), BF16),
        scratch_shapes=[pltpu.VMEM((tm, dm), BF16)],
        compiler_params=pltpu.CompilerParams(
            dimension_semantics=("parallel", "arbitrary"), vmem_limit_bytes=VMEM_LIMIT),
        name="in_proj",
    )(x2, g.reshape(1, dm), w_bf16)


def _mixer_kernel(z_ref, x_ref, wout_ref, convw_ref, cos_ref, sin_ref, dmask_ref, qdec_ref,
                  kdec_ref, sdec_ref, o_ref, state_scr, u_scr, mixed_scr, *, ts, d_conv, n_heads):
    dh = RET_HEAD_DIM
    d_ret = n_heads * dh

    @pl.when(pl.program_id(1) == 0)
    def _():
        state_scr[...] = jnp.zeros_like(state_scr)
        u_scr[0:SUBLANES, :] = jnp.zeros((SUBLANES, d_conv), F32)

    zf = lambda lo, hi: z_ref[:, lo:hi].astype(F32)
    u_scr[SUBLANES:SUBLANES + ts, :] = zf(d_conv, 2 * d_conv) * zf(2 * d_conv, 3 * d_conv)
    y = (convw_ref[2:3, :] * u_scr[SUBLANES:SUBLANES + ts, :]
         + convw_ref[1:2, :] * u_scr[SUBLANES - 1:SUBLANES - 1 + ts, :]
         + convw_ref[0:1, :] * u_scr[SUBLANES - 2:SUBLANES - 2 + ts, :])
    mixed_scr[:, 0:d_conv] = (zf(0, d_conv) * y).astype(BF16)
    u_scr[0:SUBLANES, :] = u_scr[ts:ts + SUBLANES, :]
    o_ref[...] = x_ref[...] + jnp.dot(mixed_scr[:, 0:d_conv], wout_ref[0:d_conv, :],
                                      preferred_element_type=F32)

    cos = cos_ref[...]
    sin = sin_ref[...]
    scale = RET_HEAD_DIM ** -0.5
    base = 3 * d_conv
    for h in range(n_heads):
        c0 = h * dh
        q = zf(base + c0, base + c0 + dh)
        k = zf(base + d_ret + c0, base + d_ret + c0 + dh)
        v = z_ref[:, base + 2 * d_ret + c0:base + 2 * d_ret + c0 + dh]
        g = zf(base + 3 * d_ret + c0, base + 3 * d_ret + c0 + dh)
        qr = q * cos + pltpu.roll(q, dh // 2, 1) * sin
        kr = (k * cos + pltpu.roll(k, dh // 2, 1) * sin) * scale
        s = lax.dot_general(qr.astype(BF16), kr.astype(BF16), (((1,), (1,)), ((), ())),
                            preferred_element_type=F32) * dmask_ref[h]
        o = jnp.dot(s.astype(BF16), v, preferred_element_type=F32)
        st = state_scr[h]
        o = o + jnp.dot((qr * qdec_ref[:, c0:c0 + dh]).astype(BF16), st.astype(BF16),
                        preferred_element_type=F32)
        kv = lax.dot_general((kr * kdec_ref[:, c0:c0 + dh]).astype(BF16), v,
                             (((0,), (0,)), ((), ())), preferred_element_type=F32)
        state_scr[h] = st * sdec_ref[:, c0:c0 + dh] + kv
        on = o * lax.rsqrt(jnp.mean(o * o, axis=-1, keepdims=True) + EPS)
        mixed_scr[:, d_conv + c0:d_conv + c0 + dh] = (on * (g * jax.nn.sigmoid(g))).astype(BF16)

    o_ref[...] += jnp.dot(mixed_scr[:, d_conv:], wout_ref[d_conv:, :], preferred_element_type=F32)


def _retention_tables(seq, ts, n_heads):
    dh = RET_HEAD_DIM
    half = dh // 2
    pos = jnp.arange(seq, dtype=F32)
    inv = ROPE_BASE ** (-jnp.arange(half, dtype=F32) / half)
    ang = pos[:, None] * inv[None, :]
    cos = jnp.cos(ang)
    sin = jnp.sin(ang)
    cos_full = jnp.concatenate([cos, cos], axis=-1)
    sin_signed = jnp.concatenate([-sin, sin], axis=-1)
    log_g = jnp.log1p(-jnp.exp2(-5.0 - jnp.arange(n_heads, dtype=F32)))
    idx = jnp.arange(ts, dtype=F32)
    dist = jnp.abs(idx[:, None] - idx[None, :])
    chunk_id = jnp.arange(ts) // CHUNK
    visible = chunk_id[None, :] <= chunk_id[:, None]
    dmask = jnp.where(visible[None], jnp.exp(log_g[:, None, None] * dist[None]), 0.0)
    rep = lambda a: jnp.repeat(a, dh, axis=-1)
    qdec = rep(jnp.exp(log_g[None, :] * (idx[:, None] + 1.0)))
    kdec = rep(jnp.exp(log_g[None, :] * (ts - 1.0 - idx[:, None])))
    sdec = rep(jnp.exp(log_g * ts)[None, :])
    return cos_full, sin_signed, dmask.astype(F32), qdec, kdec, sdec


def _mixer(z, x2, wout_bf16, conv_w, bsz, seq, layer):
    n_tok, dm = x2.shape
    d_conv = conv_w.shape[1]
    d_ret = dm - d_conv
    n_heads = d_ret // RET_HEAD_DIM
    d_in = z.shape[1]
    ts = _tile(seq, 256)
    ns = seq // ts
    cos_full, sin_signed, dmask, qdec, kdec, sdec = _retention_tables(seq, ts, n_heads)
    kern = functools.partial(_mixer_kernel, ts=ts, d_conv=d_conv, n_heads=n_heads)
    const2 = lambda b, s: (0, 0)
    return pl.pallas_call(
        kern,
        grid=(bsz, ns),
        in_specs=[
            pl.BlockSpec((ts, d_in), lambda b, s: (b * ns + s, 0)),
            pl.BlockSpec((ts, dm), lambda b, s: (b * ns + s, 0)),
            pl.BlockSpec((None, dm, dm), lambda b, s: (layer, 0, 0), pipeline_mode=pl.Buffered(1)),
            pl.BlockSpec((CONV_WIDTH, d_conv), const2),
            pl.BlockSpec((ts, RET_HEAD_DIM), lambda b, s: (s, 0)),
            pl.BlockSpec((ts, RET_HEAD_DIM), lambda b, s: (s, 0)),
            pl.BlockSpec((n_heads, ts, ts), lambda b, s: (0, 0, 0)),
            pl.BlockSpec((ts, d_ret), const2),
            pl.BlockSpec((ts, d_ret), const2),
            pl.BlockSpec((1, d_ret), const2),
        ],
        out_specs=pl.BlockSpec((ts, dm), lambda b, s: (b * ns + s, 0)),
        out_shape=jax.ShapeDtypeStruct((n_tok, dm), F32),
        scratch_shapes=[
            pltpu.VMEM((n_heads, RET_HEAD_DIM, RET_HEAD_DIM), F32),
            pltpu.VMEM((ts + SUBLANES, d_conv), F32),
            pltpu.VMEM((ts, dm), BF16),
        ],
        compiler_params=pltpu.CompilerParams(
            dimension_semantics=("parallel", "arbitrary"), vmem_limit_bytes=VMEM_LIMIT),
        name="mixer",
    )(z, x2, wout_bf16, conv_w, cos_full, sin_signed, dmask, qdec, kdec, sdec)


ROUTE_ROWS = SUBLANES


def _router_kernel(x_ref, g_ref, wr_ref, br_ref, tri_ref, route_ref, route_t_ref, cnt_ref, ht_hbm,
                   cnt_scr, tbuf, tsem):
    step = pl.program_id(0)
    last = pl.num_programs(0) - 1
    slot = step % 2
    tm = x_ref.shape[0]

    @pl.when(step >= 2)
    def _():
        for cp in _tm_write_copies(tbuf.at[slot], ht_hbm, 0, tsem.at[slot]):
            cp.wait()

    @pl.when(step == 0)
    def _():
        cnt_scr[...] = jnp.zeros_like(cnt_scr)

    tbuf[slot] = _rms(x_ref[...], g_ref[...])
    for cp in _tm_write_copies(tbuf.at[slot], ht_hbm, step * tm, tsem.at[slot]):
        cp.start()
    h = tbuf[slot].astype(BF16)
    logits = jnp.dot(h, wr_ref[...], preferred_element_type=F32) + br_ref[...]
    lane = lax.broadcasted_iota(jnp.int32, logits.shape, 1)
    neg = jnp.float32(-jnp.inf)
    big = jnp.int32(LANES)

    gl = jnp.where(lane < N_GROUPS, logits, neg)
    gmax = jnp.max(gl, axis=-1, keepdims=True)
    g_sel = jnp.min(jnp.where(gl == gmax, lane, big), axis=-1, keepdims=True)
    p_g = 1.0 / jnp.sum(jnp.exp(gl - gmax), axis=-1, keepdims=True)

    lo = N_GROUPS + g_sel * EXPERTS_PER_GROUP
    el = jnp.where(lane >= lo, jnp.where(lane < lo + EXPERTS_PER_GROUP, logits, neg), neg)
    v1 = jnp.max(el, axis=-1, keepdims=True)
    i1 = jnp.min(jnp.where(el == v1, lane, big), axis=-1, keepdims=True)
    el2 = jnp.where(lane == i1, neg, el)
    v2 = jnp.max(el2, axis=-1, keepdims=True)
    i2 = jnp.min(jnp.where(el2 == v2, lane, big), axis=-1, keepdims=True)
    t = jnp.exp(v2 - v1)
    gate1 = p_g / (1.0 + t)
    gate2 = p_g * t / (1.0 + t)

    tri = tri_ref[...]
    cnt = cnt_scr[...]
    hot1 = jnp.where(lane == i1, 1.0, 0.0)
    hot2 = jnp.where(lane == i2, 1.0, 0.0)
    before1 = jnp.dot(tri, hot1.astype(BF16), preferred_element_type=F32) + cnt
    cnt = cnt + jnp.sum(hot1, axis=0, keepdims=True)
    before2 = jnp.dot(tri, hot2.astype(BF16), preferred_element_type=F32) + cnt
    cnt = cnt + jnp.sum(hot2, axis=0, keepdims=True)
    rank1 = jnp.sum(hot1 * before1, axis=-1, keepdims=True)
    rank2 = jnp.sum(hot2 * before2, axis=-1, keepdims=True)
    cnt_scr[...] = cnt
    cnt_ref[...] = cnt

    e1 = (i1 - N_GROUPS).astype(F32)
    e2 = (i2 - N_GROUPS).astype(F32)
    cols = (e1, e2, gate1, gate2, rank1, rank2)
    route = jnp.zeros(logits.shape, F32)
    for j, col in enumerate(cols):
        route = jnp.where(lane == j, col, route)
    route_ref[...] = route
    route_t_ref[...] = jnp.transpose(route)[0:ROUTE_ROWS, :]

    @pl.when(step == last)
    def _():
        for s in range(2):
            for cp in _tm_write_copies(tbuf.at[s], ht_hbm, 0, tsem.at[s]):
                cp.wait()


def _router(x2, g, w_group, b_group, w_router, b_router):
    n_tok, dm = x2.shape
    n_logit = N_GROUPS + N_EXPERTS
    wr = jnp.zeros((dm, LANES), F32).at[:, :N_GROUPS].set(w_group).at[:, N_GROUPS:n_logit].set(w_router)
    br = jnp.zeros((1, LANES), F32).at[0, :N_GROUPS].set(b_group).at[0, N_GROUPS:n_logit].set(b_router)
    tm = _tile(n_tok, 512)
    assert n_tok // tm >= 2, "the token-major write pipeline needs at least two grid steps"
    row = jnp.arange(tm, dtype=jnp.int32)
    tri = (row[None, :] < row[:, None]).astype(BF16)
    route, route_t, cnt, ht = pl.pallas_call(
        _router_kernel,
        grid=(n_tok // tm,),
        in_specs=[
            pl.BlockSpec((tm, dm), lambda i: (i, 0)),
            pl.BlockSpec((1, dm), lambda i: (0, 0)),
            pl.BlockSpec((dm, LANES), lambda i: (0, 0)),
            pl.BlockSpec((1, LANES), lambda i: (0, 0)),
            pl.BlockSpec((tm, tm), lambda i: (0, 0)),
        ],
        out_specs=[pl.BlockSpec((tm, LANES), lambda i: (i, 0)),
                   pl.BlockSpec((ROUTE_ROWS, tm), lambda i: (0, i)),
                   pl.BlockSpec((1, LANES), lambda i: (0, 0)),
                   pl.BlockSpec(memory_space=pl.ANY)],
        out_shape=[jax.ShapeDtypeStruct((n_tok, LANES), F32),
                   jax.ShapeDtypeStruct((ROUTE_ROWS, n_tok), F32),
                   jax.ShapeDtypeStruct((1, LANES), F32),
                   jax.ShapeDtypeStruct((n_tok, dm // LANES, LANES), F32)],
        scratch_shapes=[pltpu.VMEM((1, LANES), F32),
                        pltpu.VMEM((2, tm, dm), F32),
                        pltpu.SemaphoreType.DMA((2,))],
        compiler_params=pltpu.CompilerParams(
            dimension_semantics=("arbitrary",), vmem_limit_bytes=VMEM_LIMIT),
        name="router",
    )(x2, g.reshape(1, dm), wr.astype(BF16), br, tri)
    return route, route_t, cnt[0, N_GROUPS:n_logit].astype(jnp.int32), ht


def _row_copy(src_tm, idx, dst, r8, s, sem):
    return pltpu.make_async_copy(src_tm.at[idx], dst.at[r8, :, s, :], sem)


def _issue_rows(idx_ref, idx_base, src_tm, dst, sem, n_rows):
    for r in range(n_rows):
        _row_copy(src_tm, idx_ref[idx_base + r], dst, r // SUBLANES, r % SUBLANES, sem).start()


def _wait_rows(dst, sem):
    pltpu.make_async_copy(dst, dst, sem).wait()


CAST_STEPS = 4
MODE_IDLE, MODE_COMPUTE, MODE_ZERO = 0, 1, 2


def _lookup(table, idx):
    pos = jnp.arange(table.shape[0], dtype=jnp.int32)
    return jnp.sum(jnp.where(idx[..., None] == pos, table, 0), axis=-1)


def _expert_schedule(counts, tmb, n_blocks):
    cs = CAST_STEPS
    n_steps = cs + n_blocks + N_EXPERTS * (cs - 1)
    nblk = (counts + tmb - 1) // tmb
    used = nblk > 0
    steps_e = jnp.where(used, jnp.maximum(nblk, cs), 0)
    step_end = cs + jnp.cumsum(steps_e)
    step_start = step_end - steps_e
    blk_start = jnp.cumsum(nblk) - nblk
    n_used_blk = jnp.sum(nblk)
    total = step_end[-1]
    eidx = jnp.arange(N_EXPERTS, dtype=jnp.int32)
    next_ge = lax.cummin(jnp.where(used, eidx, N_EXPERTS), reverse=True)
    next_gt = jnp.concatenate([next_ge[1:], jnp.full((1,), N_EXPERTS, jnp.int32)])
    seg_ord = jnp.cumsum(used.astype(jnp.int32)) - used.astype(jnp.int32)

    last_used = jnp.max(jnp.where(used, eidx, 0))

    i = jnp.arange(n_steps + 1, dtype=jnp.int32)
    e_i = jnp.minimum(jnp.sum(step_end[None, :] <= i[:, None], axis=1), N_EXPERTS - 1)
    at_e = lambda table: _lookup(table, e_i)
    p = i - at_e(step_start)
    warm = i < cs
    in_seg = jnp.logical_and(~warm, i < total)
    compute = jnp.logical_and(in_seg, p < at_e(nblk))
    zidx = i - total
    zero = (i >= total) & (i < n_steps) & (n_used_blk + zidx < n_blocks)
    mode = jnp.where(compute, MODE_COMPUTE, jnp.where(zero, MODE_ZERO, MODE_IDLE))
    blk = jnp.where(compute, at_e(blk_start) + p, jnp.where(zero, n_used_blk + zidx, 0))
    par = jnp.where(in_seg, at_e(seg_ord) % 2, 1)
    nxt = at_e(next_gt)
    has_next = jnp.logical_and(in_seg, nxt < N_EXPERTS)
    cast = warm | (has_next & (p < cs))
    cexp = jnp.where(warm, next_ge[0], jnp.where(has_next, nxt, last_used))
    cidx = jnp.where(warm, i, jnp.where(has_next, jnp.minimum(p, cs - 1), cs - 1))
    as_i32 = lambda a: a.astype(jnp.int32)
    return n_steps, tuple(map(as_i32, (mode, blk, par, cexp, cidx, cast)))


def _experts_kernel(mode_ref, blk_ref, par_ref, cexp_ref, cidx_ref, cast_ref, row_tok_ref,
                    xt_hbm, w1f_ref, w3f_ref, w2f_ref, yb_hbm,
                    xbuf, h_scr, obuf, w1a, w3a, w2a, w1b, w3b, w2b, sem, osem, *, tmb):
    del cexp_ref
    i = pl.program_id(0)
    last = pl.num_programs(0) - 1
    slot = i % 2
    nxt = (i + 1) % 2
    n_chunks = xt_hbm.shape[1]
    mode = mode_ref[i]
    par = par_ref[i]
    weights = ((w1a, w3a, w2a), (w1b, w3b, w2b))

    def wait_out(s):
        for cp in _tm_write_copies(obuf.at[s], yb_hbm, 0, osem.at[s]):
            cp.wait()

    @pl.when(jnp.logical_and(i >= 2, mode_ref[jnp.maximum(i - 2, 0)] != MODE_IDLE))
    def _():
        wait_out(slot)

    @pl.when(mode_ref[i + 1] == MODE_COMPUTE)
    def _():
        _issue_rows(row_tok_ref, blk_ref[i + 1] * tmb, xt_hbm, xbuf.at[nxt], sem.at[nxt], tmb)

    for v in range(2):
        @pl.when(jnp.logical_and(cast_ref[i] == 1, par == v))
        def _(v=v):
            w1n, w3n, w2n = weights[1 - v]
            r13, r2 = w1f_ref.shape[0], w2f_ref.shape[0]
            c = cidx_ref[i]
            w1n[pl.ds(pl.multiple_of(c * r13, r13), r13), :] = w1f_ref[...].astype(BF16)
            w3n[pl.ds(pl.multiple_of(c * r13, r13), r13), :] = w3f_ref[...].astype(BF16)
            w2n[pl.ds(pl.multiple_of(c * r2, r2), r2), :] = w2f_ref[...].astype(BF16)

    for v in range(2):
        @pl.when(jnp.logical_and(mode == MODE_COMPUTE, par == v))
        def _(v=v):
            w1c, w3c, w2c = weights[v]
            _wait_rows(xbuf.at[slot], sem.at[slot])
            for c in range(n_chunks):
                h_scr[:, c * LANES:(c + 1) * LANES] = _dense_chunk(xbuf, (slot,), c).astype(BF16)
            h = h_scr[...]
            a = jnp.dot(h, w1c[...], preferred_element_type=F32)
            b = jnp.dot(h, w3c[...], preferred_element_type=F32)
            hid = (a * jax.nn.sigmoid(a) * b).astype(BF16)
            obuf[slot] = _pack_bf16_halves(jnp.dot(hid, w2c[...], preferred_element_type=F32))

    @pl.when(mode == MODE_ZERO)
    def _():
        obuf[slot] = jnp.zeros(obuf.shape[1:], obuf.dtype)

    @pl.when(mode != MODE_IDLE)
    def _():
        for cp in _tm_write_copies(obuf.at[slot], yb_hbm, blk_ref[i] * tmb, osem.at[slot]):
            cp.start()

    @pl.when(i == last)
    def _():
        @pl.when(mode != MODE_IDLE)
        def _():
            wait_out(slot)

        @pl.when(jnp.logical_and(i >= 1, mode_ref[jnp.maximum(i - 1, 0)] != MODE_IDLE))
        def _():
            wait_out(nxt)


def _experts(xt, w1, w3, w2, counts, row_tok, tmb, n_blocks, layer):
    n_tok, n_chunks, _ = xt.shape
    dm = n_chunks * LANES
    de = w1.shape[3]
    assert dm % CAST_STEPS == 0 and de % CAST_STEPS == 0
    n_steps, sched = _expert_schedule(counts, tmb, n_blocks)
    wmap = lambda i, mode, blk, par, cexp, cidx, cast, rt: (layer, cexp[i], cidx[i], 0)
    bf16_slot = [pltpu.VMEM((dm, de), BF16), pltpu.VMEM((dm, de), BF16), pltpu.VMEM((de, dm), BF16)]
    return pl.pallas_call(
        functools.partial(_experts_kernel, tmb=tmb),
        grid_spec=pltpu.PrefetchScalarGridSpec(
            num_scalar_prefetch=7,
            grid=(n_steps,),
            in_specs=[
                pl.BlockSpec(memory_space=pl.ANY),
                pl.BlockSpec((None, None, dm // CAST_STEPS, de), wmap),
                pl.BlockSpec((None, None, dm // CAST_STEPS, de), wmap),
                pl.BlockSpec((None, None, de // CAST_STEPS, dm), wmap),
            ],
            out_specs=pl.BlockSpec(memory_space=pl.ANY),
            scratch_shapes=[
                pltpu.VMEM((2, tmb // SUBLANES, n_chunks, SUBLANES, LANES), F32),
                pltpu.VMEM((tmb, dm), BF16),
                pltpu.VMEM((2, tmb, dm // 2), jnp.uint32),
                *bf16_slot, *bf16_slot,
                pltpu.SemaphoreType.DMA((2,)), pltpu.SemaphoreType.DMA((2,))],
        ),
        out_shape=jax.ShapeDtypeStruct((n_blocks * tmb, n_chunks // 2, LANES), jnp.uint32),
        compiler_params=pltpu.CompilerParams(
            dimension_semantics=("arbitrary",), vmem_limit_bytes=VMEM_LIMIT),
        name="experts",
    )(*sched, row_tok, xt, w1, w3, w2)


def _combine_kernel(dest_ref, x_ref, route_ref, gf_ref, yb_hbm, o_ref, buf, sem, *, tm, n_tok, final):
    i = pl.program_id(0)
    n_pairs = yb_hbm.shape[1]
    dm = 2 * n_pairs * LANES

    def issue(step, slot):
        for k in range(TOP_K):
            _issue_rows(dest_ref, k * n_tok + step * tm, yb_hbm, buf.at[slot, k], sem.at[slot, k], tm)

    @pl.when(i == 0)
    def _():
        issue(0, 0)

    @pl.when(i + 1 < pl.num_programs(0))
    def _():
        issue(i + 1, (i + 1) % 2)

    slot = i % 2
    for k in range(TOP_K):
        _wait_rows(buf.at[slot, k], sem.at[slot, k])
    gate0 = route_ref[:, 2:3]
    gate1 = route_ref[:, 3:4]
    ss = jnp.zeros((tm, 1), F32)
    for c in range(n_pairs):
        y0 = _unpack_bf16_halves(_dense_chunk(buf, (slot, 0), c))
        y1 = _unpack_bf16_halves(_dense_chunk(buf, (slot, 1), c))
        for half in range(2):
            cs = slice((c + half * n_pairs) * LANES, (c + half * n_pairs + 1) * LANES)
            oc = x_ref[:, cs] + gate0 * y0[half] + gate1 * y1[half]
            o_ref[:, cs] = oc
            if final:
                ss = ss + jnp.sum(oc * oc, axis=-1, keepdims=True)
    if final:
        rs = lax.rsqrt(ss * (1.0 / dm) + EPS)
        for c in range(2 * n_pairs):
            cs = slice(c * LANES, (c + 1) * LANES)
            o_ref[:, cs] = o_ref[:, cs] * rs * gf_ref[:, cs]


def _combine(x2, route, yb, dest, gf, final):
    n_tok, dm = x2.shape
    n_pairs = yb.shape[1]
    tm = _tile(n_tok, 256)
    return pl.pallas_call(
        functools.partial(_combine_kernel, tm=tm, n_tok=n_tok, final=final),
        grid_spec=pltpu.PrefetchScalarGridSpec(
            num_scalar_prefetch=1,
            grid=(n_tok // tm,),
            in_specs=[
                pl.BlockSpec((tm, dm), lambda i, d: (i, 0)),
                pl.BlockSpec((tm, LANES), lambda i, d: (i, 0)),
                pl.BlockSpec((1, dm), lambda i, d: (0, 0)),
                pl.BlockSpec(memory_space=pl.ANY),
            ],
            out_specs=pl.BlockSpec((tm, dm), lambda i, d: (i, 0)),
            scratch_shapes=[pltpu.VMEM((2, TOP_K, tm // SUBLANES, n_pairs, SUBLANES, LANES), jnp.uint32),
                            pltpu.SemaphoreType.DMA((2, TOP_K))],
        ),
        out_shape=jax.ShapeDtypeStruct((n_tok, dm), F32),
        compiler_params=pltpu.CompilerParams(
            dimension_semantics=("arbitrary",), vmem_limit_bytes=VMEM_LIMIT),
        name="combine",
    )(dest, x2, route, gf.reshape(1, dm), yb)


SCATTER_UNROLL = 32


def _row_tok_kernel(dest_ref, pad_hbm, row_tok_hbm, rt_smem, sem, *, n_tok):
    fill = pltpu.make_async_copy(pad_hbm, rt_smem, sem)
    fill.start()
    fill.wait()
    for k in range(TOP_K):
        def scatter(j, carry):
            for u in range(SCATTER_UNROLL):
                t = j * SCATTER_UNROLL + u
                rt_smem[dest_ref[k * n_tok + t]] = t
            return carry

        lax.fori_loop(0, n_tok // SCATTER_UNROLL, scatter, 0)
    out = pltpu.make_async_copy(rt_smem, row_tok_hbm, sem)
    out.start()
    out.wait()


def _row_tok(dest_kmajor, n_tok, n_rows):
    assert n_tok % SCATTER_UNROLL == 0
    pad_tok = jnp.arange(n_rows, dtype=jnp.int32) % n_tok
    return pl.pallas_call(
        functools.partial(_row_tok_kernel, n_tok=n_tok),
        in_specs=[pl.BlockSpec(memory_space=pltpu.SMEM), pl.BlockSpec(memory_space=pl.ANY)],
        out_specs=pl.BlockSpec(memory_space=pl.ANY),
        out_shape=jax.ShapeDtypeStruct((n_rows,), jnp.int32),
        scratch_shapes=[pltpu.SMEM((n_rows,), jnp.int32), pltpu.SemaphoreType.DMA],
        name="row_tok",
    )(dest_kmajor, pad_tok)


def _plan(route_t, counts, tmb, n_blocks):
    n_tok = route_t.shape[1]
    e = route_t[0:TOP_K].astype(jnp.int32)
    rank = route_t[4:4 + TOP_K].astype(jnp.int32)
    pcounts = (counts + tmb - 1) // tmb * tmb
    pends = jnp.cumsum(pcounts)
    pstarts = pends - pcounts
    dest_kmajor = (_lookup(pstarts, e) + rank).reshape(n_tok * TOP_K)
    row_tok = _row_tok(dest_kmajor, n_tok, n_blocks * tmb)
    return dest_kmajor, row_tok


def kernel(x, norm_mix_g, w_in, conv_w, w_out, norm_ffn_g, w_group, b_group, w_router, b_router,
           w_expert_gate, w_expert_up, w_expert_down, final_norm_g):
    bsz, seq, dm = x.shape
    depth = w_in.shape[0]
    n_tok = bsz * seq
    tmb = _tile(n_tok * TOP_K, 256)
    n_blocks = n_tok * TOP_K // tmb + N_EXPERTS
    x2 = x.reshape(n_tok, dm)
    w_in_b, w_out_b = w_in.astype(BF16), w_out.astype(BF16)
    for l in range(depth):
        z = _in_proj(x2, norm_mix_g[l], w_in_b, l)
        x2 = _mixer(z, x2, w_out_b, conv_w[l], bsz, seq, l)
        route, route_t, counts, ht = _router(x2, norm_ffn_g[l], w_group[l], b_group[l], w_router[l],
                                             b_router[l])
        dest, row_tok = _plan(route_t, counts, tmb, n_blocks)
        yb = _experts(ht, w_expert_gate, w_expert_up, w_expert_down, counts, row_tok, tmb, n_blocks, l)
        x2 = _combine(x2, route, yb, dest, final_norm_g, final=(l == depth - 1))
    return x2.reshape(bsz, seq, dm)
```

```python
import functools

import jax
import jax.numpy as jnp
from jax import lax
from jax.experimental import pallas as pl
from jax.experimental.pallas import tpu as pltpu

CHUNK = 64
RET_HEAD_DIM = 128
CONV_WIDTH = 3
ROPE_BASE = 10000.0
N_GROUPS = 4
EXPERTS_PER_GROUP = 8
N_EXPERTS = N_GROUPS * EXPERTS_PER_GROUP
TOP_K = 2
EPS = 1e-6

LANES = 128
SUBLANES = 8
VMEM_LIMIT = 56 * 1024 * 1024

F32 = jnp.float32
BF16 = jnp.bfloat16


def _tile(n, want):
    t = min(n, want)
    while n % t:
        t //= 2
    return t


def _rms(x, g):
    return x * lax.rsqrt(jnp.mean(x * x, axis=-1, keepdims=True) + EPS) * g


def _tm_write_copies(src_dense, dst_tm_hbm, row0, sem):
    rows = src_dense.shape[0]
    return [pltpu.make_async_copy(src_dense.at[:, pl.ds(c * LANES, LANES)],
                                  dst_tm_hbm.at[pl.ds(row0, rows), c, :], sem)
            for c in range(dst_tm_hbm.shape[1])]


def _dense_chunk(buf, lead, c):
    v = buf[(*lead, slice(None), c)]
    return v.reshape(v.shape[0] * SUBLANES, LANES)


def _in_proj_kernel(x_ref, g_ref, w_ref, o_ref, h_scr):
    @pl.when(pl.program_id(1) == 0)
    def _():
        h_scr[...] = _rms(x_ref[...], g_ref[...]).astype(BF16)

    o_ref[...] = jnp.dot(h_scr[...], w_ref[...], preferred_element_type=F32).astype(o_ref.dtype)


def _in_proj(x2, g, w_bf16, layer):
    n_tok, dm = x2.shape
    n_out = w_bf16.shape[2]
    tm = _tile(n_tok, 1024)
    tn = _tile(n_out, 1792)
    return pl.pallas_call(
        _in_proj_kernel,
        grid=(n_tok // tm, n_out // tn),
        in_specs=[
            pl.BlockSpec((tm, dm), lambda i, j: (i, 0)),
            pl.BlockSpec((1, dm), lambda i, j: (0, 0)),
            pl.BlockSpec((None, dm, tn), lambda i, j: (layer, 0, j)),
        ],
        out_specs=pl.BlockSpec((tm, tn), lambda i, j: (i, j)),
        out_shape=jax.ShapeDtypeStruct((n_tok, n_out), BF16),
        scratch_shapes=[pltpu.VMEM((tm, dm), BF16)],
        compiler_params=pltpu.CompilerParams(
            dimension_semantics=("parallel", "arbitrary"), vmem_limit_bytes=VMEM_LIMIT),
        name="in_proj",
    )(x2, g.reshape(1, dm), w_bf16)


def _mixer_kernel(z_ref, x_ref, wout_ref, convw_ref, cos_ref, sin_ref, dmask_ref, qdec_ref,
                  kdec_ref, sdec_ref, o_ref, state_scr, u_scr, mixed_scr, *, ts, d_conv, n_heads):
    dh = RET_HEAD_DIM
    d_ret = n_heads * dh

    @pl.when(pl.program_id(1) == 0)
    def _():
        state_scr[...] = jnp.zeros_like(state_scr)
        u_scr[0:SUBLANES, :] = jnp.zeros((SUBLANES, d_conv), F32)

    zf = lambda lo, hi: z_ref[:, lo:hi].astype(F32)
    u_scr[SUBLANES:SUBLANES + ts, :] = zf(d_conv, 2 * d_conv) * zf(2 * d_conv, 3 * d_conv)
    y = (convw_ref[2:3, :] * u_scr[SUBLANES:SUBLANES + ts, :]
         + convw_ref[1:2, :] * u_scr[SUBLANES - 1:SUBLANES - 1 + ts, :]
         + convw_ref[0:1, :] * u_scr[SUBLANES - 2:SUBLANES - 2 + ts, :])
    mixed_scr[:, 0:d_conv] = (zf(0, d_conv) * y).astype(BF16)
    u_scr[0:SUBLANES, :] = u_scr[ts:ts + SUBLANES, :]
    o_ref[...] = x_ref[...] + jnp.dot(mixed_scr[:, 0:d_conv], wout_ref[0:d_conv, :],
                                      preferred_element_type=F32)

    cos = cos_ref[...]
    sin = sin_ref[...]
    scale = RET_HEAD_DIM ** -0.5
    base = 3 * d_conv
    for h in range(n_heads):
        c0 = h * dh
        q = zf(base + c0, base + c0 + dh)
        k = zf(base + d_ret + c0, base + d_ret + c0 + dh)
        v = z_ref[:, base + 2 * d_ret + c0:base + 2 * d_ret + c0 + dh]
        g = zf(base + 3 * d_ret + c0, base + 3 * d_ret + c0 + dh)
        qr = q * cos + pltpu.roll(q, dh // 2, 1) * sin
        kr = (k * cos + pltpu.roll(k, dh // 2, 1) * sin) * scale
        s = lax.dot_general(qr.astype(BF16), kr.astype(BF16), (((1,), (1,)), ((), ())),
                            preferred_element_type=F32) * dmask_ref[h]
        o = jnp.dot(s.astype(BF16), v, preferred_element_type=F32)
        st = state_scr[h]
        o = o + jnp.dot((qr * qdec_ref[:, c0:c0 + dh]).astype(BF16), st.astype(BF16),
                        preferred_element_type=F32)
        kv = lax.dot_general((kr * kdec_ref[:, c0:c0 + dh]).astype(BF16), v,
                             (((0,), (0,)), ((), ())), preferred_element_type=F32)
        state_scr[h] = st * sdec_ref[:, c0:c0 + dh] + kv
        on = o * lax.rsqrt(jnp.mean(o * o, axis=-1, keepdims=True) + EPS)
        mixed_scr[:, d_conv + c0:d_conv + c0 + dh] = (on * (g * jax.nn.sigmoid(g))).astype(BF16)

    o_ref[...] += jnp.dot(mixed_scr[:, d_conv:], wout_ref[d_conv:, :], preferred_element_type=F32)


def _retention_tables(seq, ts, n_heads):
    dh = RET_HEAD_DIM
    half = dh // 2
    pos = jnp.arange(seq, dtype=F32)
    inv = ROPE_BASE ** (-jnp.arange(half, dtype=F32) / half)
    ang = pos[:, None] * inv[None, :]
    cos = jnp.cos(ang)
    sin = jnp.sin(ang)
    cos_full = jnp.concatenate([cos, cos], axis=-1)
    sin_signed = jnp.concatenate([-sin, sin], axis=-1)
    log_g = jnp.log1p(-jnp.exp2(-5.0 - jnp.arange(n_heads, dtype=F32)))
    idx = jnp.arange(ts, dtype=F32)
    dist = jnp.abs(idx[:, None] - idx[None, :])
    chunk_id = jnp.arange(ts) // CHUNK
    visible = chunk_id[None, :] <= chunk_id[:, None]
    dmask = jnp.where(visible[None], jnp.exp(log_g[:, None, None] * dist[None]), 0.0)
    rep = lambda a: jnp.repeat(a, dh, axis=-1)
    qdec = rep(jnp.exp(log_g[None, :] * (idx[:, None] + 1.0)))
    kdec = rep(jnp.exp(log_g[None, :] * (ts - 1.0 - idx[:, None])))
    sdec = rep(jnp.exp(log_g * ts)[None, :])
    return cos_full, sin_signed, dmask.astype(F32), qdec, kdec, sdec


def _mixer(z, x2, wout_bf16, conv_w, bsz, seq, layer):
    n_tok, dm = x2.shape
    d_conv = conv_w.shape[1]
    d_ret = dm - d_conv
    n_heads = d_ret // RET_HEAD_DIM
    d_in = z.shape[1]
    ts = _tile(seq, 256)
    ns = seq // ts
    cos_full, sin_signed, dmask, qdec, kdec, sdec = _retention_tables(seq, ts, n_heads)
    kern = functools.partial(_mixer_kernel, ts=ts, d_conv=d_conv, n_heads=n_heads)
    const2 = lambda b, s: (0, 0)
    return pl.pallas_call(
        kern,
        grid=(bsz, ns),
        in_specs=[
            pl.BlockSpec((ts, d_in), lambda b, s: (b * ns + s, 0)),
            pl.BlockSpec((ts, dm), lambda b, s: (b * ns + s, 0)),
            pl.BlockSpec((None, dm, dm), lambda b, s: (layer, 0, 0), pipeline_mode=pl.Buffered(1)),
            pl.BlockSpec((CONV_WIDTH, d_conv), const2),
            pl.BlockSpec((ts, RET_HEAD_DIM), lambda b, s: (s, 0)),
            pl.BlockSpec((ts, RET_HEAD_DIM), lambda b, s: (s, 0)),
            pl.BlockSpec((n_heads, ts, ts), lambda b, s: (0, 0, 0)),
            pl.BlockSpec((ts, d_ret), const2),
            pl.BlockSpec((ts, d_ret), const2),
            pl.BlockSpec((1, d_ret), const2),
        ],
        out_specs=pl.BlockSpec((ts, dm), lambda b, s: (b * ns + s, 0)),
        out_shape=jax.ShapeDtypeStruct((n_tok, dm), F32),
        scratch_shapes=[
            pltpu.VMEM((n_heads, RET_HEAD_DIM, RET_HEAD_DIM), F32),
            pltpu.VMEM((ts + SUBLANES, d_conv), F32),
            pltpu.VMEM((ts, dm), BF16),
        ],
        compiler_params=pltpu.CompilerParams(
            dimension_semantics=("parallel", "arbitrary"), vmem_limit_bytes=VMEM_LIMIT),
        name="mixer",
    )(z, x2, wout_bf16, conv_w, cos_full, sin_signed, dmask, qdec, kdec, sdec)


ROUTE_ROWS = SUBLANES


def _router_kernel(x_ref, g_ref, wr_ref, br_ref, tri_ref, route_ref, route_t_ref, cnt_ref, ht_hbm,
                   cnt_scr, tbuf, tsem):
    step = pl.program_id(0)
    last = pl.num_programs(0) - 1
    slot = step % 2
    tm = x_ref.shape[0]

    @pl.when(step >= 2)
    def _():
        for cp in _tm_write_copies(tbuf.at[slot], ht_hbm, 0, tsem.at[slot]):
            cp.wait()

    @pl.when(step == 0)
    def _():
        cnt_scr[...] = jnp.zeros_like(cnt_scr)

    tbuf[slot] = _rms(x_ref[...], g_ref[...])
    for cp in _tm_write_copies(tbuf.at[slot], ht_hbm, step * tm, tsem.at[slot]):
        cp.start()
    h = tbuf[slot].astype(BF16)
    logits = jnp.dot(h, wr_ref[...], preferred_element_type=F32) + br_ref[...]
    lane = lax.broadcasted_iota(jnp.int32, logits.shape, 1)
    neg = jnp.float32(-jnp.inf)
    big = jnp.int32(LANES)

    gl = jnp.where(lane < N_GROUPS, logits, neg)
    gmax = jnp.max(gl, axis=-1, keepdims=True)
    g_sel = jnp.min(jnp.where(gl == gmax, lane, big), axis=-1, keepdims=True)
    p_g = 1.0 / jnp.sum(jnp.exp(gl - gmax), axis=-1, keepdims=True)

    lo = N_GROUPS + g_sel * EXPERTS_PER_GROUP
    el = jnp.where(lane >= lo, jnp.where(lane < lo + EXPERTS_PER_GROUP, logits, neg), neg)
    v1 = jnp.max(el, axis=-1, keepdims=True)
    i1 = jnp.min(jnp.where(el == v1, lane, big), axis=-1, keepdims=True)
    el2 = jnp.where(lane == i1, neg, el)
    v2 = jnp.max(el2, axis=-1, keepdims=True)
    i2 = jnp.min(jnp.where(el2 == v2, lane, big), axis=-1, keepdims=True)
    t = jnp.exp(v2 - v1)
    gate1 = p_g / (1.0 + t)
    gate2 = p_g * t / (1.0 + t)

    tri = tri_ref[...]
    cnt = cnt_scr[...]
    hot1 = jnp.where(lane == i1, 1.0, 0.0)
    hot2 = jnp.where(lane == i2, 1.0, 0.0)
    before1 = jnp.dot(tri, hot1.astype(BF16), preferred_element_type=F32) + cnt
    cnt = cnt + jnp.sum(hot1, axis=0, keepdims=True)
    before2 = jnp.dot(tri, hot2.astype(BF16), preferred_element_type=F32) + cnt
    cnt = cnt + jnp.sum(hot2, axis=0, keepdims=True)
    rank1 = jnp.sum(hot1 * before1, axis=-1, keepdims=True)
    rank2 = jnp.sum(hot2 * before2, axis=-1, keepdims=True)
    cnt_scr[...] = cnt
    cnt_ref[...] = cnt

    e1 = (i1 - N_GROUPS).astype(F32)
    e2 = (i2 - N_GROUPS).astype(F32)
    cols = (e1, e2, gate1, gate2, rank1, rank2)
    route = jnp.zeros(logits.shape, F32)
    for j, col in enumerate(cols):
        route = jnp.where(lane == j, col, route)
    route_ref[...] = route
    route_t_ref[...] = jnp.transpose(route)[0:ROUTE_ROWS, :]

    @pl.when(step == last)
    def _():
        for s in range(2):
            for cp in _tm_write_copies(tbuf.at[s], ht_hbm, 0, tsem.at[s]):
                cp.wait()


def _router(x2, g, w_group, b_group, w_router, b_router):
    n_tok, dm = x2.shape
    n_logit = N_GROUPS + N_EXPERTS
    wr = jnp.zeros((dm, LANES), F32).at[:, :N_GROUPS].set(w_group).at[:, N_GROUPS:n_logit].set(w_router)
    br = jnp.zeros((1, LANES), F32).at[0, :N_GROUPS].set(b_group).at[0, N_GROUPS:n_logit].set(b_router)
    tm = _tile(n_tok, 512)
    assert n_tok // tm >= 2, "the token-major write pipeline needs at least two grid steps"
    row = jnp.arange(tm, dtype=jnp.int32)
    tri = (row[None, :] < row[:, None]).astype(BF16)
    route, route_t, cnt, ht = pl.pallas_call(
        _router_kernel,
        grid=(n_tok // tm,),
        in_specs=[
            pl.BlockSpec((tm, dm), lambda i: (i, 0)),
            pl.BlockSpec((1, dm), lambda i: (0, 0)),
            pl.BlockSpec((dm, LANES), lambda i: (0, 0)),
            pl.BlockSpec((1, LANES), lambda i: (0, 0)),
            pl.BlockSpec((tm, tm), lambda i: (0, 0)),
        ],
        out_specs=[pl.BlockSpec((tm, LANES), lambda i: (i, 0)),
                   pl.BlockSpec((ROUTE_ROWS, tm), lambda i: (0, i)),
                   pl.BlockSpec((1, LANES), lambda i: (0, 0)),
                   pl.BlockSpec(memory_space=pl.ANY)],
        out_shape=[jax.ShapeDtypeStruct((n_tok, LANES), F32),
                   jax.ShapeDtypeStruct((ROUTE_ROWS, n_tok), F32),
                   jax.ShapeDtypeStruct((1, LANES), F32),
                   jax.ShapeDtypeStruct((n_tok, dm // LANES, LANES), F32)],
        scratch_shapes=[pltpu.VMEM((1, LANES), F32),
                        pltpu.VMEM((2, tm, dm), F32),
                        pltpu.SemaphoreType.DMA((2,))],
        compiler_params=pltpu.CompilerParams(
            dimension_semantics=("arbitrary",), vmem_limit_bytes=VMEM_LIMIT),
        name="router",
    )(x2, g.reshape(1, dm), wr.astype(BF16), br, tri)
    return route, route_t, cnt[0, N_GROUPS:n_logit].astype(jnp.int32), ht


def _row_copy(src_tm, idx, dst, r8, s, sem):
    return pltpu.make_async_copy(src_tm.at[idx], dst.at[r8, :, s, :], sem)


def _issue_rows(idx_ref, idx_base, src_tm, dst, sem, n_rows):
    for r in range(n_rows):
        _row_copy(src_tm, idx_ref[idx_base + r], dst, r // SUBLANES, r % SUBLANES, sem).start()


def _wait_rows(dst, sem):
    pltpu.make_async_copy(dst, dst, sem).wait()


CAST_STEPS = 4
MODE_IDLE, MODE_COMPUTE, MODE_ZERO = 0, 1, 2


def _lookup(table, idx):
    pos = jnp.arange(table.shape[0], dtype=jnp.int32)
    return jnp.sum(jnp.where(idx[..., None] == pos, table, 0), axis=-1)


def _expert_schedule(counts, tmb, n_blocks):
    cs = CAST_STEPS
    n_steps = cs + n_blocks + N_EXPERTS * (cs - 1)
    nblk = (counts + tmb - 1) // tmb
    used = nblk > 0
    steps_e = jnp.where(used, jnp.maximum(nblk, cs), 0)
    step_end = cs + jnp.cumsum(steps_e)
    step_start = step_end - steps_e
    blk_start = jnp.cumsum(nblk) - nblk
    n_used_blk = jnp.sum(nblk)
    total = step_end[-1]
    eidx = jnp.arange(N_EXPERTS, dtype=jnp.int32)
    next_ge = lax.cummin(jnp.where(used, eidx, N_EXPERTS), reverse=True)
    next_gt = jnp.concatenate([next_ge[1:], jnp.full((1,), N_EXPERTS, jnp.int32)])
    seg_ord = jnp.cumsum(used.astype(jnp.int32)) - used.astype(jnp.int32)

    last_used = jnp.max(jnp.where(used, eidx, 0))

    i = jnp.arange(n_steps + 1, dtype=jnp.int32)
    e_i = jnp.minimum(jnp.sum(step_end[None, :] <= i[:, None], axis=1), N_EXPERTS - 1)
    at_e = lambda table: _lookup(table, e_i)
    p = i - at_e(step_start)
    warm = i < cs
    in_seg = jnp.logical_and(~warm, i < total)
    compute = jnp.logical_and(in_seg, p < at_e(nblk))
    zidx = i - total
    zero = (i >= total) & (i < n_steps) & (n_used_blk + zidx < n_blocks)
    mode = jnp.where(compute, MODE_COMPUTE, jnp.where(zero, MODE_ZERO, MODE_IDLE))
    blk = jnp.where(compute, at_e(blk_start) + p, jnp.where(zero, n_used_blk + zidx, 0))
    par = jnp.where(in_seg, at_e(seg_ord) % 2, 1)
    nxt = at_e(next_gt)
    has_next = jnp.logical_and(in_seg, nxt < N_EXPERTS)
    cast = warm | (has_next & (p < cs))
    cexp = jnp.where(warm, next_ge[0], jnp.where(has_next, nxt, last_used))
    cidx = jnp.where(warm, i, jnp.where(has_next, jnp.minimum(p, cs - 1), cs - 1))
    as_i32 = lambda a: a.astype(jnp.int32)
    return n_steps, tuple(map(as_i32, (mode, blk, par, cexp, cidx, cast)))


def _experts_kernel(mode_ref, blk_ref, par_ref, cexp_ref, cidx_ref, cast_ref, row_tok_ref,
                    xt_hbm, w1f_ref, w3f_ref, w2f_ref, yb_hbm,
                    xbuf, h_scr, obuf, w1a, w3a, w2a, w1b, w3b, w2b, sem, osem, *, tmb):
    del cexp_ref
    i = pl.program_id(0)
    last = pl.num_programs(0) - 1
    slot = i % 2
    nxt = (i + 1) % 2
    n_chunks = xt_hbm.shape[1]
    mode = mode_ref[i]
    par = par_ref[i]
    weights = ((w1a, w3a, w2a), (w1b, w3b, w2b))

    def wait_out(s):
        for cp in _tm_write_copies(obuf.at[s], yb_hbm, 0, osem.at[s]):
            cp.wait()

    @pl.when(jnp.logical_and(i >= 2, mode_ref[jnp.maximum(i - 2, 0)] != MODE_IDLE))
    def _():
        wait_out(slot)

    @pl.when(mode_ref[i + 1] == MODE_COMPUTE)
    def _():
        _issue_rows(row_tok_ref, blk_ref[i + 1] * tmb, xt_hbm, xbuf.at[nxt], sem.at[nxt], tmb)

    for v in range(2):
        @pl.when(jnp.logical_and(cast_ref[i] == 1, par == v))
        def _(v=v):
            w1n, w3n, w2n = weights[1 - v]
            r13, r2 = w1f_ref.shape[0], w2f_ref.shape[0]
            c = cidx_ref[i]
            w1n[pl.ds(pl.multiple_of(c * r13, r13), r13), :] = w1f_ref[...].astype(BF16)
            w3n[pl.ds(pl.multiple_of(c * r13, r13), r13), :] = w3f_ref[...].astype(BF16)
            w2n[pl.ds(pl.multiple_of(c * r2, r2), r2), :] = w2f_ref[...].astype(BF16)

    for v in range(2):
        @pl.when(jnp.logical_and(mode == MODE_COMPUTE, par == v))
        def _(v=v):
            w1c, w3c, w2c = weights[v]
            _wait_rows(xbuf.at[slot], sem.at[slot])
            for c in range(n_chunks):
                h_scr[:, c * LANES:(c + 1) * LANES] = _dense_chunk(xbuf, (slot,), c).astype(BF16)
            h = h_scr[...]
            a = jnp.dot(h, w1c[...], preferred_element_type=F32)
            b = jnp.dot(h, w3c[...], preferred_element_type=F32)
            hid = (a * jax.nn.sigmoid(a) * b).astype(BF16)
            obuf[slot] = jnp.dot(hid, w2c[...], preferred_element_type=F32)

    @pl.when(mode == MODE_ZERO)
    def _():
        obuf[slot] = jnp.zeros(obuf.shape[1:], F32)

    @pl.when(mode != MODE_IDLE)
    def _():
        for cp in _tm_write_copies(obuf.at[slot], yb_hbm, blk_ref[i] * tmb, osem.at[slot]):
            cp.start()

    @pl.when(i == last)
    def _():
        @pl.when(mode != MODE_IDLE)
        def _():
            wait_out(slot)

        @pl.when(jnp.logical_and(i >= 1, mode_ref[jnp.maximum(i - 1, 0)] != MODE_IDLE))
        def _():
            wait_out(nxt)


def _experts(xt, w1, w3, w2, counts, row_tok, tmb, n_blocks, layer):
    n_tok, n_chunks, _ = xt.shape
    dm = n_chunks * LANES
    de = w1.shape[3]
    assert dm % CAST_STEPS == 0 and de % CAST_STEPS == 0
    n_steps, sched = _expert_schedule(counts, tmb, n_blocks)
    wmap = lambda i, mode, blk, par, cexp, cidx, cast, rt: (layer, cexp[i], cidx[i], 0)
    bf16_slot = [pltpu.VMEM((dm, de), BF16), pltpu.VMEM((dm, de), BF16), pltpu.VMEM((de, dm), BF16)]
    return pl.pallas_call(
        functools.partial(_experts_kernel, tmb=tmb),
        grid_spec=pltpu.PrefetchScalarGridSpec(
            num_scalar_prefetch=7,
            grid=(n_steps,),
            in_specs=[
                pl.BlockSpec(memory_space=pl.ANY),
                pl.BlockSpec((None, None, dm // CAST_STEPS, de), wmap),
                pl.BlockSpec((None, None, dm // CAST_STEPS, de), wmap),
                pl.BlockSpec((None, None, de // CAST_STEPS, dm), wmap),
            ],
            out_specs=pl.BlockSpec(memory_space=pl.ANY),
            scratch_shapes=[
                pltpu.VMEM((2, tmb // SUBLANES, n_chunks, SUBLANES, LANES), F32),
                pltpu.VMEM((tmb, dm), BF16),
                pltpu.VMEM((2, tmb, dm), F32),
                *bf16_slot, *bf16_slot,
                pltpu.SemaphoreType.DMA((2,)), pltpu.SemaphoreType.DMA((2,))],
        ),
        out_shape=jax.ShapeDtypeStruct((n_blocks * tmb, n_chunks, LANES), F32),
        compiler_params=pltpu.CompilerParams(
            dimension_semantics=("arbitrary",), vmem_limit_bytes=VMEM_LIMIT),
        name="experts",
    )(*sched, row_tok, xt, w1, w3, w2)


def _combine_kernel(dest_ref, x_ref, route_ref, gf_ref, yb_hbm, o_ref, buf, sem, *, tm, n_tok, final):
    i = pl.program_id(0)
    n_chunks = yb_hbm.shape[1]
    dm = n_chunks * LANES

    def issue(step, slot):
        for k in range(TOP_K):
            _issue_rows(dest_ref, k * n_tok + step * tm, yb_hbm, buf.at[slot, k], sem.at[slot, k], tm)

    @pl.when(i == 0)
    def _():
        issue(0, 0)

    @pl.when(i + 1 < pl.num_programs(0))
    def _():
        issue(i + 1, (i + 1) % 2)

    slot = i % 2
    for k in range(TOP_K):
        _wait_rows(buf.at[slot, k], sem.at[slot, k])
    gate0 = route_ref[:, 2:3]
    gate1 = route_ref[:, 3:4]
    ss = jnp.zeros((tm, 1), F32)
    for c in range(n_chunks):
        cs = slice(c * LANES, (c + 1) * LANES)
        oc = (x_ref[:, cs] + gate0 * _dense_chunk(buf, (slot, 0), c)
              + gate1 * _dense_chunk(buf, (slot, 1), c))
        o_ref[:, cs] = oc
        if final:
            ss = ss + jnp.sum(oc * oc, axis=-1, keepdims=True)
    if final:
        rs = lax.rsqrt(ss * (1.0 / dm) + EPS)
        for c in range(n_chunks):
            cs = slice(c * LANES, (c + 1) * LANES)
            o_ref[:, cs] = o_ref[:, cs] * rs * gf_ref[:, cs]


def _combine(x2, route, yb, dest, gf, final):
    n_tok, dm = x2.shape
    n_chunks = dm // LANES
    tm = _tile(n_tok, 512)
    return pl.pallas_call(
        functools.partial(_combine_kernel, tm=tm, n_tok=n_tok, final=final),
        grid_spec=pltpu.PrefetchScalarGridSpec(
            num_scalar_prefetch=1,
            grid=(n_tok // tm,),
            in_specs=[
                pl.BlockSpec((tm, dm), lambda i, d: (i, 0)),
                pl.BlockSpec((tm, LANES), lambda i, d: (i, 0)),
                pl.BlockSpec((1, dm), lambda i, d: (0, 0)),
                pl.BlockSpec(memory_space=pl.ANY),
            ],
            out_specs=pl.BlockSpec((tm, dm), lambda i, d: (i, 0)),
            scratch_shapes=[pltpu.VMEM((2, TOP_K, tm // SUBLANES, n_chunks, SUBLANES, LANES), F32),
                            pltpu.SemaphoreType.DMA((2, TOP_K))],
        ),
        out_shape=jax.ShapeDtypeStruct((n_tok, dm), F32),
        compiler_params=pltpu.CompilerParams(
            dimension_semantics=("arbitrary",), vmem_limit_bytes=VMEM_LIMIT),
        name="combine",
    )(dest, x2, route, gf.reshape(1, dm), yb)


SCATTER_UNROLL = 32


def _row_tok_kernel(dest_ref, pad_hbm, row_tok_hbm, rt_smem, sem, *, n_tok):
    fill = pltpu.make_async_copy(pad_hbm, rt_smem, sem)
    fill.start()
    fill.wait()
    for k in range(TOP_K):
        def scatter(j, carry):
            for u in range(SCATTER_UNROLL):
                t = j * SCATTER_UNROLL + u
                rt_smem[dest_ref[k * n_tok + t]] = t
            return carry

        lax.fori_loop(0, n_tok // SCATTER_UNROLL, scatter, 0)
    out = pltpu.make_async_copy(rt_smem, row_tok_hbm, sem)
    out.start()
    out.wait()


def _row_tok(dest_kmajor, n_tok, n_rows):
    assert n_tok % SCATTER_UNROLL == 0
    pad_tok = jnp.arange(n_rows, dtype=jnp.int32) % n_tok
    return pl.pallas_call(
        functools.partial(_row_tok_kernel, n_tok=n_tok),
        in_specs=[pl.BlockSpec(memory_space=pltpu.SMEM), pl.BlockSpec(memory_space=pl.ANY)],
        out_specs=pl.BlockSpec(memory_space=pl.ANY),
        out_shape=jax.ShapeDtypeStruct((n_rows,), jnp.int32),
        scratch_shapes=[pltpu.SMEM((n_rows,), jnp.int32), pltpu.SemaphoreType.DMA],
        name="row_tok",
    )(dest_kmajor, pad_tok)


def _plan(route_t, counts, tmb, n_blocks):
    n_tok = route_t.shape[1]
    e = route_t[0:TOP_K].astype(jnp.int32)
    rank = route_t[4:4 + TOP_K].astype(jnp.int32)
    pcounts = (counts + tmb - 1) // tmb * tmb
    pends = jnp.cumsum(pcounts)
    pstarts = pends - pcounts
    dest_kmajor = (_lookup(pstarts, e) + rank).reshape(n_tok * TOP_K)
    row_tok = _row_tok(dest_kmajor, n_tok, n_blocks * tmb)
    return dest_kmajor, row_tok


def kernel(x, norm_mix_g, w_in, conv_w, w_out, norm_ffn_g, w_group, b_group, w_router, b_router,
           w_expert_gate, w_expert_up, w_expert_down, final_norm_g):
    bsz, seq, dm = x.shape
    depth = w_in.shape[0]
    n_tok = bsz * seq
    tmb = _tile(n_tok * TOP_K, 256)
    n_blocks = n_tok * TOP_K // tmb + N_EXPERTS
    x2 = x.reshape(n_tok, dm)
    w_in_b, w_out_b = w_in.astype(BF16), w_out.astype(BF16)
    for l in range(depth):
        z = _in_proj(x2, norm_mix_g[l], w_in_b, l)
        x2 = _mixer(z, x2, w_out_b, conv_w[l], bsz, seq, l)
        route, route_t, counts, ht = _router(x2, norm_ffn_g[l], w_group[l], b_group[l], w_router[l],
                                             b_router[l])
        dest, row_tok = _plan(route_t, counts, tmb, n_blocks)
        yb = _experts(ht, w_expert_gate, w_expert_up, w_expert_down, counts, row_tok, tmb, n_blocks, l)
        x2 = _combine(x2, route, yb, dest, final_norm_g, final=(l == depth - 1))
    return x2.reshape(bsz, seq, dm)
```

```python
import functools

import jax
import jax.numpy as jnp
from jax import lax
from jax.experimental import pallas as pl
from jax.experimental.pallas import tpu as pltpu

CHUNK = 64
RET_HEAD_DIM = 128
CONV_WIDTH = 3
ROPE_BASE = 10000.0
N_GROUPS = 4
EXPERTS_PER_GROUP = 8
N_EXPERTS = N_GROUPS * EXPERTS_PER_GROUP
TOP_K = 2
EPS = 1e-6

LANES = 128
SUBLANES = 8
VMEM_LIMIT = 56 * 1024 * 1024

F32 = jnp.float32
BF16 = jnp.bfloat16


def _tile(n, want):
    t = min(n, want)
    while n % t:
        t //= 2
    return t


def _rms(x, g):
    return x * lax.rsqrt(jnp.mean(x * x, axis=-1, keepdims=True) + EPS) * g


def _tm_write_copies(src_dense, dst_tm_hbm, row0, sem):
    rows = src_dense.shape[0]
    return [pltpu.make_async_copy(src_dense.at[:, pl.ds(c * LANES, LANES)],
                                  dst_tm_hbm.at[pl.ds(row0, rows), c, :], sem)
            for c in range(dst_tm_hbm.shape[1])]


def _dense_chunk(buf, lead, c):
    v = buf[(*lead, slice(None), c)]
    return v.reshape(v.shape[0] * SUBLANES, LANES)


def _in_proj_kernel(x_ref, g_ref, w_ref, o_ref, h_scr):
    @pl.when(pl.program_id(1) == 0)
    def _():
        h_scr[...] = _rms(x_ref[...], g_ref[...]).astype(BF16)

    o_ref[...] = jnp.dot(h_scr[...], w_ref[...], preferred_element_type=F32).astype(o_ref.dtype)


def _in_proj(x2, g, w_bf16, layer):
    n_tok, dm = x2.shape
    n_out = w_bf16.shape[2]
    tm = _tile(n_tok, 1024)
    tn = _tile(n_out, 1792)
    return pl.pallas_call(
        _in_proj_kernel,
        grid=(n_tok // tm, n_out // tn),
        in_specs=[
            pl.BlockSpec((tm, dm), lambda i, j: (i, 0)),
            pl.BlockSpec((1, dm), lambda i, j: (0, 0)),
            pl.BlockSpec((None, dm, tn), lambda i, j: (layer, 0, j)),
        ],
        out_specs=pl.BlockSpec((tm, tn), lambda i, j: (i, j)),
        out_shape=jax.ShapeDtypeStruct((n_tok, n_out), BF16),
        scratch_shapes=[pltpu.VMEM((tm, dm), BF16)],
        compiler_params=pltpu.CompilerParams(
            dimension_semantics=("parallel", "arbitrary"), vmem_limit_bytes=VMEM_LIMIT),
        name="in_proj",
    )(x2, g.reshape(1, dm), w_bf16)


def _mixer_kernel(z_ref, x_ref, wout_ref, convw_ref, cos_ref, sin_ref, dmask_ref, qdec_ref,
                  kdec_ref, sdec_ref, o_ref, state_scr, u_scr, mixed_scr, *, ts, d_conv, n_heads):
    dh = RET_HEAD_DIM
    d_ret = n_heads * dh

    @pl.when(pl.program_id(1) == 0)
    def _():
        state_scr[...] = jnp.zeros_like(state_scr)
        u_scr[0:SUBLANES, :] = jnp.zeros((SUBLANES, d_conv), F32)

    zf = lambda lo, hi: z_ref[:, lo:hi].astype(F32)
    u_scr[SUBLANES:SUBLANES + ts, :] = zf(d_conv, 2 * d_conv) * zf(2 * d_conv, 3 * d_conv)
    y = (convw_ref[2:3, :] * u_scr[SUBLANES:SUBLANES + ts, :]
         + convw_ref[1:2, :] * u_scr[SUBLANES - 1:SUBLANES - 1 + ts, :]
         + convw_ref[0:1, :] * u_scr[SUBLANES - 2:SUBLANES - 2 + ts, :])
    mixed_scr[:, 0:d_conv] = (zf(0, d_conv) * y).astype(BF16)
    u_scr[0:SUBLANES, :] = u_scr[ts:ts + SUBLANES, :]
    o_ref[...] = x_ref[...] + jnp.dot(mixed_scr[:, 0:d_conv], wout_ref[0:d_conv, :],
                                      preferred_element_type=F32)

    cos = cos_ref[...]
    sin = sin_ref[...]
    scale = RET_HEAD_DIM ** -0.5
    base = 3 * d_conv
    for h in range(n_heads):
        c0 = h * dh
        q = zf(base + c0, base + c0 + dh)
        k = zf(base + d_ret + c0, base + d_ret + c0 + dh)
        v = z_ref[:, base + 2 * d_ret + c0:base + 2 * d_ret + c0 + dh]
        g = zf(base + 3 * d_ret + c0, base + 3 * d_ret + c0 + dh)
        qr = q * cos + pltpu.roll(q, dh // 2, 1) * sin
        kr = (k * cos + pltpu.roll(k, dh // 2, 1) * sin) * scale
        s = lax.dot_general(qr.astype(BF16), kr.astype(BF16), (((1,), (1,)), ((), ())),
                            preferred_element_type=F32) * dmask_ref[h]
        o = jnp.dot(s.astype(BF16), v, preferred_element_type=F32)
        st = state_scr[h]
        o = o + jnp.dot((qr * qdec_ref[:, c0:c0 + dh]).astype(BF16), st.astype(BF16),
                        preferred_element_type=F32)
        kv = lax.dot_general((kr * kdec_ref[:, c0:c0 + dh]).astype(BF16), v,
                             (((0,), (0,)), ((), ())), preferred_element_type=F32)
        state_scr[h] = st * sdec_ref[:, c0:c0 + dh] + kv
        on = o * lax.rsqrt(jnp.mean(o * o, axis=-1, keepdims=True) + EPS)
        mixed_scr[:, d_conv + c0:d_conv + c0 + dh] = (on * (g * jax.nn.sigmoid(g))).astype(BF16)

    o_ref[...] += jnp.dot(mixed_scr[:, d_conv:], wout_ref[d_conv:, :], preferred_element_type=F32)


def _retention_tables(seq, ts, n_heads):
    dh = RET_HEAD_DIM
    half = dh // 2
    pos = jnp.arange(seq, dtype=F32)
    inv = ROPE_BASE ** (-jnp.arange(half, dtype=F32) / half)
    ang = pos[:, None] * inv[None, :]
    cos = jnp.cos(ang)
    sin = jnp.sin(ang)
    cos_full = jnp.concatenate([cos, cos], axis=-1)
    sin_signed = jnp.concatenate([-sin, sin], axis=-1)
    log_g = jnp.log1p(-jnp.exp2(-5.0 - jnp.arange(n_heads, dtype=F32)))
    idx = jnp.arange(ts, dtype=F32)
    dist = jnp.abs(idx[:, None] - idx[None, :])
    chunk_id = jnp.arange(ts) // CHUNK
    visible = chunk_id[None, :] <= chunk_id[:, None]
    dmask = jnp.where(visible[None], jnp.exp(log_g[:, None, None] * dist[None]), 0.0)
    rep = lambda a: jnp.repeat(a, dh, axis=-1)
    qdec = rep(jnp.exp(log_g[None, :] * (idx[:, None] + 1.0)))
    kdec = rep(jnp.exp(log_g[None, :] * (ts - 1.0 - idx[:, None])))
    sdec = rep(jnp.exp(log_g * ts)[None, :])
    return cos_full, sin_signed, dmask.astype(F32), qdec, kdec, sdec


def _mixer(z, x2, wout_bf16, conv_w, bsz, seq, layer):
    n_tok, dm = x2.shape
    d_conv = conv_w.shape[1]
    d_ret = dm - d_conv
    n_heads = d_ret // RET_HEAD_DIM
    d_in = z.shape[1]
    ts = _tile(seq, 256)
    ns = seq // ts
    cos_full, sin_signed, dmask, qdec, kdec, sdec = _retention_tables(seq, ts, n_heads)
    kern = functools.partial(_mixer_kernel, ts=ts, d_conv=d_conv, n_heads=n_heads)
    const2 = lambda b, s: (0, 0)
    return pl.pallas_call(
        kern,
        grid=(bsz, ns),
        in_specs=[
            pl.BlockSpec((ts, d_in), lambda b, s: (b * ns + s, 0)),
            pl.BlockSpec((ts, dm), lambda b, s: (b * ns + s, 0)),
            pl.BlockSpec((None, dm, dm), lambda b, s: (layer, 0, 0), pipeline_mode=pl.Buffered(1)),
            pl.BlockSpec((CONV_WIDTH, d_conv), const2),
            pl.BlockSpec((ts, RET_HEAD_DIM), lambda b, s: (s, 0)),
            pl.BlockSpec((ts, RET_HEAD_DIM), lambda b, s: (s, 0)),
            pl.BlockSpec((n_heads, ts, ts), lambda b, s: (0, 0, 0)),
            pl.BlockSpec((ts, d_ret), const2),
            pl.BlockSpec((ts, d_ret), const2),
            pl.BlockSpec((1, d_ret), const2),
        ],
        out_specs=pl.BlockSpec((ts, dm), lambda b, s: (b * ns + s, 0)),
        out_shape=jax.ShapeDtypeStruct((n_tok, dm), F32),
        scratch_shapes=[
            pltpu.VMEM((n_heads, RET_HEAD_DIM, RET_HEAD_DIM), F32),
            pltpu.VMEM((ts + SUBLANES, d_conv), F32),
            pltpu.VMEM((ts, dm), BF16),
        ],
        compiler_params=pltpu.CompilerParams(
            dimension_semantics=("parallel", "arbitrary"), vmem_limit_bytes=VMEM_LIMIT),
        name="mixer",
    )(z, x2, wout_bf16, conv_w, cos_full, sin_signed, dmask, qdec, kdec, sdec)


ROUTE_ROWS = SUBLANES


def _router_kernel(x_ref, g_ref, wr_ref, br_ref, tri_ref, route_ref, route_t_ref, cnt_ref, ht_hbm,
                   cnt_scr, tbuf, tsem):
    step = pl.program_id(0)
    last = pl.num_programs(0) - 1
    slot = step % 2
    tm = x_ref.shape[0]

    @pl.when(step >= 2)
    def _():
        for cp in _tm_write_copies(tbuf.at[slot], ht_hbm, 0, tsem.at[slot]):
            cp.wait()

    @pl.when(step == 0)
    def _():
        cnt_scr[...] = jnp.zeros_like(cnt_scr)

    tbuf[slot] = _rms(x_ref[...], g_ref[...])
    for cp in _tm_write_copies(tbuf.at[slot], ht_hbm, step * tm, tsem.at[slot]):
        cp.start()
    h = tbuf[slot].astype(BF16)
    logits = jnp.dot(h, wr_ref[...], preferred_element_type=F32) + br_ref[...]
    lane = lax.broadcasted_iota(jnp.int32, logits.shape, 1)
    neg = jnp.float32(-jnp.inf)
    big = jnp.int32(LANES)

    gl = jnp.where(lane < N_GROUPS, logits, neg)
    gmax = jnp.max(gl, axis=-1, keepdims=True)
    g_sel = jnp.min(jnp.where(gl == gmax, lane, big), axis=-1, keepdims=True)
    p_g = 1.0 / jnp.sum(jnp.exp(gl - gmax), axis=-1, keepdims=True)

    lo = N_GROUPS + g_sel * EXPERTS_PER_GROUP
    el = jnp.where(lane >= lo, jnp.where(lane < lo + EXPERTS_PER_GROUP, logits, neg), neg)
    v1 = jnp.max(el, axis=-1, keepdims=True)
    i1 = jnp.min(jnp.where(el == v1, lane, big), axis=-1, keepdims=True)
    el2 = jnp.where(lane == i1, neg, el)
    v2 = jnp.max(el2, axis=-1, keepdims=True)
    i2 = jnp.min(jnp.where(el2 == v2, lane, big), axis=-1, keepdims=True)
    t = jnp.exp(v2 - v1)
    gate1 = p_g / (1.0 + t)
    gate2 = p_g * t / (1.0 + t)

    tri = tri_ref[...]
    cnt = cnt_scr[...]
    hot1 = jnp.where(lane == i1, 1.0, 0.0)
    hot2 = jnp.where(lane == i2, 1.0, 0.0)
    before1 = jnp.dot(tri, hot1.astype(BF16), preferred_element_type=F32) + cnt
    cnt = cnt + jnp.sum(hot1, axis=0, keepdims=True)
    before2 = jnp.dot(tri, hot2.astype(BF16), preferred_element_type=F32) + cnt
    cnt = cnt + jnp.sum(hot2, axis=0, keepdims=True)
    rank1 = jnp.sum(hot1 * before1, axis=-1, keepdims=True)
    rank2 = jnp.sum(hot2 * before2, axis=-1, keepdims=True)
    cnt_scr[...] = cnt
    cnt_ref[...] = cnt

    e1 = (i1 - N_GROUPS).astype(F32)
    e2 = (i2 - N_GROUPS).astype(F32)
    cols = (e1, e2, gate1, gate2, rank1, rank2)
    route = jnp.zeros(logits.shape, F32)
    for j, col in enumerate(cols):
        route = jnp.where(lane == j, col, route)
    route_ref[...] = route
    route_t_ref[...] = jnp.transpose(route)[0:ROUTE_ROWS, :]

    @pl.when(step == last)
    def _():
        for s in range(2):
            for cp in _tm_write_copies(tbuf.at[s], ht_hbm, 0, tsem.at[s]):
                cp.wait()


def _router(x2, g, w_group, b_group, w_router, b_router):
    n_tok, dm = x2.shape
    n_logit = N_GROUPS + N_EXPERTS
    wr = jnp.zeros((dm, LANES), F32).at[:, :N_GROUPS].set(w_group).at[:, N_GROUPS:n_logit].set(w_router)
    br = jnp.zeros((1, LANES), F32).at[0, :N_GROUPS].set(b_group).at[0, N_GROUPS:n_logit].set(b_router)
    tm = _tile(n_tok, 512)
    assert n_tok // tm >= 2, "the token-major write pipeline needs at least two grid steps"
    row = jnp.arange(tm, dtype=jnp.int32)
    tri = (row[None, :] < row[:, None]).astype(BF16)
    route, route_t, cnt, ht = pl.pallas_call(
        _router_kernel,
        grid=(n_tok // tm,),
        in_specs=[
            pl.BlockSpec((tm, dm), lambda i: (i, 0)),
            pl.BlockSpec((1, dm), lambda i: (0, 0)),
            pl.BlockSpec((dm, LANES), lambda i: (0, 0)),
            pl.BlockSpec((1, LANES), lambda i: (0, 0)),
            pl.BlockSpec((tm, tm), lambda i: (0, 0)),
        ],
        out_specs=[pl.BlockSpec((tm, LANES), lambda i: (i, 0)),
                   pl.BlockSpec((ROUTE_ROWS, tm), lambda i: (0, i)),
                   pl.BlockSpec((1, LANES), lambda i: (0, 0)),
                   pl.BlockSpec(memory_space=pl.ANY)],
        out_shape=[jax.ShapeDtypeStruct((n_tok, LANES), F32),
                   jax.ShapeDtypeStruct((ROUTE_ROWS, n_tok), F32),
                   jax.ShapeDtypeStruct((1, LANES), F32),
                   jax.ShapeDtypeStruct((n_tok, dm // LANES, LANES), F32)],
        scratch_shapes=[pltpu.VMEM((1, LANES), F32),
                        pltpu.VMEM((2, tm, dm), F32),
                        pltpu.SemaphoreType.DMA((2,))],
        compiler_params=pltpu.CompilerParams(
            dimension_semantics=("arbitrary",), vmem_limit_bytes=VMEM_LIMIT),
        name="router",
    )(x2, g.reshape(1, dm), wr.astype(BF16), br, tri)
    return route, route_t, cnt[0, N_GROUPS:n_logit].astype(jnp.int32), ht


def _row_copy(src_tm, idx, dst, r8, s, sem):
    return pltpu.make_async_copy(src_tm.at[idx], dst.at[r8, :, s, :], sem)


def _issue_rows(idx_ref, idx_base, src_tm, dst, sem, n_rows):
    for r in range(n_rows):
        _row_copy(src_tm, idx_ref[idx_base + r], dst, r // SUBLANES, r % SUBLANES, sem).start()


def _wait_rows(dst, sem):
    pltpu.make_async_copy(dst, dst, sem).wait()


CAST_STEPS = 4
MODE_IDLE, MODE_COMPUTE, MODE_ZERO = 0, 1, 2


def _lookup(table, idx):
    pos = jnp.arange(table.shape[0], dtype=jnp.int32)
    return jnp.sum(jnp.where(idx[..., None] == pos, table, 0), axis=-1)


def _expert_schedule(counts, tmb, n_blocks):
    cs = CAST_STEPS
    n_steps = cs + n_blocks + N_EXPERTS * (cs - 1)
    nblk = (counts + tmb - 1) // tmb
    used = nblk > 0
    steps_e = jnp.where(used, jnp.maximum(nblk, cs), 0)
    step_end = cs + jnp.cumsum(steps_e)
    step_start = step_end - steps_e
    blk_start = jnp.cumsum(nblk) - nblk
    n_used_blk = jnp.sum(nblk)
    total = step_end[-1]
    eidx = jnp.arange(N_EXPERTS, dtype=jnp.int32)
    next_ge = lax.cummin(jnp.where(used, eidx, N_EXPERTS), reverse=True)
    next_gt = jnp.concatenate([next_ge[1:], jnp.full((1,), N_EXPERTS, jnp.int32)])
    seg_ord = jnp.cumsum(used.astype(jnp.int32)) - used.astype(jnp.int32)

    last_used = jnp.max(jnp.where(used, eidx, 0))

    i = jnp.arange(n_steps + 1, dtype=jnp.int32)
    e_i = jnp.minimum(jnp.sum(step_end[None, :] <= i[:, None], axis=1), N_EXPERTS - 1)
    at_e = lambda table: _lookup(table, e_i)
    p = i - at_e(step_start)
    warm = i < cs
    in_seg = jnp.logical_and(~warm, i < total)
    compute = jnp.logical_and(in_seg, p < at_e(nblk))
    zidx = i - total
    zero = (i >= total) & (i < n_steps) & (n_used_blk + zidx < n_blocks)
    mode = jnp.where(compute, MODE_COMPUTE, jnp.where(zero, MODE_ZERO, MODE_IDLE))
    blk = jnp.where(compute, at_e(blk_start) + p, jnp.where(zero, n_used_blk + zidx, 0))
    par = jnp.where(in_seg, at_e(seg_ord) % 2, 1)
    nxt = at_e(next_gt)
    has_next = jnp.logical_and(in_seg, nxt < N_EXPERTS)
    cast = warm | (has_next & (p < cs))
    cexp = jnp.where(warm, next_ge[0], jnp.where(has_next, nxt, last_used))
    cidx = jnp.where(warm, i, jnp.where(has_next, jnp.minimum(p, cs - 1), cs - 1))
    as_i32 = lambda a: a.astype(jnp.int32)
    return n_steps, tuple(map(as_i32, (mode, blk, par, cexp, cidx, cast)))


def _experts_kernel(mode_ref, blk_ref, par_ref, cexp_ref, cidx_ref, cast_ref, row_tok_ref,
                    xt_hbm, w1f_ref, w3f_ref, w2f_ref, yb_hbm,
                    xbuf, h_scr, obuf, w1a, w3a, w2a, w1b, w3b, w2b, sem, osem, *, tmb):
    del cexp_ref
    i = pl.program_id(0)
    last = pl.num_programs(0) - 1
    slot = i % 2
    nxt = (i + 1) % 2
    n_chunks = xt_hbm.shape[1]
    mode = mode_ref[i]
    par = par_ref[i]
    weights = ((w1a, w3a, w2a), (w1b, w3b, w2b))

    def wait_out(s):
        for cp in _tm_write_copies(obuf.at[s], yb_hbm, 0, osem.at[s]):
            cp.wait()

    @pl.when(jnp.logical_and(i >= 2, mode_ref[jnp.maximum(i - 2, 0)] != MODE_IDLE))
    def _():
        wait_out(slot)

    @pl.when(mode_ref[i + 1] == MODE_COMPUTE)
    def _():
        _issue_rows(row_tok_ref, blk_ref[i + 1] * tmb, xt_hbm, xbuf.at[nxt], sem.at[nxt], tmb)

    for v in range(2):
        @pl.when(jnp.logical_and(cast_ref[i] == 1, par == v))
        def _(v=v):
            w1n, w3n, w2n = weights[1 - v]
            r13, r2 = w1f_ref.shape[0], w2f_ref.shape[0]
            c = cidx_ref[i]
            w1n[pl.ds(pl.multiple_of(c * r13, r13), r13), :] = w1f_ref[...].astype(BF16)
            w3n[pl.ds(pl.multiple_of(c * r13, r13), r13), :] = w3f_ref[...].astype(BF16)
            w2n[pl.ds(pl.multiple_of(c * r2, r2), r2), :] = w2f_ref[...].astype(BF16)

    for v in range(2):
        @pl.when(jnp.logical_and(mode == MODE_COMPUTE, par == v))
        def _(v=v):
            w1c, w3c, w2c = weights[v]
            _wait_rows(xbuf.at[slot], sem.at[slot])
            for c in range(n_chunks):
                h_scr[:, c * LANES:(c + 1) * LANES] = _dense_chunk(xbuf, (slot,), c).astype(BF16)
            h = h_scr[...]
            a = jnp.dot(h, w1c[...], preferred_element_type=F32)
            b = jnp.dot(h, w3c[...], preferred_element_type=F32)
            hid = (a * jax.nn.sigmoid(a) * b).astype(BF16)
            obuf[slot] = jnp.dot(hid, w2c[...], preferred_element_type=F32)

    @pl.when(mode == MODE_ZERO)
    def _():
        obuf[slot] = jnp.zeros(obuf.shape[1:], F32)

    @pl.when(mode != MODE_IDLE)
    def _():
        for cp in _tm_write_copies(obuf.at[slot], yb_hbm, blk_ref[i] * tmb, osem.at[slot]):
            cp.start()

    @pl.when(i == last)
    def _():
        @pl.when(mode != MODE_IDLE)
        def _():
            wait_out(slot)

        @pl.when(jnp.logical_and(i >= 1, mode_ref[jnp.maximum(i - 1, 0)] != MODE_IDLE))
        def _():
            wait_out(nxt)


def _experts(xt, w1, w3, w2, counts, row_tok, tmb, n_blocks, layer):
    n_tok, n_chunks, _ = xt.shape
    dm = n_chunks * LANES
    de = w1.shape[3]
    assert dm % CAST_STEPS == 0 and de % CAST_STEPS == 0
    n_steps, sched = _expert_schedule(counts, tmb, n_blocks)
    wmap = lambda i, mode, blk, par, cexp, cidx, cast, rt: (layer, cexp[i], cidx[i], 0)
    bf16_slot = [pltpu.VMEM((dm, de), BF16), pltpu.VMEM((dm, de), BF16), pltpu.VMEM((de, dm), BF16)]
    return pl.pallas_call(
        functools.partial(_experts_kernel, tmb=tmb),
        grid_spec=pltpu.PrefetchScalarGridSpec(
            num_scalar_prefetch=7,
            grid=(n_steps,),
            in_specs=[
                pl.BlockSpec(memory_space=pl.ANY),
                pl.BlockSpec((None, None, dm // CAST_STEPS, de), wmap),
                pl.BlockSpec((None, None, dm // CAST_STEPS, de), wmap),
                pl.BlockSpec((None, None, de // CAST_STEPS, dm), wmap),
            ],
            out_specs=pl.BlockSpec(memory_space=pl.ANY),
            scratch_shapes=[
                pltpu.VMEM((2, tmb // SUBLANES, n_chunks, SUBLANES, LANES), F32),
                pltpu.VMEM((tmb, dm), BF16),
                pltpu.VMEM((2, tmb, dm), F32),
                *bf16_slot, *bf16_slot,
                pltpu.SemaphoreType.DMA((2,)), pltpu.SemaphoreType.DMA((2,))],
        ),
        out_shape=jax.ShapeDtypeStruct((n_blocks * tmb, n_chunks, LANES), F32),
        compiler_params=pltpu.CompilerParams(
            dimension_semantics=("arbitrary",), vmem_limit_bytes=VMEM_LIMIT),
        name="experts",
    )(*sched, row_tok, xt, w1, w3, w2)


def _combine_kernel(dest_ref, x_ref, route_ref, gf_ref, yb_hbm, o_ref, buf, sem, *, tm, n_tok, final):
    i = pl.program_id(0)
    n_chunks = yb_hbm.shape[1]
    dm = n_chunks * LANES

    def issue(step, slot):
        for k in range(TOP_K):
            _issue_rows(dest_ref, k * n_tok + step * tm, yb_hbm, buf.at[slot, k], sem.at[slot, k], tm)

    @pl.when(i == 0)
    def _():
        issue(0, 0)

    @pl.when(i + 1 < pl.num_programs(0))
    def _():
        issue(i + 1, (i + 1) % 2)

    slot = i % 2
    for k in range(TOP_K):
        _wait_rows(buf.at[slot, k], sem.at[slot, k])
    gate0 = route_ref[:, 2:3]
    gate1 = route_ref[:, 3:4]
    ss = jnp.zeros((tm, 1), F32)
    for c in range(n_chunks):
        cs = slice(c * LANES, (c + 1) * LANES)
        oc = (x_ref[:, cs] + gate0 * _dense_chunk(buf, (slot, 0), c)
              + gate1 * _dense_chunk(buf, (slot, 1), c))
        o_ref[:, cs] = oc
        if final:
            ss = ss + jnp.sum(oc * oc, axis=-1, keepdims=True)
    if final:
        rs = lax.rsqrt(ss * (1.0 / dm) + EPS)
        for c in range(n_chunks):
            cs = slice(c * LANES, (c + 1) * LANES)
            o_ref[:, cs] = o_ref[:, cs] * rs * gf_ref[:, cs]


def _combine(x2, route, yb, dest, gf, final):
    n_tok, dm = x2.shape
    n_chunks = dm // LANES
    tm = _tile(n_tok, 128)
    return pl.pallas_call(
        functools.partial(_combine_kernel, tm=tm, n_tok=n_tok, final=final),
        grid_spec=pltpu.PrefetchScalarGridSpec(
            num_scalar_prefetch=1,
            grid=(n_tok // tm,),
            in_specs=[
                pl.BlockSpec((tm, dm), lambda i, d: (i, 0)),
                pl.BlockSpec((tm, LANES), lambda i, d: (i, 0)),
                pl.BlockSpec((1, dm), lambda i, d: (0, 0)),
                pl.BlockSpec(memory_space=pl.ANY),
            ],
            out_specs=pl.BlockSpec((tm, dm), lambda i, d: (i, 0)),
            scratch_shapes=[pltpu.VMEM((2, TOP_K, tm // SUBLANES, n_chunks, SUBLANES, LANES), F32),
                            pltpu.SemaphoreType.DMA((2, TOP_K))],
        ),
        out_shape=jax.ShapeDtypeStruct((n_tok, dm), F32),
        compiler_params=pltpu.CompilerParams(
            dimension_semantics=("arbitrary",), vmem_limit_bytes=VMEM_LIMIT),
        name="combine",
    )(dest, x2, route, gf.reshape(1, dm), yb)


SCATTER_UNROLL = 32


def _row_tok_kernel(dest_ref, pad_hbm, row_tok_hbm, rt_smem, sem, *, n_tok):
    fill = pltpu.make_async_copy(pad_hbm, rt_smem, sem)
    fill.start()
    fill.wait()
    for k in range(TOP_K):
        def scatter(j, carry):
            for u in range(SCATTER_UNROLL):
                t = j * SCATTER_UNROLL + u
                rt_smem[dest_ref[k * n_tok + t]] = t
            return carry

        lax.fori_loop(0, n_tok // SCATTER_UNROLL, scatter, 0)
    out = pltpu.make_async_copy(rt_smem, row_tok_hbm, sem)
    out.start()
    out.wait()


def _row_tok(dest_kmajor, n_tok, n_rows):
    assert n_tok % SCATTER_UNROLL == 0
    pad_tok = jnp.arange(n_rows, dtype=jnp.int32) % n_tok
    return pl.pallas_call(
        functools.partial(_row_tok_kernel, n_tok=n_tok),
        in_specs=[pl.BlockSpec(memory_space=pltpu.SMEM), pl.BlockSpec(memory_space=pl.ANY)],
        out_specs=pl.BlockSpec(memory_space=pl.ANY),
        out_shape=jax.ShapeDtypeStruct((n_rows,), jnp.int32),
        scratch_shapes=[pltpu.SMEM((n_rows,), jnp.int32), pltpu.SemaphoreType.DMA],
        name="row_tok",
    )(dest_kmajor, pad_tok)


def _plan(route_t, counts, tmb, n_blocks):
    n_tok = route_t.shape[1]
    e = route_t[0:TOP_K].astype(jnp.int32)
    rank = route_t[4:4 + TOP_K].astype(jnp.int32)
    pcounts = (counts + tmb - 1) // tmb * tmb
    pends = jnp.cumsum(pcounts)
    pstarts = pends - pcounts
    dest_kmajor = (_lookup(pstarts, e) + rank).reshape(n_tok * TOP_K)
    row_tok = _row_tok(dest_kmajor, n_tok, n_blocks * tmb)
    return dest_kmajor, row_tok


def kernel(x, norm_mix_g, w_in, conv_w, w_out, norm_ffn_g, w_group, b_group, w_router, b_router,
           w_expert_gate, w_expert_up, w_expert_down, final_norm_g):
    bsz, seq, dm = x.shape
    depth = w_in.shape[0]
    n_tok = bsz * seq
    tmb = _tile(n_tok * TOP_K, 256)
    n_blocks = n_tok * TOP_K // tmb + N_EXPERTS
    x2 = x.reshape(n_tok, dm)
    w_in_b, w_out_b = w_in.astype(BF16), w_out.astype(BF16)
    for l in range(depth):
        z = _in_proj(x2, norm_mix_g[l], w_in_b, l)
        x2 = _mixer(z, x2, w_out_b, conv_w[l], bsz, seq, l)
        route, route_t, counts, ht = _router(x2, norm_ffn_g[l], w_group[l], b_group[l], w_router[l],
                                             b_router[l])
        dest, row_tok = _plan(route_t, counts, tmb, n_blocks)
        yb = _experts(ht, w_expert_gate, w_expert_up, w_expert_down, counts, row_tok, tmb, n_blocks, l)
        x2 = _combine(x2, route, yb, dest, final_norm_g, final=(l == depth - 1))
    return x2.reshape(bsz, seq, dm)
```

```python
import functools

import jax
import jax.numpy as jnp
from jax import lax
from jax.experimental import pallas as pl
from jax.experimental.pallas import tpu as pltpu

CHUNK = 64
RET_HEAD_DIM = 128
CONV_WIDTH = 3
ROPE_BASE = 10000.0
N_GROUPS = 4
EXPERTS_PER_GROUP = 8
N_EXPERTS = N_GROUPS * EXPERTS_PER_GROUP
TOP_K = 2
EPS = 1e-6

LANES = 128
SUBLANES = 8
VMEM_LIMIT = 56 * 1024 * 1024

IN_PROJ_ROWS, IN_PROJ_COLS = 1024, 1792
MIXER_ROWS = 256
ROUTER_ROWS = 512
EXPERT_ROWS = 256
COMBINE_ROWS = 256

F32 = jnp.float32
BF16 = jnp.bfloat16


def _tile(n, want):
    t = min(n, want)
    while n % t:
        t //= 2
    return t


def _rms(x, g):
    return x * lax.rsqrt(jnp.mean(x * x, axis=-1, keepdims=True) + EPS) * g


def _tm_write_copies(src_dense, dst_tm_hbm, row0, sem):
    rows = src_dense.shape[0]
    return [pltpu.make_async_copy(src_dense.at[:, pl.ds(c * LANES, LANES)],
                                  dst_tm_hbm.at[pl.ds(row0, rows), c, :], sem)
            for c in range(dst_tm_hbm.shape[1])]


def _dense_chunk(buf, lead, c):
    v = buf[(*lead, slice(None), c)]
    return v.reshape(v.shape[0] * SUBLANES, LANES)


def _in_proj_kernel(x_ref, g_ref, w_ref, o_ref, h_scr):
    @pl.when(pl.program_id(1) == 0)
    def _():
        h_scr[...] = _rms(x_ref[...], g_ref[...]).astype(BF16)

    o_ref[...] = jnp.dot(h_scr[...], w_ref[...], preferred_element_type=F32).astype(o_ref.dtype)


def _in_proj(x2, g, w_bf16, layer):
    n_tok, dm = x2.shape
    n_out = w_bf16.shape[2]
    tm = _tile(n_tok, IN_PROJ_ROWS)
    tn = _tile(n_out, IN_PROJ_COLS)
    return pl.pallas_call(
        _in_proj_kernel,
        grid=(n_tok // tm, n_out // tn),
        in_specs=[
            pl.BlockSpec((tm, dm), lambda i, j: (i, 0)),
            pl.BlockSpec((1, dm), lambda i, j: (0, 0)),
            pl.BlockSpec((None, dm, tn), lambda i, j: (layer, 0, j)),
        ],
        out_specs=pl.BlockSpec((tm, tn), lambda i, j: (i, j)),
        out_shape=jax.ShapeDtypeStruct((n_tok, n_out), BF16),
        scratch_shapes=[pltpu.VMEM((tm, dm), BF16)],
        compiler_params=pltpu.CompilerParams(
            dimension_semantics=("parallel", "arbitrary"), vmem_limit_bytes=VMEM_LIMIT),
        name="in_proj",
    )(x2, g.reshape(1, dm), w_bf16)


def _mixer_kernel(z_ref, x_ref, wout_ref, convw_ref, cos_ref, sin_ref, dmask_ref, qdec_ref,
                  kdec_ref, sdec_ref, o_ref, state_scr, u_scr, mixed_scr, *, ts, d_conv, n_heads):
    dh = RET_HEAD_DIM
    d_ret = n_heads * dh

    @pl.when(pl.program_id(1) == 0)
    def _():
        state_scr[...] = jnp.zeros_like(state_scr)
        u_scr[0:SUBLANES, :] = jnp.zeros((SUBLANES, d_conv), F32)

    zf = lambda lo, hi: z_ref[:, lo:hi].astype(F32)
    u_scr[SUBLANES:SUBLANES + ts, :] = zf(d_conv, 2 * d_conv) * zf(2 * d_conv, 3 * d_conv)
    y = (convw_ref[2:3, :] * u_scr[SUBLANES:SUBLANES + ts, :]
         + convw_ref[1:2, :] * u_scr[SUBLANES - 1:SUBLANES - 1 + ts, :]
         + convw_ref[0:1, :] * u_scr[SUBLANES - 2:SUBLANES - 2 + ts, :])
    mixed_scr[:, 0:d_conv] = (zf(0, d_conv) * y).astype(BF16)
    u_scr[0:SUBLANES, :] = u_scr[ts:ts + SUBLANES, :]
    o_ref[...] = x_ref[...] + jnp.dot(mixed_scr[:, 0:d_conv], wout_ref[0:d_conv, :],
                                      preferred_element_type=F32)

    cos = cos_ref[...]
    sin = sin_ref[...]
    scale = RET_HEAD_DIM ** -0.5
    base = 3 * d_conv
    for h in range(n_heads):
        c0 = h * dh
        q = zf(base + c0, base + c0 + dh)
        k = zf(base + d_ret + c0, base + d_ret + c0 + dh)
        v = z_ref[:, base + 2 * d_ret + c0:base + 2 * d_ret + c0 + dh]
        g = zf(base + 3 * d_ret + c0, base + 3 * d_ret + c0 + dh)
        qr = q * cos + pltpu.roll(q, dh // 2, 1) * sin
        kr = (k * cos + pltpu.roll(k, dh // 2, 1) * sin) * scale
        s = lax.dot_general(qr.astype(BF16), kr.astype(BF16), (((1,), (1,)), ((), ())),
                            preferred_element_type=F32) * dmask_ref[h]
        o = jnp.dot(s.astype(BF16), v, preferred_element_type=F32)
        st = state_scr[h]
        o = o + jnp.dot((qr * qdec_ref[:, c0:c0 + dh]).astype(BF16), st.astype(BF16),
                        preferred_element_type=F32)
        kv = lax.dot_general((kr * kdec_ref[:, c0:c0 + dh]).astype(BF16), v,
                             (((0,), (0,)), ((), ())), preferred_element_type=F32)
        state_scr[h] = st * sdec_ref[:, c0:c0 + dh] + kv
        on = o * lax.rsqrt(jnp.mean(o * o, axis=-1, keepdims=True) + EPS)
        mixed_scr[:, d_conv + c0:d_conv + c0 + dh] = (on * (g * jax.nn.sigmoid(g))).astype(BF16)

    o_ref[...] += jnp.dot(mixed_scr[:, d_conv:], wout_ref[d_conv:, :], preferred_element_type=F32)


def _retention_tables(seq, ts, n_heads):
    dh = RET_HEAD_DIM
    half = dh // 2
    pos = jnp.arange(seq, dtype=F32)
    inv = ROPE_BASE ** (-jnp.arange(half, dtype=F32) / half)
    ang = pos[:, None] * inv[None, :]
    cos = jnp.cos(ang)
    sin = jnp.sin(ang)
    cos_full = jnp.concatenate([cos, cos], axis=-1)
    sin_signed = jnp.concatenate([-sin, sin], axis=-1)
    log_g = jnp.log1p(-jnp.exp2(-5.0 - jnp.arange(n_heads, dtype=F32)))
    idx = jnp.arange(ts, dtype=F32)
    dist = jnp.abs(idx[:, None] - idx[None, :])
    chunk_id = jnp.arange(ts) // CHUNK
    visible = chunk_id[None, :] <= chunk_id[:, None]
    dmask = jnp.where(visible[None], jnp.exp(log_g[:, None, None] * dist[None]), 0.0)
    rep = lambda a: jnp.repeat(a, dh, axis=-1)
    qdec = rep(jnp.exp(log_g[None, :] * (idx[:, None] + 1.0)))
    kdec = rep(jnp.exp(log_g[None, :] * (ts - 1.0 - idx[:, None])))
    sdec = rep(jnp.exp(log_g * ts)[None, :])
    return cos_full, sin_signed, dmask.astype(F32), qdec, kdec, sdec


def _mixer(z, x2, wout_bf16, conv_w, bsz, seq, layer):
    n_tok, dm = x2.shape
    d_conv = conv_w.shape[1]
    d_ret = dm - d_conv
    n_heads = d_ret // RET_HEAD_DIM
    d_in = z.shape[1]
    ts = _tile(seq, MIXER_ROWS)
    ns = seq // ts
    cos_full, sin_signed, dmask, qdec, kdec, sdec = _retention_tables(seq, ts, n_heads)
    kern = functools.partial(_mixer_kernel, ts=ts, d_conv=d_conv, n_heads=n_heads)
    const2 = lambda b, s: (0, 0)
    return pl.pallas_call(
        kern,
        grid=(bsz, ns),
        in_specs=[
            pl.BlockSpec((ts, d_in), lambda b, s: (b * ns + s, 0)),
            pl.BlockSpec((ts, dm), lambda b, s: (b * ns + s, 0)),
            pl.BlockSpec((None, dm, dm), lambda b, s: (layer, 0, 0), pipeline_mode=pl.Buffered(1)),
            pl.BlockSpec((CONV_WIDTH, d_conv), const2),
            pl.BlockSpec((ts, RET_HEAD_DIM), lambda b, s: (s, 0)),
            pl.BlockSpec((ts, RET_HEAD_DIM), lambda b, s: (s, 0)),
            pl.BlockSpec((n_heads, ts, ts), lambda b, s: (0, 0, 0)),
            pl.BlockSpec((ts, d_ret), const2),
            pl.BlockSpec((ts, d_ret), const2),
            pl.BlockSpec((1, d_ret), const2),
        ],
        out_specs=pl.BlockSpec((ts, dm), lambda b, s: (b * ns + s, 0)),
        out_shape=jax.ShapeDtypeStruct((n_tok, dm), F32),
        scratch_shapes=[
            pltpu.VMEM((n_heads, RET_HEAD_DIM, RET_HEAD_DIM), F32),
            pltpu.VMEM((ts + SUBLANES, d_conv), F32),
            pltpu.VMEM((ts, dm), BF16),
        ],
        compiler_params=pltpu.CompilerParams(
            dimension_semantics=("parallel", "arbitrary"), vmem_limit_bytes=VMEM_LIMIT),
        name="mixer",
    )(z, x2, wout_bf16, conv_w, cos_full, sin_signed, dmask, qdec, kdec, sdec)


ROUTE_ROWS = SUBLANES


def _router_kernel(x_ref, g_ref, wr_ref, br_ref, tri_ref, route_ref, route_t_ref, cnt_ref, ht_hbm,
                   cnt_scr, tbuf, tsem):
    step = pl.program_id(0)
    last = pl.num_programs(0) - 1
    slot = step % 2
    tm = x_ref.shape[0]

    @pl.when(step >= 2)
    def _():
        for cp in _tm_write_copies(tbuf.at[slot], ht_hbm, 0, tsem.at[slot]):
            cp.wait()

    @pl.when(step == 0)
    def _():
        cnt_scr[...] = jnp.zeros_like(cnt_scr)

    tbuf[slot] = _rms(x_ref[...], g_ref[...])
    for cp in _tm_write_copies(tbuf.at[slot], ht_hbm, step * tm, tsem.at[slot]):
        cp.start()
    h = tbuf[slot].astype(BF16)
    logits = jnp.dot(h, wr_ref[...], preferred_element_type=F32) + br_ref[...]
    lane = lax.broadcasted_iota(jnp.int32, logits.shape, 1)
    neg = jnp.float32(-jnp.inf)
    big = jnp.int32(LANES)

    gl = jnp.where(lane < N_GROUPS, logits, neg)
    gmax = jnp.max(gl, axis=-1, keepdims=True)
    g_sel = jnp.min(jnp.where(gl == gmax, lane, big), axis=-1, keepdims=True)
    p_g = 1.0 / jnp.sum(jnp.exp(gl - gmax), axis=-1, keepdims=True)

    lo = N_GROUPS + g_sel * EXPERTS_PER_GROUP
    el = jnp.where(lane >= lo, jnp.where(lane < lo + EXPERTS_PER_GROUP, logits, neg), neg)
    v1 = jnp.max(el, axis=-1, keepdims=True)
    i1 = jnp.min(jnp.where(el == v1, lane, big), axis=-1, keepdims=True)
    el2 = jnp.where(lane == i1, neg, el)
    v2 = jnp.max(el2, axis=-1, keepdims=True)
    i2 = jnp.min(jnp.where(el2 == v2, lane, big), axis=-1, keepdims=True)
    t = jnp.exp(v2 - v1)
    gate1 = p_g / (1.0 + t)
    gate2 = p_g * t / (1.0 + t)

    tri = tri_ref[...]
    cnt = cnt_scr[...]
    hot1 = jnp.where(lane == i1, 1.0, 0.0)
    hot2 = jnp.where(lane == i2, 1.0, 0.0)
    before1 = jnp.dot(tri, hot1.astype(BF16), preferred_element_type=F32) + cnt
    cnt = cnt + jnp.sum(hot1, axis=0, keepdims=True)
    before2 = jnp.dot(tri, hot2.astype(BF16), preferred_element_type=F32) + cnt
    cnt = cnt + jnp.sum(hot2, axis=0, keepdims=True)
    rank1 = jnp.sum(hot1 * before1, axis=-1, keepdims=True)
    rank2 = jnp.sum(hot2 * before2, axis=-1, keepdims=True)
    cnt_scr[...] = cnt
    cnt_ref[...] = cnt

    e1 = (i1 - N_GROUPS).astype(F32)
    e2 = (i2 - N_GROUPS).astype(F32)
    cols = (e1, e2, gate1, gate2, rank1, rank2)
    route = jnp.zeros(logits.shape, F32)
    for j, col in enumerate(cols):
        route = jnp.where(lane == j, col, route)
    route_ref[...] = route
    route_t_ref[...] = jnp.transpose(route)[0:ROUTE_ROWS, :]

    @pl.when(step == last)
    def _():
        for s in range(2):
            for cp in _tm_write_copies(tbuf.at[s], ht_hbm, 0, tsem.at[s]):
                cp.wait()


def _router(x2, g, w_group, b_group, w_router, b_router):
    n_tok, dm = x2.shape
    n_logit = N_GROUPS + N_EXPERTS
    wr = jnp.zeros((dm, LANES), F32).at[:, :N_GROUPS].set(w_group).at[:, N_GROUPS:n_logit].set(w_router)
    br = jnp.zeros((1, LANES), F32).at[0, :N_GROUPS].set(b_group).at[0, N_GROUPS:n_logit].set(b_router)
    tm = _tile(n_tok, ROUTER_ROWS)
    assert n_tok // tm >= 2, "the token-major write pipeline needs at least two grid steps"
    row = jnp.arange(tm, dtype=jnp.int32)
    tri = (row[None, :] < row[:, None]).astype(BF16)
    route, route_t, cnt, ht = pl.pallas_call(
        _router_kernel,
        grid=(n_tok // tm,),
        in_specs=[
            pl.BlockSpec((tm, dm), lambda i: (i, 0)),
            pl.BlockSpec((1, dm), lambda i: (0, 0)),
            pl.BlockSpec((dm, LANES), lambda i: (0, 0)),
            pl.BlockSpec((1, LANES), lambda i: (0, 0)),
            pl.BlockSpec((tm, tm), lambda i: (0, 0)),
        ],
        out_specs=[pl.BlockSpec((tm, LANES), lambda i: (i, 0)),
                   pl.BlockSpec((ROUTE_ROWS, tm), lambda i: (0, i)),
                   pl.BlockSpec((1, LANES), lambda i: (0, 0)),
                   pl.BlockSpec(memory_space=pl.ANY)],
        out_shape=[jax.ShapeDtypeStruct((n_tok, LANES), F32),
                   jax.ShapeDtypeStruct((ROUTE_ROWS, n_tok), F32),
                   jax.ShapeDtypeStruct((1, LANES), F32),
                   jax.ShapeDtypeStruct((n_tok, dm // LANES, LANES), F32)],
        scratch_shapes=[pltpu.VMEM((1, LANES), F32),
                        pltpu.VMEM((2, tm, dm), F32),
                        pltpu.SemaphoreType.DMA((2,))],
        compiler_params=pltpu.CompilerParams(
            dimension_semantics=("arbitrary",), vmem_limit_bytes=VMEM_LIMIT),
        name="router",
    )(x2, g.reshape(1, dm), wr.astype(BF16), br, tri)
    return route, route_t, cnt[0, N_GROUPS:n_logit].astype(jnp.int32), ht


def _row_copy(src_tm, idx, dst, r8, s, sem):
    return pltpu.make_async_copy(src_tm.at[idx], dst.at[r8, :, s, :], sem)


def _issue_rows(idx_ref, idx_base, src_tm, dst, sem, n_rows):
    for r in range(n_rows):
        _row_copy(src_tm, idx_ref[idx_base + r], dst, r // SUBLANES, r % SUBLANES, sem).start()


def _wait_rows(dst, sem):
    pltpu.make_async_copy(dst, dst, sem).wait()


CAST_STEPS = 4
MODE_IDLE, MODE_COMPUTE, MODE_ZERO = 0, 1, 2


def _lookup(table, idx):
    pos = jnp.arange(table.shape[0], dtype=jnp.int32)
    return jnp.sum(jnp.where(idx[..., None] == pos, table, 0), axis=-1)


def _expert_schedule(counts, tmb, n_blocks):
    cs = CAST_STEPS
    n_steps = cs + n_blocks + N_EXPERTS * (cs - 1)
    nblk = (counts + tmb - 1) // tmb
    used = nblk > 0
    steps_e = jnp.where(used, jnp.maximum(nblk, cs), 0)
    step_end = cs + jnp.cumsum(steps_e)
    step_start = step_end - steps_e
    blk_start = jnp.cumsum(nblk) - nblk
    n_used_blk = jnp.sum(nblk)
    total = step_end[-1]
    eidx = jnp.arange(N_EXPERTS, dtype=jnp.int32)
    next_ge = lax.cummin(jnp.where(used, eidx, N_EXPERTS), reverse=True)
    next_gt = jnp.concatenate([next_ge[1:], jnp.full((1,), N_EXPERTS, jnp.int32)])
    seg_ord = jnp.cumsum(used.astype(jnp.int32)) - used.astype(jnp.int32)

    last_used = jnp.max(jnp.where(used, eidx, 0))

    i = jnp.arange(n_steps + 1, dtype=jnp.int32)
    e_i = jnp.minimum(jnp.sum(step_end[None, :] <= i[:, None], axis=1), N_EXPERTS - 1)
    at_e = lambda table: _lookup(table, e_i)
    p = i - at_e(step_start)
    warm = i < cs
    in_seg = jnp.logical_and(~warm, i < total)
    compute = jnp.logical_and(in_seg, p < at_e(nblk))
    zidx = i - total
    zero = (i >= total) & (i < n_steps) & (n_used_blk + zidx < n_blocks)
    mode = jnp.where(compute, MODE_COMPUTE, jnp.where(zero, MODE_ZERO, MODE_IDLE))
    blk = jnp.where(compute, at_e(blk_start) + p, jnp.where(zero, n_used_blk + zidx, 0))
    par = jnp.where(in_seg, at_e(seg_ord) % 2, 1)
    nxt = at_e(next_gt)
    has_next = jnp.logical_and(in_seg, nxt < N_EXPERTS)
    cast = warm | (has_next & (p < cs))
    cexp = jnp.where(warm, next_ge[0], jnp.where(has_next, nxt, last_used))
    cidx = jnp.where(warm, i, jnp.where(has_next, jnp.minimum(p, cs - 1), cs - 1))
    as_i32 = lambda a: a.astype(jnp.int32)
    return n_steps, tuple(map(as_i32, (mode, blk, par, cexp, cidx, cast)))


def _experts_kernel(mode_ref, blk_ref, par_ref, cexp_ref, cidx_ref, cast_ref, row_tok_ref,
                    xt_hbm, w1f_ref, w3f_ref, w2f_ref, yb_hbm,
                    xbuf, h_scr, obuf, w1a, w3a, w2a, w1b, w3b, w2b, sem, osem, *, tmb):
    del cexp_ref
    i = pl.program_id(0)
    last = pl.num_programs(0) - 1
    slot = i % 2
    nxt = (i + 1) % 2
    n_chunks = xt_hbm.shape[1]
    mode = mode_ref[i]
    par = par_ref[i]
    weights = ((w1a, w3a, w2a), (w1b, w3b, w2b))

    def wait_out(s):
        for cp in _tm_write_copies(obuf.at[s], yb_hbm, 0, osem.at[s]):
            cp.wait()

    @pl.when(jnp.logical_and(i >= 2, mode_ref[jnp.maximum(i - 2, 0)] != MODE_IDLE))
    def _():
        wait_out(slot)

    @pl.when(mode_ref[i + 1] == MODE_COMPUTE)
    def _():
        _issue_rows(row_tok_ref, blk_ref[i + 1] * tmb, xt_hbm, xbuf.at[nxt], sem.at[nxt], tmb)

    for v in range(2):
        @pl.when(jnp.logical_and(cast_ref[i] == 1, par == v))
        def _(v=v):
            w1n, w3n, w2n = weights[1 - v]
            r13, r2 = w1f_ref.shape[0], w2f_ref.shape[0]
            c = cidx_ref[i]
            w1n[pl.ds(pl.multiple_of(c * r13, r13), r13), :] = w1f_ref[...].astype(BF16)
            w3n[pl.ds(pl.multiple_of(c * r13, r13), r13), :] = w3f_ref[...].astype(BF16)
            w2n[pl.ds(pl.multiple_of(c * r2, r2), r2), :] = w2f_ref[...].astype(BF16)

    for v in range(2):
        @pl.when(jnp.logical_and(mode == MODE_COMPUTE, par == v))
        def _(v=v):
            w1c, w3c, w2c = weights[v]
            _wait_rows(xbuf.at[slot], sem.at[slot])
            for c in range(n_chunks):
                h_scr[:, c * LANES:(c + 1) * LANES] = _dense_chunk(xbuf, (slot,), c).astype(BF16)
            h = h_scr[...]
            a = jnp.dot(h, w1c[...], preferred_element_type=F32)
            b = jnp.dot(h, w3c[...], preferred_element_type=F32)
            hid = (a * jax.nn.sigmoid(a) * b).astype(BF16)
            obuf[slot] = jnp.dot(hid, w2c[...], preferred_element_type=F32)

    @pl.when(mode == MODE_ZERO)
    def _():
        obuf[slot] = jnp.zeros(obuf.shape[1:], F32)

    @pl.when(mode != MODE_IDLE)
    def _():
        for cp in _tm_write_copies(obuf.at[slot], yb_hbm, blk_ref[i] * tmb, osem.at[slot]):
            cp.start()

    @pl.when(i == last)
    def _():
        @pl.when(mode != MODE_IDLE)
        def _():
            wait_out(slot)

        @pl.when(jnp.logical_and(i >= 1, mode_ref[jnp.maximum(i - 1, 0)] != MODE_IDLE))
        def _():
            wait_out(nxt)


def _experts(xt, w1, w3, w2, counts, row_tok, tmb, n_blocks, layer):
    n_tok, n_chunks, _ = xt.shape
    dm = n_chunks * LANES
    de = w1.shape[3]
    assert dm % CAST_STEPS == 0 and de % CAST_STEPS == 0
    n_steps, sched = _expert_schedule(counts, tmb, n_blocks)
    wmap = lambda i, mode, blk, par, cexp, cidx, cast, rt: (layer, cexp[i], cidx[i], 0)
    bf16_slot = [pltpu.VMEM((dm, de), BF16), pltpu.VMEM((dm, de), BF16), pltpu.VMEM((de, dm), BF16)]
    return pl.pallas_call(
        functools.partial(_experts_kernel, tmb=tmb),
        grid_spec=pltpu.PrefetchScalarGridSpec(
            num_scalar_prefetch=7,
            grid=(n_steps,),
            in_specs=[
                pl.BlockSpec(memory_space=pl.ANY),
                pl.BlockSpec((None, None, dm // CAST_STEPS, de), wmap),
                pl.BlockSpec((None, None, dm // CAST_STEPS, de), wmap),
                pl.BlockSpec((None, None, de // CAST_STEPS, dm), wmap),
            ],
            out_specs=pl.BlockSpec(memory_space=pl.ANY),
            scratch_shapes=[
                pltpu.VMEM((2, tmb // SUBLANES, n_chunks, SUBLANES, LANES), F32),
                pltpu.VMEM((tmb, dm), BF16),
                pltpu.VMEM((2, tmb, dm), F32),
                *bf16_slot, *bf16_slot,
                pltpu.SemaphoreType.DMA((2,)), pltpu.SemaphoreType.DMA((2,))],
        ),
        out_shape=jax.ShapeDtypeStruct((n_blocks * tmb, n_chunks, LANES), F32),
        compiler_params=pltpu.CompilerParams(
            dimension_semantics=("arbitrary",), vmem_limit_bytes=VMEM_LIMIT),
        name="experts",
    )(*sched, row_tok, xt, w1, w3, w2)


def _combine_kernel(dest_ref, x_ref, route_ref, gf_ref, yb_hbm, o_ref, buf, sem, *, tm, n_tok, final):
    i = pl.program_id(0)
    n_chunks = yb_hbm.shape[1]
    dm = n_chunks * LANES

    def issue(step, slot):
        for k in range(TOP_K):
            _issue_rows(dest_ref, k * n_tok + step * tm, yb_hbm, buf.at[slot, k], sem.at[slot, k], tm)

    @pl.when(i == 0)
    def _():
        issue(0, 0)

    @pl.when(i + 1 < pl.num_programs(0))
    def _():
        issue(i + 1, (i + 1) % 2)

    slot = i % 2
    for k in range(TOP_K):
        _wait_rows(buf.at[slot, k], sem.at[slot, k])
    gate0 = route_ref[:, 2:3]
    gate1 = route_ref[:, 3:4]
    ss = jnp.zeros((tm, 1), F32)
    for c in range(n_chunks):
        cs = slice(c * LANES, (c + 1) * LANES)
        oc = (x_ref[:, cs] + gate0 * _dense_chunk(buf, (slot, 0), c)
              + gate1 * _dense_chunk(buf, (slot, 1), c))
        o_ref[:, cs] = oc
        if final:
            ss = ss + jnp.sum(oc * oc, axis=-1, keepdims=True)
    if final:
        rs = lax.rsqrt(ss * (1.0 / dm) + EPS)
        for c in range(n_chunks):
            cs = slice(c * LANES, (c + 1) * LANES)
            o_ref[:, cs] = o_ref[:, cs] * rs * gf_ref[:, cs]


def _combine(x2, route, yb, dest, gf, final):
    n_tok, dm = x2.shape
    n_chunks = dm // LANES
    tm = _tile(n_tok, COMBINE_ROWS)
    return pl.pallas_call(
        functools.partial(_combine_kernel, tm=tm, n_tok=n_tok, final=final),
        grid_spec=pltpu.PrefetchScalarGridSpec(
            num_scalar_prefetch=1,
            grid=(n_tok // tm,),
            in_specs=[
                pl.BlockSpec((tm, dm), lambda i, d: (i, 0)),
                pl.BlockSpec((tm, LANES), lambda i, d: (i, 0)),
                pl.BlockSpec((1, dm), lambda i, d: (0, 0)),
                pl.BlockSpec(memory_space=pl.ANY),
            ],
            out_specs=pl.BlockSpec((tm, dm), lambda i, d: (i, 0)),
            scratch_shapes=[pltpu.VMEM((2, TOP_K, tm // SUBLANES, n_chunks, SUBLANES, LANES), F32),
                            pltpu.SemaphoreType.DMA((2, TOP_K))],
        ),
        out_shape=jax.ShapeDtypeStruct((n_tok, dm), F32),
        compiler_params=pltpu.CompilerParams(
            dimension_semantics=("arbitrary",), vmem_limit_bytes=VMEM_LIMIT),
        name="combine",
    )(dest, x2, route, gf.reshape(1, dm), yb)


SCATTER_UNROLL = 32


def _row_tok_kernel(dest_ref, pad_hbm, row_tok_hbm, rt_smem, sem, *, n_tok):
    fill = pltpu.make_async_copy(pad_hbm, rt_smem, sem)
    fill.start()
    fill.wait()
    for k in range(TOP_K):
        def scatter(j, carry):
            for u in range(SCATTER_UNROLL):
                t = j * SCATTER_UNROLL + u
                rt_smem[dest_ref[k * n_tok + t]] = t
            return carry

        lax.fori_loop(0, n_tok // SCATTER_UNROLL, scatter, 0)
    out = pltpu.make_async_copy(rt_smem, row_tok_hbm, sem)
    out.start()
    out.wait()


def _row_tok(dest_kmajor, n_tok, n_rows):
    assert n_tok % SCATTER_UNROLL == 0
    pad_tok = jnp.arange(n_rows, dtype=jnp.int32) % n_tok
    return pl.pallas_call(
        functools.partial(_row_tok_kernel, n_tok=n_tok),
        in_specs=[pl.BlockSpec(memory_space=pltpu.SMEM), pl.BlockSpec(memory_space=pl.ANY)],
        out_specs=pl.BlockSpec(memory_space=pl.ANY),
        out_shape=jax.ShapeDtypeStruct((n_rows,), jnp.int32),
        scratch_shapes=[pltpu.SMEM((n_rows,), jnp.int32), pltpu.SemaphoreType.DMA],
        name="row_tok",
    )(dest_kmajor, pad_tok)


def _plan(route_t, counts, tmb, n_blocks):
    n_tok = route_t.shape[1]
    e = route_t[0:TOP_K].astype(jnp.int32)
    rank = route_t[4:4 + TOP_K].astype(jnp.int32)
    pcounts = (counts + tmb - 1) // tmb * tmb
    pends = jnp.cumsum(pcounts)
    pstarts = pends - pcounts
    dest_kmajor = (_lookup(pstarts, e) + rank).reshape(n_tok * TOP_K)
    row_tok = _row_tok(dest_kmajor, n_tok, n_blocks * tmb)
    return dest_kmajor, row_tok


def kernel(x, norm_mix_g, w_in, conv_w, w_out, norm_ffn_g, w_group, b_group, w_router, b_router,
           w_expert_gate, w_expert_up, w_expert_down, final_norm_g):
    bsz, seq, dm = x.shape
    depth = w_in.shape[0]
    n_tok = bsz * seq
    tmb = _tile(n_tok * TOP_K, EXPERT_ROWS)
    n_blocks = n_tok * TOP_K // tmb + N_EXPERTS
    x2 = x.reshape(n_tok, dm)
    w_in_b, w_out_b = w_in.astype(BF16), w_out.astype(BF16)
    for l in range(depth):
        z = _in_proj(x2, norm_mix_g[l], w_in_b, l)
        x2 = _mixer(z, x2, w_out_b, conv_w[l], bsz, seq, l)
        route, route_t, counts, ht = _router(x2, norm_ffn_g[l], w_group[l], b_group[l], w_router[l],
                                             b_router[l])
        dest, row_tok = _plan(route_t, counts, tmb, n_blocks)
        yb = _experts(ht, w_expert_gate, w_expert_up, w_expert_down, counts, row_tok, tmb, n_blocks, l)
        x2 = _combine(x2, route, yb, dest, final_norm_g, final=(l == depth - 1))
    return x2.reshape(bsz, seq, dm)
```

```python
import functools

import jax
import jax.numpy as jnp
from jax import lax
from jax.experimental import pallas as pl
from jax.experimental.pallas import tpu as pltpu

CHUNK = 64
RET_HEAD_DIM = 128
CONV_WIDTH = 3
ROPE_BASE = 10000.0
N_GROUPS = 4
EXPERTS_PER_GROUP = 8
N_EXPERTS = N_GROUPS * EXPERTS_PER_GROUP
TOP_K = 2
EPS = 1e-6

LANES = 128
SUBLANES = 8
VMEM_LIMIT = 56 * 1024 * 1024

IN_PROJ_ROWS, IN_PROJ_COLS = 1024, 1792
MIXER_ROWS = 256
ROUTER_ROWS = 512
EXPERT_ROWS = 256
COMBINE_ROWS = 256

F32 = jnp.float32
BF16 = jnp.bfloat16


def _tile(n, want):
    t = min(n, want)
    while n % t:
        t //= 2
    return t


def _rms(x, g):
    return x * lax.rsqrt(jnp.mean(x * x, axis=-1, keepdims=True) + EPS) * g


def _tm_write_copies(src_dense, dst_tm_hbm, row0, sem):
    rows = src_dense.shape[0]
    return [pltpu.make_async_copy(src_dense.at[:, pl.ds(c * LANES, LANES)],
                                  dst_tm_hbm.at[pl.ds(row0, rows), c, :], sem)
            for c in range(dst_tm_hbm.shape[1])]


def _dense_chunk(buf, lead, c):
    v = buf[(*lead, slice(None), c)]
    return v.reshape(v.shape[0] * SUBLANES, LANES)


def _in_proj_kernel(x_ref, g_ref, w_ref, o_ref, h_scr):
    @pl.when(pl.program_id(1) == 0)
    def _():
        h_scr[...] = _rms(x_ref[...], g_ref[...]).astype(BF16)

    o_ref[...] = jnp.dot(h_scr[...], w_ref[...], preferred_element_type=F32).astype(o_ref.dtype)


def _in_proj(x2, g, w_bf16, layer):
    n_tok, dm = x2.shape
    n_out = w_bf16.shape[2]
    tm = _tile(n_tok, IN_PROJ_ROWS)
    tn = _tile(n_out, IN_PROJ_COLS)
    return pl.pallas_call(
        _in_proj_kernel,
        grid=(n_tok // tm, n_out // tn),
        in_specs=[
            pl.BlockSpec((tm, dm), lambda i, j: (i, 0)),
            pl.BlockSpec((1, dm), lambda i, j: (0, 0)),
            pl.BlockSpec((None, dm, tn), lambda i, j: (layer, 0, j)),
        ],
        out_specs=pl.BlockSpec((tm, tn), lambda i, j: (i, j)),
        out_shape=jax.ShapeDtypeStruct((n_tok, n_out), BF16),
        scratch_shapes=[pltpu.VMEM((tm, dm), BF16)],
        compiler_params=pltpu.CompilerParams(
            dimension_semantics=("parallel", "arbitrary"), vmem_limit_bytes=VMEM_LIMIT),
        name="in_proj",
    )(x2, g.reshape(1, dm), w_bf16)


def _mixer_kernel(z_ref, x_ref, wout_ref, convw_ref, cos_ref, sin_ref, dmask_ref, qdec_ref,
                  kdec_ref, sdec_ref, o_ref, state_scr, u_scr, mixed_scr, *, ts, d_conv, n_heads):
    dh = RET_HEAD_DIM
    d_ret = n_heads * dh

    @pl.when(pl.program_id(1) == 0)
    def _():
        state_scr[...] = jnp.zeros_like(state_scr)
        u_scr[0:SUBLANES, :] = jnp.zeros((SUBLANES, d_conv), F32)

    zf = lambda lo, hi: z_ref[:, lo:hi].astype(F32)
    u_scr[SUBLANES:SUBLANES + ts, :] = zf(d_conv, 2 * d_conv) * zf(2 * d_conv, 3 * d_conv)
    y = (convw_ref[2:3, :] * u_scr[SUBLANES:SUBLANES + ts, :]
         + convw_ref[1:2, :] * u_scr[SUBLANES - 1:SUBLANES - 1 + ts, :]
         + convw_ref[0:1, :] * u_scr[SUBLANES - 2:SUBLANES - 2 + ts, :])
    mixed_scr[:, 0:d_conv] = (zf(0, d_conv) * y).astype(BF16)
    u_scr[0:SUBLANES, :] = u_scr[ts:ts + SUBLANES, :]
    o_ref[...] = x_ref[...] + jnp.dot(mixed_scr[:, 0:d_conv], wout_ref[0:d_conv, :],
                                      preferred_element_type=F32)

    cos = cos_ref[...]
    sin = sin_ref[...]
    scale = RET_HEAD_DIM ** -0.5
    base = 3 * d_conv
    for h in range(n_heads):
        c0 = h * dh
        q = zf(base + c0, base + c0 + dh)
        k = zf(base + d_ret + c0, base + d_ret + c0 + dh)
        v = z_ref[:, base + 2 * d_ret + c0:base + 2 * d_ret + c0 + dh]
        g = zf(base + 3 * d_ret + c0, base + 3 * d_ret + c0 + dh)
        qr = q * cos + pltpu.roll(q, dh // 2, 1) * sin
        kr = (k * cos + pltpu.roll(k, dh // 2, 1) * sin) * scale
        s = lax.dot_general(qr.astype(BF16), kr.astype(BF16), (((1,), (1,)), ((), ())),
                            preferred_element_type=F32) * dmask_ref[h]
        o = jnp.dot(s.astype(BF16), v, preferred_element_type=F32)
        st = state_scr[h]
        o = o + jnp.dot((qr * qdec_ref[:, c0:c0 + dh]).astype(BF16), st.astype(BF16),
                        preferred_element_type=F32)
        kv = lax.dot_general((kr * kdec_ref[:, c0:c0 + dh]).astype(BF16), v,
                             (((0,), (0,)), ((), ())), preferred_element_type=F32)
        state_scr[h] = st * sdec_ref[:, c0:c0 + dh] + kv
        on = o * lax.rsqrt(jnp.mean(o * o, axis=-1, keepdims=True) + EPS)
        mixed_scr[:, d_conv + c0:d_conv + c0 + dh] = (on * (g * jax.nn.sigmoid(g))).astype(BF16)

    o_ref[...] += jnp.dot(mixed_scr[:, d_conv:], wout_ref[d_conv:, :], preferred_element_type=F32)


def _retention_tables(seq, ts, n_heads):
    dh = RET_HEAD_DIM
    half = dh // 2
    pos = jnp.arange(seq, dtype=F32)
    inv = ROPE_BASE ** (-jnp.arange(half, dtype=F32) / half)
    ang = pos[:, None] * inv[None, :]
    cos = jnp.cos(ang)
    sin = jnp.sin(ang)
    cos_full = jnp.concatenate([cos, cos], axis=-1)
    sin_signed = jnp.concatenate([-sin, sin], axis=-1)
    log_g = jnp.log1p(-jnp.exp2(-5.0 - jnp.arange(n_heads, dtype=F32)))
    idx = jnp.arange(ts, dtype=F32)
    dist = jnp.abs(idx[:, None] - idx[None, :])
    chunk_id = jnp.arange(ts) // CHUNK
    visible = chunk_id[None, :] <= chunk_id[:, None]
    dmask = jnp.where(visible[None], jnp.exp(log_g[:, None, None] * dist[None]), 0.0)
    rep = lambda a: jnp.repeat(a, dh, axis=-1)
    qdec = rep(jnp.exp(log_g[None, :] * (idx[:, None] + 1.0)))
    kdec = rep(jnp.exp(log_g[None, :] * (ts - 1.0 - idx[:, None])))
    sdec = rep(jnp.exp(log_g * ts)[None, :])
    return cos_full, sin_signed, dmask.astype(F32), qdec, kdec, sdec


def _mixer(z, x2, wout_bf16, conv_w, bsz, seq, layer):
    n_tok, dm = x2.shape
    d_conv = conv_w.shape[1]
    d_ret = dm - d_conv
    n_heads = d_ret // RET_HEAD_DIM
    d_in = z.shape[1]
    ts = _tile(seq, MIXER_ROWS)
    ns = seq // ts
    cos_full, sin_signed, dmask, qdec, kdec, sdec = _retention_tables(seq, ts, n_heads)
    kern = functools.partial(_mixer_kernel, ts=ts, d_conv=d_conv, n_heads=n_heads)
    const2 = lambda b, s: (0, 0)
    return pl.pallas_call(
        kern,
        grid=(bsz, ns),
        in_specs=[
            pl.BlockSpec((ts, d_in), lambda b, s: (b * ns + s, 0)),
            pl.BlockSpec((ts, dm), lambda b, s: (b * ns + s, 0)),
            pl.BlockSpec((None, dm, dm), lambda b, s: (layer, 0, 0), pipeline_mode=pl.Buffered(1)),
            pl.BlockSpec((CONV_WIDTH, d_conv), const2),
            pl.BlockSpec((ts, RET_HEAD_DIM), lambda b, s: (s, 0)),
            pl.BlockSpec((ts, RET_HEAD_DIM), lambda b, s: (s, 0)),
            pl.BlockSpec((n_heads, ts, ts), lambda b, s: (0, 0, 0)),
            pl.BlockSpec((ts, d_ret), const2),
            pl.BlockSpec((ts, d_ret), const2),
            pl.BlockSpec((1, d_ret), const2),
        ],
        out_specs=pl.BlockSpec((ts, dm), lambda b, s: (b * ns + s, 0)),
        out_shape=jax.ShapeDtypeStruct((n_tok, dm), F32),
        scratch_shapes=[
            pltpu.VMEM((n_heads, RET_HEAD_DIM, RET_HEAD_DIM), F32),
            pltpu.VMEM((ts + SUBLANES, d_conv), F32),
            pltpu.VMEM((ts, dm), BF16),
        ],
        compiler_params=pltpu.CompilerParams(
            dimension_semantics=("parallel", "arbitrary"), vmem_limit_bytes=VMEM_LIMIT),
        name="mixer",
    )(z, x2, wout_bf16, conv_w, cos_full, sin_signed, dmask, qdec, kdec, sdec)


ROUTE_ROWS = SUBLANES


def _router_kernel(x_ref, g_ref, wr_ref, br_ref, tri_ref, route_ref, route_t_ref, cnt_ref, ht_hbm,
                   cnt_scr, tbuf, tsem):
    step = pl.program_id(0)
    last = pl.num_programs(0) - 1
    slot = step % 2
    tm = x_ref.shape[0]

    @pl.when(step >= 2)
    def _():
        for cp in _tm_write_copies(tbuf.at[slot], ht_hbm, 0, tsem.at[slot]):
            cp.wait()

    @pl.when(step == 0)
    def _():
        cnt_scr[...] = jnp.zeros_like(cnt_scr)

    tbuf[slot] = _rms(x_ref[...], g_ref[...])
    for cp in _tm_write_copies(tbuf.at[slot], ht_hbm, step * tm, tsem.at[slot]):
        cp.start()
    h = tbuf[slot].astype(BF16)
    logits = jnp.dot(h, wr_ref[...], preferred_element_type=F32) + br_ref[...]
    lane = lax.broadcasted_iota(jnp.int32, logits.shape, 1)
    neg = jnp.float32(-jnp.inf)
    big = jnp.int32(LANES)

    gl = jnp.where(lane < N_GROUPS, logits, neg)
    gmax = jnp.max(gl, axis=-1, keepdims=True)
    g_sel = jnp.min(jnp.where(gl == gmax, lane, big), axis=-1, keepdims=True)
    p_g = 1.0 / jnp.sum(jnp.exp(gl - gmax), axis=-1, keepdims=True)

    lo = N_GROUPS + g_sel * EXPERTS_PER_GROUP
    el = jnp.where(lane >= lo, jnp.where(lane < lo + EXPERTS_PER_GROUP, logits, neg), neg)
    v1 = jnp.max(el, axis=-1, keepdims=True)
    i1 = jnp.min(jnp.where(el == v1, lane, big), axis=-1, keepdims=True)
    el2 = jnp.where(lane == i1, neg, el)
    v2 = jnp.max(el2, axis=-1, keepdims=True)
    i2 = jnp.min(jnp.where(el2 == v2, lane, big), axis=-1, keepdims=True)
    t = jnp.exp(v2 - v1)
    gate1 = p_g / (1.0 + t)
    gate2 = p_g * t / (1.0 + t)

    tri = tri_ref[...]
    cnt = cnt_scr[...]
    hot1 = jnp.where(lane == i1, 1.0, 0.0)
    hot2 = jnp.where(lane == i2, 1.0, 0.0)
    before1 = jnp.dot(tri, hot1.astype(BF16), preferred_element_type=F32) + cnt
    cnt = cnt + jnp.sum(hot1, axis=0, keepdims=True)
    before2 = jnp.dot(tri, hot2.astype(BF16), preferred_element_type=F32) + cnt
    cnt = cnt + jnp.sum(hot2, axis=0, keepdims=True)
    rank1 = jnp.sum(hot1 * before1, axis=-1, keepdims=True)
    rank2 = jnp.sum(hot2 * before2, axis=-1, keepdims=True)
    cnt_scr[...] = cnt
    cnt_ref[...] = cnt

    e1 = (i1 - N_GROUPS).astype(F32)
    e2 = (i2 - N_GROUPS).astype(F32)
    cols = (e1, e2, gate1, gate2, rank1, rank2)
    route = jnp.zeros(logits.shape, F32)
    for j, col in enumerate(cols):
        route = jnp.where(lane == j, col, route)
    route_ref[...] = route
    route_t_ref[...] = jnp.transpose(route)[0:ROUTE_ROWS, :]

    @pl.when(step == last)
    def _():
        for s in range(2):
            for cp in _tm_write_copies(tbuf.at[s], ht_hbm, 0, tsem.at[s]):
                cp.wait()


def _router(x2, g, w_group, b_group, w_router, b_router):
    n_tok, dm = x2.shape
    n_logit = N_GROUPS + N_EXPERTS
    wr = jnp.zeros((dm, LANES), F32).at[:, :N_GROUPS].set(w_group).at[:, N_GROUPS:n_logit].set(w_router)
    br = jnp.zeros((1, LANES), F32).at[0, :N_GROUPS].set(b_group).at[0, N_GROUPS:n_logit].set(b_router)
    tm = _tile(n_tok, ROUTER_ROWS)
    assert n_tok // tm >= 2, "the token-major write pipeline needs at least two grid steps"
    row = jnp.arange(tm, dtype=jnp.int32)
    tri = (row[None, :] < row[:, None]).astype(BF16)
    route, route_t, cnt, ht = pl.pallas_call(
        _router_kernel,
        grid=(n_tok // tm,),
        in_specs=[
            pl.BlockSpec((tm, dm), lambda i: (i, 0)),
            pl.BlockSpec((1, dm), lambda i: (0, 0)),
            pl.BlockSpec((dm, LANES), lambda i: (0, 0)),
            pl.BlockSpec((1, LANES), lambda i: (0, 0)),
            pl.BlockSpec((tm, tm), lambda i: (0, 0)),
        ],
        out_specs=[pl.BlockSpec((tm, LANES), lambda i: (i, 0)),
                   pl.BlockSpec((ROUTE_ROWS, tm), lambda i: (0, i)),
                   pl.BlockSpec((1, LANES), lambda i: (0, 0)),
                   pl.BlockSpec(memory_space=pl.ANY)],
        out_shape=[jax.ShapeDtypeStruct((n_tok, LANES), F32),
                   jax.ShapeDtypeStruct((ROUTE_ROWS, n_tok), F32),
                   jax.ShapeDtypeStruct((1, LANES), F32),
                   jax.ShapeDtypeStruct((n_tok, dm // LANES, LANES), F32)],
        scratch_shapes=[pltpu.VMEM((1, LANES), F32),
                        pltpu.VMEM((2, tm, dm), F32),
                        pltpu.SemaphoreType.DMA((2,))],
        compiler_params=pltpu.CompilerParams(
            dimension_semantics=("arbitrary",), vmem_limit_bytes=VMEM_LIMIT),
        name="router",
    )(x2, g.reshape(1, dm), wr.astype(BF16), br, tri)
    return route, route_t, cnt[0, N_GROUPS:n_logit].astype(jnp.int32), ht


def _row_copy(src_tm, idx, dst, r8, s, sem):
    return pltpu.make_async_copy(src_tm.at[idx], dst.at[r8, :, s, :], sem)


DMA_PRIORITIES = 2


def _issue_rows(idx_ref, idx_base, src_tm, dst, sem, n_rows, spread=False):
    for r in range(n_rows):
        priority = r % DMA_PRIORITIES if spread else 0
        _row_copy(src_tm, idx_ref[idx_base + r], dst, r // SUBLANES, r % SUBLANES, sem).start(
            priority=priority)


def _wait_rows(dst, sem):
    pltpu.make_async_copy(dst, dst, sem).wait()


CAST_STEPS = 4
MODE_IDLE, MODE_COMPUTE, MODE_ZERO = 0, 1, 2


def _lookup(table, idx):
    pos = jnp.arange(table.shape[0], dtype=jnp.int32)
    return jnp.sum(jnp.where(idx[..., None] == pos, table, 0), axis=-1)


def _expert_schedule(counts, tmb, n_blocks):
    cs = CAST_STEPS
    n_steps = cs + n_blocks + N_EXPERTS * (cs - 1)
    nblk = (counts + tmb - 1) // tmb
    used = nblk > 0
    steps_e = jnp.where(used, jnp.maximum(nblk, cs), 0)
    step_end = cs + jnp.cumsum(steps_e)
    step_start = step_end - steps_e
    blk_start = jnp.cumsum(nblk) - nblk
    n_used_blk = jnp.sum(nblk)
    total = step_end[-1]
    eidx = jnp.arange(N_EXPERTS, dtype=jnp.int32)
    next_ge = lax.cummin(jnp.where(used, eidx, N_EXPERTS), reverse=True)
    next_gt = jnp.concatenate([next_ge[1:], jnp.full((1,), N_EXPERTS, jnp.int32)])
    seg_ord = jnp.cumsum(used.astype(jnp.int32)) - used.astype(jnp.int32)

    last_used = jnp.max(jnp.where(used, eidx, 0))

    i = jnp.arange(n_steps + 1, dtype=jnp.int32)
    e_i = jnp.minimum(jnp.sum(step_end[None, :] <= i[:, None], axis=1), N_EXPERTS - 1)
    at_e = lambda table: _lookup(table, e_i)
    p = i - at_e(step_start)
    warm = i < cs
    in_seg = jnp.logical_and(~warm, i < total)
    compute = jnp.logical_and(in_seg, p < at_e(nblk))
    zidx = i - total
    zero = (i >= total) & (i < n_steps) & (n_used_blk + zidx < n_blocks)
    mode = jnp.where(compute, MODE_COMPUTE, jnp.where(zero, MODE_ZERO, MODE_IDLE))
    blk = jnp.where(compute, at_e(blk_start) + p, jnp.where(zero, n_used_blk + zidx, 0))
    par = jnp.where(in_seg, at_e(seg_ord) % 2, 1)
    nxt = at_e(next_gt)
    has_next = jnp.logical_and(in_seg, nxt < N_EXPERTS)
    cast = warm | (has_next & (p < cs))
    cexp = jnp.where(warm, next_ge[0], jnp.where(has_next, nxt, last_used))
    cidx = jnp.where(warm, i, jnp.where(has_next, jnp.minimum(p, cs - 1), cs - 1))
    as_i32 = lambda a: a.astype(jnp.int32)
    return n_steps, tuple(map(as_i32, (mode, blk, par, cexp, cidx, cast)))


def _experts_kernel(mode_ref, blk_ref, par_ref, cexp_ref, cidx_ref, cast_ref, row_tok_ref,
                    xt_hbm, w1f_ref, w3f_ref, w2f_ref, yb_hbm,
                    xbuf, h_scr, obuf, w1a, w3a, w2a, w1b, w3b, w2b, sem, osem, *, tmb):
    del cexp_ref
    i = pl.program_id(0)
    last = pl.num_programs(0) - 1
    slot = i % 2
    nxt = (i + 1) % 2
    n_chunks = xt_hbm.shape[1]
    mode = mode_ref[i]
    par = par_ref[i]
    weights = ((w1a, w3a, w2a), (w1b, w3b, w2b))

    def wait_out(s):
        for cp in _tm_write_copies(obuf.at[s], yb_hbm, 0, osem.at[s]):
            cp.wait()

    @pl.when(jnp.logical_and(i >= 2, mode_ref[jnp.maximum(i - 2, 0)] != MODE_IDLE))
    def _():
        wait_out(slot)

    @pl.when(mode_ref[i + 1] == MODE_COMPUTE)
    def _():
        _issue_rows(row_tok_ref, blk_ref[i + 1] * tmb, xt_hbm, xbuf.at[nxt], sem.at[nxt], tmb)

    for v in range(2):
        @pl.when(jnp.logical_and(cast_ref[i] == 1, par == v))
        def _(v=v):
            w1n, w3n, w2n = weights[1 - v]
            r13, r2 = w1f_ref.shape[0], w2f_ref.shape[0]
            c = cidx_ref[i]
            w1n[pl.ds(pl.multiple_of(c * r13, r13), r13), :] = w1f_ref[...].astype(BF16)
            w3n[pl.ds(pl.multiple_of(c * r13, r13), r13), :] = w3f_ref[...].astype(BF16)
            w2n[pl.ds(pl.multiple_of(c * r2, r2), r2), :] = w2f_ref[...].astype(BF16)

    for v in range(2):
        @pl.when(jnp.logical_and(mode == MODE_COMPUTE, par == v))
        def _(v=v):
            w1c, w3c, w2c = weights[v]
            _wait_rows(xbuf.at[slot], sem.at[slot])
            for c in range(n_chunks):
                h_scr[:, c * LANES:(c + 1) * LANES] = _dense_chunk(xbuf, (slot,), c).astype(BF16)
            h = h_scr[...]
            a = jnp.dot(h, w1c[...], preferred_element_type=F32)
            b = jnp.dot(h, w3c[...], preferred_element_type=F32)
            hid = (a * jax.nn.sigmoid(a) * b).astype(BF16)
            obuf[slot] = jnp.dot(hid, w2c[...], preferred_element_type=F32)

    @pl.when(mode == MODE_ZERO)
    def _():
        obuf[slot] = jnp.zeros(obuf.shape[1:], F32)

    @pl.when(mode != MODE_IDLE)
    def _():
        for cp in _tm_write_copies(obuf.at[slot], yb_hbm, blk_ref[i] * tmb, osem.at[slot]):
            cp.start()

    @pl.when(i == last)
    def _():
        @pl.when(mode != MODE_IDLE)
        def _():
            wait_out(slot)

        @pl.when(jnp.logical_and(i >= 1, mode_ref[jnp.maximum(i - 1, 0)] != MODE_IDLE))
        def _():
            wait_out(nxt)


def _experts(xt, w1, w3, w2, counts, row_tok, tmb, n_blocks, layer):
    n_tok, n_chunks, _ = xt.shape
    dm = n_chunks * LANES
    de = w1.shape[3]
    assert dm % CAST_STEPS == 0 and de % CAST_STEPS == 0
    n_steps, sched = _expert_schedule(counts, tmb, n_blocks)
    wmap = lambda i, mode, blk, par, cexp, cidx, cast, rt: (layer, cexp[i], cidx[i], 0)
    bf16_slot = [pltpu.VMEM((dm, de), BF16), pltpu.VMEM((dm, de), BF16), pltpu.VMEM((de, dm), BF16)]
    return pl.pallas_call(
        functools.partial(_experts_kernel, tmb=tmb),
        grid_spec=pltpu.PrefetchScalarGridSpec(
            num_scalar_prefetch=7,
            grid=(n_steps,),
            in_specs=[
                pl.BlockSpec(memory_space=pl.ANY),
                pl.BlockSpec((None, None, dm // CAST_STEPS, de), wmap),
                pl.BlockSpec((None, None, dm // CAST_STEPS, de), wmap),
                pl.BlockSpec((None, None, de // CAST_STEPS, dm), wmap),
            ],
            out_specs=pl.BlockSpec(memory_space=pl.ANY),
            scratch_shapes=[
                pltpu.VMEM((2, tmb // SUBLANES, n_chunks, SUBLANES, LANES), F32),
                pltpu.VMEM((tmb, dm), BF16),
                pltpu.VMEM((2, tmb, dm), F32),
                *bf16_slot, *bf16_slot,
                pltpu.SemaphoreType.DMA((2,)), pltpu.SemaphoreType.DMA((2,))],
        ),
        out_shape=jax.ShapeDtypeStruct((n_blocks * tmb, n_chunks, LANES), F32),
        compiler_params=pltpu.CompilerParams(
            dimension_semantics=("arbitrary",), vmem_limit_bytes=VMEM_LIMIT),
        name="experts",
    )(*sched, row_tok, xt, w1, w3, w2)


def _combine_kernel(dest_ref, x_ref, route_ref, gf_ref, yb_hbm, o_ref, buf, sem, *, tm, n_tok, final):
    i = pl.program_id(0)
    n_chunks = yb_hbm.shape[1]
    dm = n_chunks * LANES

    def issue(step, slot):
        for k in range(TOP_K):
            _issue_rows(dest_ref, k * n_tok + step * tm, yb_hbm, buf.at[slot, k], sem.at[slot, k], tm,
                        spread=True)

    @pl.when(i == 0)
    def _():
        issue(0, 0)

    @pl.when(i + 1 < pl.num_programs(0))
    def _():
        issue(i + 1, (i + 1) % 2)

    slot = i % 2
    for k in range(TOP_K):
        _wait_rows(buf.at[slot, k], sem.at[slot, k])
    gate0 = route_ref[:, 2:3]
    gate1 = route_ref[:, 3:4]
    ss = jnp.zeros((tm, 1), F32)
    for c in range(n_chunks):
        cs = slice(c * LANES, (c + 1) * LANES)
        oc = (x_ref[:, cs] + gate0 * _dense_chunk(buf, (slot, 0), c)
              + gate1 * _dense_chunk(buf, (slot, 1), c))
        o_ref[:, cs] = oc
        if final:
            ss = ss + jnp.sum(oc * oc, axis=-1, keepdims=True)
    if final:
        rs = lax.rsqrt(ss * (1.0 / dm) + EPS)
        for c in range(n_chunks):
            cs = slice(c * LANES, (c + 1) * LANES)
            o_ref[:, cs] = o_ref[:, cs] * rs * gf_ref[:, cs]


def _combine(x2, route, yb, dest, gf, final):
    n_tok, dm = x2.shape
    n_chunks = dm // LANES
    tm = _tile(n_tok, COMBINE_ROWS)
    return pl.pallas_call(
        functools.partial(_combine_kernel, tm=tm, n_tok=n_tok, final=final),
        grid_spec=pltpu.PrefetchScalarGridSpec(
            num_scalar_prefetch=1,
            grid=(n_tok // tm,),
            in_specs=[
                pl.BlockSpec((tm, dm), lambda i, d: (i, 0)),
                pl.BlockSpec((tm, LANES), lambda i, d: (i, 0)),
                pl.BlockSpec((1, dm), lambda i, d: (0, 0)),
                pl.BlockSpec(memory_space=pl.ANY),
            ],
            out_specs=pl.BlockSpec((tm, dm), lambda i, d: (i, 0)),
            scratch_shapes=[pltpu.VMEM((2, TOP_K, tm // SUBLANES, n_chunks, SUBLANES, LANES), F32),
                            pltpu.SemaphoreType.DMA((2, TOP_K))],
        ),
        out_shape=jax.ShapeDtypeStruct((n_tok, dm), F32),
        compiler_params=pltpu.CompilerParams(
            dimension_semantics=("arbitrary",), vmem_limit_bytes=VMEM_LIMIT),
        name="combine",
    )(dest, x2, route, gf.reshape(1, dm), yb)


SCATTER_UNROLL = 32


def _row_tok_kernel(dest_ref, pad_hbm, row_tok_hbm, rt_smem, sem, *, n_tok):
    fill = pltpu.make_async_copy(pad_hbm, rt_smem, sem)
    fill.start()
    fill.wait()
    for k in range(TOP_K):
        def scatter(j, carry):
            for u in range(SCATTER_UNROLL):
                t = j * SCATTER_UNROLL + u
                rt_smem[dest_ref[k * n_tok + t]] = t
            return carry

        lax.fori_loop(0, n_tok // SCATTER_UNROLL, scatter, 0)
    out = pltpu.make_async_copy(rt_smem, row_tok_hbm, sem)
    out.start()
    out.wait()


def _row_tok(dest_kmajor, n_tok, n_rows):
    assert n_tok % SCATTER_UNROLL == 0
    pad_tok = jnp.arange(n_rows, dtype=jnp.int32) % n_tok
    return pl.pallas_call(
        functools.partial(_row_tok_kernel, n_tok=n_tok),
        in_specs=[pl.BlockSpec(memory_space=pltpu.SMEM), pl.BlockSpec(memory_space=pl.ANY)],
        out_specs=pl.BlockSpec(memory_space=pl.ANY),
        out_shape=jax.ShapeDtypeStruct((n_rows,), jnp.int32),
        scratch_shapes=[pltpu.SMEM((n_rows,), jnp.int32), pltpu.SemaphoreType.DMA],
        name="row_tok",
    )(dest_kmajor, pad_tok)


def _plan(route_t, counts, tmb, n_blocks):
    n_tok = route_t.shape[1]
    e = route_t[0:TOP_K].astype(jnp.int32)
    rank = route_t[4:4 + TOP_K].astype(jnp.int32)
    pcounts = (counts + tmb - 1) // tmb * tmb
    pends = jnp.cumsum(pcounts)
    pstarts = pends - pcounts
    dest_kmajor = (_lookup(pstarts, e) + rank).reshape(n_tok * TOP_K)
    row_tok = _row_tok(dest_kmajor, n_tok, n_blocks * tmb)
    return dest_kmajor, row_tok


def kernel(x, norm_mix_g, w_in, conv_w, w_out, norm_ffn_g, w_group, b_group, w_router, b_router,
           w_expert_gate, w_expert_up, w_expert_down, final_norm_g):
    bsz, seq, dm = x.shape
    depth = w_in.shape[0]
    n_tok = bsz * seq
    tmb = _tile(n_tok * TOP_K, EXPERT_ROWS)
    n_blocks = n_tok * TOP_K // tmb + N_EXPERTS
    x2 = x.reshape(n_tok, dm)
    w_in_b, w_out_b = w_in.astype(BF16), w_out.astype(BF16)
    for l in range(depth):
        z = _in_proj(x2, norm_mix_g[l], w_in_b, l)
        x2 = _mixer(z, x2, w_out_b, conv_w[l], bsz, seq, l)
        route, route_t, counts, ht = _router(x2, norm_ffn_g[l], w_group[l], b_group[l], w_router[l],
                                             b_router[l])
        dest, row_tok = _plan(route_t, counts, tmb, n_blocks)
        yb = _experts(ht, w_expert_gate, w_expert_up, w_expert_down, counts, row_tok, tmb, n_blocks, l)
        x2 = _combine(x2, route, yb, dest, final_norm_g, final=(l == depth - 1))
    return x2.reshape(bsz, seq, dm)
```

```python
import functools

import jax
import jax.numpy as jnp
from jax import lax
from jax.experimental import pallas as pl
from jax.experimental.pallas import tpu as pltpu

CHUNK = 64
RET_HEAD_DIM = 128
CONV_WIDTH = 3
ROPE_BASE = 10000.0
N_GROUPS = 4
EXPERTS_PER_GROUP = 8
N_EXPERTS = N_GROUPS * EXPERTS_PER_GROUP
TOP_K = 2
EPS = 1e-6

LANES = 128
SUBLANES = 8
VMEM_LIMIT = 56 * 1024 * 1024

IN_PROJ_ROWS, IN_PROJ_COLS = 1024, 1792
MIXER_ROWS = 256
ROUTER_ROWS = 512
EXPERT_ROWS = 256
COMBINE_ROWS = 256

F32 = jnp.float32
BF16 = jnp.bfloat16


def _tile(n, want):
    t = min(n, want)
    while n % t:
        t //= 2
    return t


def _rms(x, g):
    return x * lax.rsqrt(jnp.mean(x * x, axis=-1, keepdims=True) + EPS) * g


def _tm_write_copies(src_dense, dst_tm_hbm, row0, sem):
    rows = src_dense.shape[0]
    return [pltpu.make_async_copy(src_dense.at[:, pl.ds(c * LANES, LANES)],
                                  dst_tm_hbm.at[pl.ds(row0, rows), c, :], sem)
            for c in range(dst_tm_hbm.shape[1])]


def _dense_chunk(buf, lead, c):
    v = buf[(*lead, slice(None), c)]
    return v.reshape(v.shape[0] * SUBLANES, LANES)


def _in_proj_kernel(x_ref, g_ref, w_ref, o_ref, h_scr):
    @pl.when(pl.program_id(1) == 0)
    def _():
        h_scr[...] = _rms(x_ref[...], g_ref[...]).astype(BF16)

    o_ref[...] = jnp.dot(h_scr[...], w_ref[...], preferred_element_type=F32).astype(o_ref.dtype)


def _in_proj(x2, g, w_bf16, layer):
    n_tok, dm = x2.shape
    n_out = w_bf16.shape[2]
    tm = _tile(n_tok, IN_PROJ_ROWS)
    tn = _tile(n_out, IN_PROJ_COLS)
    return pl.pallas_call(
        _in_proj_kernel,
        grid=(n_tok // tm, n_out // tn),
        in_specs=[
            pl.BlockSpec((tm, dm), lambda i, j: (i, 0)),
            pl.BlockSpec((1, dm), lambda i, j: (0, 0)),
            pl.BlockSpec((None, dm, tn), lambda i, j: (layer, 0, j)),
        ],
        out_specs=pl.BlockSpec((tm, tn), lambda i, j: (i, j)),
        out_shape=jax.ShapeDtypeStruct((n_tok, n_out), BF16),
        scratch_shapes=[pltpu.VMEM((tm, dm), BF16)],
        compiler_params=pltpu.CompilerParams(
            dimension_semantics=("parallel", "arbitrary"), vmem_limit_bytes=VMEM_LIMIT),
        name="in_proj",
    )(x2, g.reshape(1, dm), w_bf16)


def _mixer_kernel(z_ref, x_ref, wout_ref, convw_ref, cos_ref, sin_ref, dmask_ref, qdec_ref,
                  kdec_ref, sdec_ref, o_ref, state_scr, u_scr, mixed_scr, *, ts, d_conv, n_heads):
    dh = RET_HEAD_DIM
    d_ret = n_heads * dh

    @pl.when(pl.program_id(1) == 0)
    def _():
        state_scr[...] = jnp.zeros_like(state_scr)
        u_scr[0:SUBLANES, :] = jnp.zeros((SUBLANES, d_conv), F32)

    zf = lambda lo, hi: z_ref[:, lo:hi].astype(F32)
    u_scr[SUBLANES:SUBLANES + ts, :] = zf(d_conv, 2 * d_conv) * zf(2 * d_conv, 3 * d_conv)
    y = (convw_ref[2:3, :] * u_scr[SUBLANES:SUBLANES + ts, :]
         + convw_ref[1:2, :] * u_scr[SUBLANES - 1:SUBLANES - 1 + ts, :]
         + convw_ref[0:1, :] * u_scr[SUBLANES - 2:SUBLANES - 2 + ts, :])
    mixed_scr[:, 0:d_conv] = (zf(0, d_conv) * y).astype(BF16)
    u_scr[0:SUBLANES, :] = u_scr[ts:ts + SUBLANES, :]
    o_ref[...] = x_ref[...] + jnp.dot(mixed_scr[:, 0:d_conv], wout_ref[0:d_conv, :],
                                      preferred_element_type=F32)

    cos = cos_ref[...]
    sin = sin_ref[...]
    scale = RET_HEAD_DIM ** -0.5
    base = 3 * d_conv
    for h in range(n_heads):
        c0 = h * dh
        q = zf(base + c0, base + c0 + dh)
        k = zf(base + d_ret + c0, base + d_ret + c0 + dh)
        v = z_ref[:, base + 2 * d_ret + c0:base + 2 * d_ret + c0 + dh]
        g = zf(base + 3 * d_ret + c0, base + 3 * d_ret + c0 + dh)
        qr = q * cos + pltpu.roll(q, dh // 2, 1) * sin
        kr = (k * cos + pltpu.roll(k, dh // 2, 1) * sin) * scale
        s = lax.dot_general(qr.astype(BF16), kr.astype(BF16), (((1,), (1,)), ((), ())),
                            preferred_element_type=F32) * dmask_ref[h]
        o = jnp.dot(s.astype(BF16), v, preferred_element_type=F32)
        st = state_scr[h]
        o = o + jnp.dot((qr * qdec_ref[:, c0:c0 + dh]).astype(BF16), st.astype(BF16),
                        preferred_element_type=F32)
        kv = lax.dot_general((kr * kdec_ref[:, c0:c0 + dh]).astype(BF16), v,
                             (((0,), (0,)), ((), ())), preferred_element_type=F32)
        state_scr[h] = st * sdec_ref[:, c0:c0 + dh] + kv
        on = o * lax.rsqrt(jnp.mean(o * o, axis=-1, keepdims=True) + EPS)
        mixed_scr[:, d_conv + c0:d_conv + c0 + dh] = (on * (g * jax.nn.sigmoid(g))).astype(BF16)

    o_ref[...] += jnp.dot(mixed_scr[:, d_conv:], wout_ref[d_conv:, :], preferred_element_type=F32)


def _retention_tables(seq, ts, n_heads):
    dh = RET_HEAD_DIM
    half = dh // 2
    pos = jnp.arange(seq, dtype=F32)
    inv = ROPE_BASE ** (-jnp.arange(half, dtype=F32) / half)
    ang = pos[:, None] * inv[None, :]
    cos = jnp.cos(ang)
    sin = jnp.sin(ang)
    cos_full = jnp.concatenate([cos, cos], axis=-1)
    sin_signed = jnp.concatenate([-sin, sin], axis=-1)
    log_g = jnp.log1p(-jnp.exp2(-5.0 - jnp.arange(n_heads, dtype=F32)))
    idx = jnp.arange(ts, dtype=F32)
    dist = jnp.abs(idx[:, None] - idx[None, :])
    chunk_id = jnp.arange(ts) // CHUNK
    visible = chunk_id[None, :] <= chunk_id[:, None]
    dmask = jnp.where(visible[None], jnp.exp(log_g[:, None, None] * dist[None]), 0.0)
    rep = lambda a: jnp.repeat(a, dh, axis=-1)
    qdec = rep(jnp.exp(log_g[None, :] * (idx[:, None] + 1.0)))
    kdec = rep(jnp.exp(log_g[None, :] * (ts - 1.0 - idx[:, None])))
    sdec = rep(jnp.exp(log_g * ts)[None, :])
    return cos_full, sin_signed, dmask.astype(F32), qdec, kdec, sdec


def _mixer(z, x2, wout_bf16, conv_w, bsz, seq, layer):
    n_tok, dm = x2.shape
    d_conv = conv_w.shape[1]
    d_ret = dm - d_conv
    n_heads = d_ret // RET_HEAD_DIM
    d_in = z.shape[1]
    ts = _tile(seq, MIXER_ROWS)
    ns = seq // ts
    cos_full, sin_signed, dmask, qdec, kdec, sdec = _retention_tables(seq, ts, n_heads)
    kern = functools.partial(_mixer_kernel, ts=ts, d_conv=d_conv, n_heads=n_heads)
    const2 = lambda b, s: (0, 0)
    return pl.pallas_call(
        kern,
        grid=(bsz, ns),
        in_specs=[
            pl.BlockSpec((ts, d_in), lambda b, s: (b * ns + s, 0)),
            pl.BlockSpec((ts, dm), lambda b, s: (b * ns + s, 0)),
            pl.BlockSpec((None, dm, dm), lambda b, s: (layer, 0, 0), pipeline_mode=pl.Buffered(1)),
            pl.BlockSpec((CONV_WIDTH, d_conv), const2),
            pl.BlockSpec((ts, RET_HEAD_DIM), lambda b, s: (s, 0)),
            pl.BlockSpec((ts, RET_HEAD_DIM), lambda b, s: (s, 0)),
            pl.BlockSpec((n_heads, ts, ts), lambda b, s: (0, 0, 0)),
            pl.BlockSpec((ts, d_ret), const2),
            pl.BlockSpec((ts, d_ret), const2),
            pl.BlockSpec((1, d_ret), const2),
        ],
        out_specs=pl.BlockSpec((ts, dm), lambda b, s: (b * ns + s, 0)),
        out_shape=jax.ShapeDtypeStruct((n_tok, dm), F32),
        scratch_shapes=[
            pltpu.VMEM((n_heads, RET_HEAD_DIM, RET_HEAD_DIM), F32),
            pltpu.VMEM((ts + SUBLANES, d_conv), F32),
            pltpu.VMEM((ts, dm), BF16),
        ],
        compiler_params=pltpu.CompilerParams(
            dimension_semantics=("parallel", "arbitrary"), vmem_limit_bytes=VMEM_LIMIT),
        name="mixer",
    )(z, x2, wout_bf16, conv_w, cos_full, sin_signed, dmask, qdec, kdec, sdec)


ROUTE_ROWS = SUBLANES


def _router_kernel(x_ref, g_ref, wr_ref, br_ref, tri_ref, route_ref, route_t_ref, cnt_ref, ht_hbm,
                   cnt_scr, tbuf, tsem):
    step = pl.program_id(0)
    last = pl.num_programs(0) - 1
    slot = step % 2
    tm = x_ref.shape[0]

    @pl.when(step >= 2)
    def _():
        for cp in _tm_write_copies(tbuf.at[slot], ht_hbm, 0, tsem.at[slot]):
            cp.wait()

    @pl.when(step == 0)
    def _():
        cnt_scr[...] = jnp.zeros_like(cnt_scr)

    tbuf[slot] = _rms(x_ref[...], g_ref[...])
    for cp in _tm_write_copies(tbuf.at[slot], ht_hbm, step * tm, tsem.at[slot]):
        cp.start()
    h = tbuf[slot].astype(BF16)
    logits = jnp.dot(h, wr_ref[...], preferred_element_type=F32) + br_ref[...]
    lane = lax.broadcasted_iota(jnp.int32, logits.shape, 1)
    neg = jnp.float32(-jnp.inf)
    big = jnp.int32(LANES)

    gl = jnp.where(lane < N_GROUPS, logits, neg)
    gmax = jnp.max(gl, axis=-1, keepdims=True)
    g_sel = jnp.min(jnp.where(gl == gmax, lane, big), axis=-1, keepdims=True)
    p_g = 1.0 / jnp.sum(jnp.exp(gl - gmax), axis=-1, keepdims=True)

    lo = N_GROUPS + g_sel * EXPERTS_PER_GROUP
    el = jnp.where(lane >= lo, jnp.where(lane < lo + EXPERTS_PER_GROUP, logits, neg), neg)
    v1 = jnp.max(el, axis=-1, keepdims=True)
    i1 = jnp.min(jnp.where(el == v1, lane, big), axis=-1, keepdims=True)
    el2 = jnp.where(lane == i1, neg, el)
    v2 = jnp.max(el2, axis=-1, keepdims=True)
    i2 = jnp.min(jnp.where(el2 == v2, lane, big), axis=-1, keepdims=True)
    t = jnp.exp(v2 - v1)
    gate1 = p_g / (1.0 + t)
    gate2 = p_g * t / (1.0 + t)

    tri = tri_ref[...]
    cnt = cnt_scr[...]
    hot1 = jnp.where(lane == i1, 1.0, 0.0)
    hot2 = jnp.where(lane == i2, 1.0, 0.0)
    before1 = jnp.dot(tri, hot1.astype(BF16), preferred_element_type=F32) + cnt
    cnt = cnt + jnp.sum(hot1, axis=0, keepdims=True)
    before2 = jnp.dot(tri, hot2.astype(BF16), preferred_element_type=F32) + cnt
    cnt = cnt + jnp.sum(hot2, axis=0, keepdims=True)
    rank1 = jnp.sum(hot1 * before1, axis=-1, keepdims=True)
    rank2 = jnp.sum(hot2 * before2, axis=-1, keepdims=True)
    cnt_scr[...] = cnt
    cnt_ref[...] = cnt

    e1 = (i1 - N_GROUPS).astype(F32)
    e2 = (i2 - N_GROUPS).astype(F32)
    cols = (e1, e2, gate1, gate2, rank1, rank2)
    route = jnp.zeros(logits.shape, F32)
    for j, col in enumerate(cols):
        route = jnp.where(lane == j, col, route)
    route_ref[...] = route
    route_t_ref[...] = jnp.transpose(route)[0:ROUTE_ROWS, :]

    @pl.when(step == last)
    def _():
        for s in range(2):
            for cp in _tm_write_copies(tbuf.at[s], ht_hbm, 0, tsem.at[s]):
                cp.wait()


def _router(x2, g, w_group, b_group, w_router, b_router):
    n_tok, dm = x2.shape
    n_logit = N_GROUPS + N_EXPERTS
    wr = jnp.zeros((dm, LANES), F32).at[:, :N_GROUPS].set(w_group).at[:, N_GROUPS:n_logit].set(w_router)
    br = jnp.zeros((1, LANES), F32).at[0, :N_GROUPS].set(b_group).at[0, N_GROUPS:n_logit].set(b_router)
    tm = _tile(n_tok, ROUTER_ROWS)
    assert n_tok // tm >= 2, "the token-major write pipeline needs at least two grid steps"
    row = jnp.arange(tm, dtype=jnp.int32)
    tri = (row[None, :] < row[:, None]).astype(BF16)
    route, route_t, cnt, ht = pl.pallas_call(
        _router_kernel,
        grid=(n_tok // tm,),
        in_specs=[
            pl.BlockSpec((tm, dm), lambda i: (i, 0)),
            pl.BlockSpec((1, dm), lambda i: (0, 0)),
            pl.BlockSpec((dm, LANES), lambda i: (0, 0)),
            pl.BlockSpec((1, LANES), lambda i: (0, 0)),
            pl.BlockSpec((tm, tm), lambda i: (0, 0)),
        ],
        out_specs=[pl.BlockSpec((tm, LANES), lambda i: (i, 0)),
                   pl.BlockSpec((ROUTE_ROWS, tm), lambda i: (0, i)),
                   pl.BlockSpec((1, LANES), lambda i: (0, 0)),
                   pl.BlockSpec(memory_space=pl.ANY)],
        out_shape=[jax.ShapeDtypeStruct((n_tok, LANES), F32),
                   jax.ShapeDtypeStruct((ROUTE_ROWS, n_tok), F32),
                   jax.ShapeDtypeStruct((1, LANES), F32),
                   jax.ShapeDtypeStruct((n_tok, dm // LANES, LANES), F32)],
        scratch_shapes=[pltpu.VMEM((1, LANES), F32),
                        pltpu.VMEM((2, tm, dm), F32),
                        pltpu.SemaphoreType.DMA((2,))],
        compiler_params=pltpu.CompilerParams(
            dimension_semantics=("arbitrary",), vmem_limit_bytes=VMEM_LIMIT),
        name="router",
    )(x2, g.reshape(1, dm), wr.astype(BF16), br, tri)
    return route, route_t, cnt[0, N_GROUPS:n_logit].astype(jnp.int32), ht


def _row_copy(src_tm, idx, dst, r8, s, sem):
    return pltpu.make_async_copy(src_tm.at[idx], dst.at[r8, :, s, :], sem)


DMA_PRIORITIES = 2


def _issue_rows(idx_ref, idx_base, src_tm, dst, sem, n_rows, spread=False):
    for r in range(n_rows):
        priority = r % DMA_PRIORITIES if spread else 0
        _row_copy(src_tm, idx_ref[idx_base + r], dst, r // SUBLANES, r % SUBLANES, sem).start(
            priority=priority)


def _wait_rows(dst, sem):
    pltpu.make_async_copy(dst, dst, sem).wait()


CAST_STEPS = 4
MODE_IDLE, MODE_COMPUTE, MODE_ZERO = 0, 1, 2


def _lookup(table, idx):
    pos = jnp.arange(table.shape[0], dtype=jnp.int32)
    return jnp.sum(jnp.where(idx[..., None] == pos, table, 0), axis=-1)


def _expert_schedule(counts, tmb, n_blocks):
    cs = CAST_STEPS
    n_steps = cs + n_blocks + N_EXPERTS * (cs - 1)
    nblk = (counts + tmb - 1) // tmb
    used = nblk > 0
    steps_e = jnp.where(used, jnp.maximum(nblk, cs), 0)
    step_end = cs + jnp.cumsum(steps_e)
    step_start = step_end - steps_e
    blk_start = jnp.cumsum(nblk) - nblk
    n_used_blk = jnp.sum(nblk)
    total = step_end[-1]
    eidx = jnp.arange(N_EXPERTS, dtype=jnp.int32)
    next_ge = lax.cummin(jnp.where(used, eidx, N_EXPERTS), reverse=True)
    next_gt = jnp.concatenate([next_ge[1:], jnp.full((1,), N_EXPERTS, jnp.int32)])
    seg_ord = jnp.cumsum(used.astype(jnp.int32)) - used.astype(jnp.int32)

    last_used = jnp.max(jnp.where(used, eidx, 0))

    i = jnp.arange(n_steps + 1, dtype=jnp.int32)
    e_i = jnp.minimum(jnp.sum(step_end[None, :] <= i[:, None], axis=1), N_EXPERTS - 1)
    at_e = lambda table: _lookup(table, e_i)
    p = i - at_e(step_start)
    warm = i < cs
    in_seg = jnp.logical_and(~warm, i < total)
    compute = jnp.logical_and(in_seg, p < at_e(nblk))
    zidx = i - total
    zero = (i >= total) & (i < n_steps) & (n_used_blk + zidx < n_blocks)
    mode = jnp.where(compute, MODE_COMPUTE, jnp.where(zero, MODE_ZERO, MODE_IDLE))
    blk = jnp.where(compute, at_e(blk_start) + p, jnp.where(zero, n_used_blk + zidx, 0))
    par = jnp.where(in_seg, at_e(seg_ord) % 2, 1)
    nxt = at_e(next_gt)
    has_next = jnp.logical_and(in_seg, nxt < N_EXPERTS)
    cast = warm | (has_next & (p < cs))
    cexp = jnp.where(warm, next_ge[0], jnp.where(has_next, nxt, last_used))
    cidx = jnp.where(warm, i, jnp.where(has_next, jnp.minimum(p, cs - 1), cs - 1))
    as_i32 = lambda a: a.astype(jnp.int32)
    return n_steps, tuple(map(as_i32, (mode, blk, par, cexp, cidx, cast)))


def _experts_kernel(mode_ref, blk_ref, par_ref, cexp_ref, cidx_ref, cast_ref, row_tok_ref,
                    xt_hbm, w1f_ref, w3f_ref, w2f_ref, yb_hbm,
                    xbuf, h_scr, obuf, w1a, w3a, w2a, w1b, w3b, w2b, sem, osem, *, tmb):
    del cexp_ref
    i = pl.program_id(0)
    last = pl.num_programs(0) - 1
    slot = i % 2
    nxt = (i + 1) % 2
    n_chunks = xt_hbm.shape[1]
    mode = mode_ref[i]
    par = par_ref[i]
    weights = ((w1a, w3a, w2a), (w1b, w3b, w2b))

    def wait_out(s):
        for cp in _tm_write_copies(obuf.at[s], yb_hbm, 0, osem.at[s]):
            cp.wait()

    @pl.when(jnp.logical_and(i >= 2, mode_ref[jnp.maximum(i - 2, 0)] != MODE_IDLE))
    def _():
        wait_out(slot)

    @pl.when(mode_ref[i + 1] == MODE_COMPUTE)
    def _():
        _issue_rows(row_tok_ref, blk_ref[i + 1] * tmb, xt_hbm, xbuf.at[nxt], sem.at[nxt], tmb, spread=True)

    for v in range(2):
        @pl.when(jnp.logical_and(cast_ref[i] == 1, par == v))
        def _(v=v):
            w1n, w3n, w2n = weights[1 - v]
            r13, r2 = w1f_ref.shape[0], w2f_ref.shape[0]
            c = cidx_ref[i]
            w1n[pl.ds(pl.multiple_of(c * r13, r13), r13), :] = w1f_ref[...].astype(BF16)
            w3n[pl.ds(pl.multiple_of(c * r13, r13), r13), :] = w3f_ref[...].astype(BF16)
            w2n[pl.ds(pl.multiple_of(c * r2, r2), r2), :] = w2f_ref[...].astype(BF16)

    for v in range(2):
        @pl.when(jnp.logical_and(mode == MODE_COMPUTE, par == v))
        def _(v=v):
            w1c, w3c, w2c = weights[v]
            _wait_rows(xbuf.at[slot], sem.at[slot])
            for c in range(n_chunks):
                h_scr[:, c * LANES:(c + 1) * LANES] = _dense_chunk(xbuf, (slot,), c).astype(BF16)
            h = h_scr[...]
            a = jnp.dot(h, w1c[...], preferred_element_type=F32)
            b = jnp.dot(h, w3c[...], preferred_element_type=F32)
            hid = (a * jax.nn.sigmoid(a) * b).astype(BF16)
            obuf[slot] = jnp.dot(hid, w2c[...], preferred_element_type=F32)

    @pl.when(mode == MODE_ZERO)
    def _():
        obuf[slot] = jnp.zeros(obuf.shape[1:], F32)

    @pl.when(mode != MODE_IDLE)
    def _():
        for cp in _tm_write_copies(obuf.at[slot], yb_hbm, blk_ref[i] * tmb, osem.at[slot]):
            cp.start()

    @pl.when(i == last)
    def _():
        @pl.when(mode != MODE_IDLE)
        def _():
            wait_out(slot)

        @pl.when(jnp.logical_and(i >= 1, mode_ref[jnp.maximum(i - 1, 0)] != MODE_IDLE))
        def _():
            wait_out(nxt)


def _experts(xt, w1, w3, w2, counts, row_tok, tmb, n_blocks, layer):
    n_tok, n_chunks, _ = xt.shape
    dm = n_chunks * LANES
    de = w1.shape[3]
    assert dm % CAST_STEPS == 0 and de % CAST_STEPS == 0
    n_steps, sched = _expert_schedule(counts, tmb, n_blocks)
    wmap = lambda i, mode, blk, par, cexp, cidx, cast, rt: (layer, cexp[i], cidx[i], 0)
    bf16_slot = [pltpu.VMEM((dm, de), BF16), pltpu.VMEM((dm, de), BF16), pltpu.VMEM((de, dm), BF16)]
    return pl.pallas_call(
        functools.partial(_experts_kernel, tmb=tmb),
        grid_spec=pltpu.PrefetchScalarGridSpec(
            num_scalar_prefetch=7,
            grid=(n_steps,),
            in_specs=[
                pl.BlockSpec(memory_space=pl.ANY),
                pl.BlockSpec((None, None, dm // CAST_STEPS, de), wmap),
                pl.BlockSpec((None, None, dm // CAST_STEPS, de), wmap),
                pl.BlockSpec((None, None, de // CAST_STEPS, dm), wmap),
            ],
            out_specs=pl.BlockSpec(memory_space=pl.ANY),
            scratch_shapes=[
                pltpu.VMEM((2, tmb // SUBLANES, n_chunks, SUBLANES, LANES), F32),
                pltpu.VMEM((tmb, dm), BF16),
                pltpu.VMEM((2, tmb, dm), F32),
                *bf16_slot, *bf16_slot,
                pltpu.SemaphoreType.DMA((2,)), pltpu.SemaphoreType.DMA((2,))],
        ),
        out_shape=jax.ShapeDtypeStruct((n_blocks * tmb, n_chunks, LANES), F32),
        compiler_params=pltpu.CompilerParams(
            dimension_semantics=("arbitrary",), vmem_limit_bytes=VMEM_LIMIT),
        name="experts",
    )(*sched, row_tok, xt, w1, w3, w2)


def _combine_kernel(dest_ref, x_ref, route_ref, gf_ref, yb_hbm, o_ref, buf, sem, *, tm, n_tok, final):
    i = pl.program_id(0)
    n_chunks = yb_hbm.shape[1]
    dm = n_chunks * LANES

    def issue(step, slot):
        for k in range(TOP_K):
            _issue_rows(dest_ref, k * n_tok + step * tm, yb_hbm, buf.at[slot, k], sem.at[slot, k], tm,
                        spread=True)

    @pl.when(i == 0)
    def _():
        issue(0, 0)

    @pl.when(i + 1 < pl.num_programs(0))
    def _():
        issue(i + 1, (i + 1) % 2)

    slot = i % 2
    for k in range(TOP_K):
        _wait_rows(buf.at[slot, k], sem.at[slot, k])
    gate0 = route_ref[:, 2:3]
    gate1 = route_ref[:, 3:4]
    ss = jnp.zeros((tm, 1), F32)
    for c in range(n_chunks):
        cs = slice(c * LANES, (c + 1) * LANES)
        oc = (x_ref[:, cs] + gate0 * _dense_chunk(buf, (slot, 0), c)
              + gate1 * _dense_chunk(buf, (slot, 1), c))
        o_ref[:, cs] = oc
        if final:
            ss = ss + jnp.sum(oc * oc, axis=-1, keepdims=True)
    if final:
        rs = lax.rsqrt(ss * (1.0 / dm) + EPS)
        for c in range(n_chunks):
            cs = slice(c * LANES, (c + 1) * LANES)
            o_ref[:, cs] = o_ref[:, cs] * rs * gf_ref[:, cs]


def _combine(x2, route, yb, dest, gf, final):
    n_tok, dm = x2.shape
    n_chunks = dm // LANES
    tm = _tile(n_tok, COMBINE_ROWS)
    return pl.pallas_call(
        functools.partial(_combine_kernel, tm=tm, n_tok=n_tok, final=final),
        grid_spec=pltpu.PrefetchScalarGridSpec(
            num_scalar_prefetch=1,
            grid=(n_tok // tm,),
            in_specs=[
                pl.BlockSpec((tm, dm), lambda i, d: (i, 0)),
                pl.BlockSpec((tm, LANES), lambda i, d: (i, 0)),
                pl.BlockSpec((1, dm), lambda i, d: (0, 0)),
                pl.BlockSpec(memory_space=pl.ANY),
            ],
            out_specs=pl.BlockSpec((tm, dm), lambda i, d: (i, 0)),
            scratch_shapes=[pltpu.VMEM((2, TOP_K, tm // SUBLANES, n_chunks, SUBLANES, LANES), F32),
                            pltpu.SemaphoreType.DMA((2, TOP_K))],
        ),
        out_shape=jax.ShapeDtypeStruct((n_tok, dm), F32),
        compiler_params=pltpu.CompilerParams(
            dimension_semantics=("arbitrary",), vmem_limit_bytes=VMEM_LIMIT),
        name="combine",
    )(dest, x2, route, gf.reshape(1, dm), yb)


SCATTER_UNROLL = 32


def _row_tok_kernel(dest_ref, pad_hbm, row_tok_hbm, rt_smem, sem, *, n_tok):
    fill = pltpu.make_async_copy(pad_hbm, rt_smem, sem)
    fill.start()
    fill.wait()
    for k in range(TOP_K):
        def scatter(j, carry):
            for u in range(SCATTER_UNROLL):
                t = j * SCATTER_UNROLL + u
                rt_smem[dest_ref[k * n_tok + t]] = t
            return carry

        lax.fori_loop(0, n_tok // SCATTER_UNROLL, scatter, 0)
    out = pltpu.make_async_copy(rt_smem, row_tok_hbm, sem)
    out.start()
    out.wait()


def _row_tok(dest_kmajor, n_tok, n_rows):
    assert n_tok % SCATTER_UNROLL == 0
    pad_tok = jnp.arange(n_rows, dtype=jnp.int32) % n_tok
    return pl.pallas_call(
        functools.partial(_row_tok_kernel, n_tok=n_tok),
        in_specs=[pl.BlockSpec(memory_space=pltpu.SMEM), pl.BlockSpec(memory_space=pl.ANY)],
        out_specs=pl.BlockSpec(memory_space=pl.ANY),
        out_shape=jax.ShapeDtypeStruct((n_rows,), jnp.int32),
        scratch_shapes=[pltpu.SMEM((n_rows,), jnp.int32), pltpu.SemaphoreType.DMA],
        name="row_tok",
    )(dest_kmajor, pad_tok)


def _plan(route_t, counts, tmb, n_blocks):
    n_tok = route_t.shape[1]
    e = route_t[0:TOP_K].astype(jnp.int32)
    rank = route_t[4:4 + TOP_K].astype(jnp.int32)
    pcounts = (counts + tmb - 1) // tmb * tmb
    pends = jnp.cumsum(pcounts)
    pstarts = pends - pcounts
    dest_kmajor = (_lookup(pstarts, e) + rank).reshape(n_tok * TOP_K)
    row_tok = _row_tok(dest_kmajor, n_tok, n_blocks * tmb)
    return dest_kmajor, row_tok


def kernel(x, norm_mix_g, w_in, conv_w, w_out, norm_ffn_g, w_group, b_group, w_router, b_router,
           w_expert_gate, w_expert_up, w_expert_down, final_norm_g):
    bsz, seq, dm = x.shape
    depth = w_in.shape[0]
    n_tok = bsz * seq
    tmb = _tile(n_tok * TOP_K, EXPERT_ROWS)
    n_blocks = n_tok * TOP_K // tmb + N_EXPERTS
    x2 = x.reshape(n_tok, dm)
    w_in_b, w_out_b = w_in.astype(BF16), w_out.astype(BF16)
    for l in range(depth):
        z = _in_proj(x2, norm_mix_g[l], w_in_b, l)
        x2 = _mixer(z, x2, w_out_b, conv_w[l], bsz, seq, l)
        route, route_t, counts, ht = _router(x2, norm_ffn_g[l], w_group[l], b_group[l], w_router[l],
                                             b_router[l])
        dest, row_tok = _plan(route_t, counts, tmb, n_blocks)
        yb = _experts(ht, w_expert_gate, w_expert_up, w_expert_down, counts, row_tok, tmb, n_blocks, l)
        x2 = _combine(x2, route, yb, dest, final_norm_g, final=(l == depth - 1))
    return x2.reshape(bsz, seq, dm)
```
